```python
import math
import jax, jax.numpy as jnp
from jax import lax
import numpy as np

D_MODEL = 1024
BATCH = 8
SEQ = 2048
DEPTH = 2

DIFF_HEADS = 4
DIFF_QK_DIM = D_MODEL // 16
DIFF_V_DIM = 2 * DIFF_QK_DIM
DIFF_WIDTH = DIFF_HEADS * DIFF_V_DIM
DIFF_QK_COLS = DIFF_HEADS * 2 * DIFF_QK_DIM
Q_BLOCK = 128
ROPE_THETA = 10000.0
FOURIER_GROUPS = 4
FOURIER_GROUP_DIM = D_MODEL // 16
FOURIER_WIDTH = FOURIER_GROUPS * FOURIER_GROUP_DIM
GLA_HEADS = 4
GLA_V_DIM = D_MODEL // 16
GLA_K_DIM = GLA_V_DIM // 2
GLA_WIDTH = GLA_HEADS * GLA_V_DIM
GLA_K_COLS = GLA_HEADS * GLA_K_DIM
GLA_GATE_RANK = 16
GLA_GATE_TAU = 16.0
GLA_CHUNK = 64
IN_WIDTH = 3 * DIFF_QK_COLS + FOURIER_WIDTH + 2 * GLA_K_COLS + 2 * GLA_WIDTH + 2 * GLA_GATE_RANK
MIX_WIDTH = DIFF_WIDTH + FOURIER_WIDTH + GLA_WIDTH
FFN_HIDDEN = ((math.ceil(8 * D_MODEL / 3) + 255) // 256) * 256
DEEPNORM_ALPHA = (2 * DEPTH) ** 0.25
DEEPNORM_BETA = (8 * DEPTH) ** -0.25
LN_EPS = 1e-5

kernel_name = "hybrid_diffattn_fnet_gla_deepnorm_encoder"


def layer_norm(x, g, b):
    xf = x.astype(jnp.float32)
    mu = jnp.mean(xf, axis=-1, keepdims=True)
    var = jnp.mean(jnp.square(xf - mu), axis=-1, keepdims=True)
    y = (xf - mu) * lax.rsqrt(var + LN_EPS)
    return (y * g.astype(jnp.float32) + b.astype(jnp.float32)).astype(x.dtype)


def rms_norm(x, g):
    xf = x.astype(jnp.float32)
    y = xf * lax.rsqrt(jnp.mean(jnp.square(xf), axis=-1, keepdims=True) + LN_EPS)
    return (y * g.astype(jnp.float32)).astype(x.dtype)


def rope_tables(seq_len, dim):
    pos = jnp.arange(seq_len, dtype=jnp.float32)
    inv_freq = ROPE_THETA ** (-jnp.arange(0, dim, 2, dtype=jnp.float32) / dim)
    ang = pos[:, None] * inv_freq[None, :]
    return jnp.cos(ang), jnp.sin(ang)


def apply_rope(t, cos, sin):
    tf = t.astype(jnp.float32)
    half = tf.shape[-1] // 2
    t1, t2 = tf[..., :half], tf[..., half:]
    c = cos[None, :, None, None, :]
    s = sin[None, :, None, None, :]
    return jnp.concatenate([t1 * c - t2 * s, t2 * c + t1 * s], axis=-1).astype(t.dtype)


def split_in_proj(h):
    sizes = (DIFF_QK_COLS, DIFF_QK_COLS, DIFF_WIDTH, FOURIER_WIDTH,
             GLA_K_COLS, GLA_K_COLS, GLA_WIDTH, GLA_WIDTH, 2 * GLA_GATE_RANK)
    idx = np.cumsum(sizes)[:-1].tolist()
    return jnp.split(h, idx, axis=-1)


def diff_attention(q, k, v, lam_params, lam_init, g, cos, sin):
    B, S, _ = q.shape
    qh = apply_rope(q.reshape(B, S, DIFF_HEADS, 2, DIFF_QK_DIM), cos, sin)
    kh = apply_rope(k.reshape(B, S, DIFF_HEADS, 2, DIFF_QK_DIM), cos, sin)
    vh = v.reshape(B, S, DIFF_HEADS, DIFF_V_DIM)
    lp = lam_params.astype(jnp.float32)
    lam = jnp.exp(jnp.sum(lp[0] * lp[1])) - jnp.exp(jnp.sum(lp[2] * lp[3])) + lam_init
    scale = DIFF_QK_DIM ** -0.5
    nb = S // Q_BLOCK
    qb = jnp.moveaxis(qh.reshape(B, nb, Q_BLOCK, DIFF_HEADS, 2, DIFF_QK_DIM), 1, 0)

    def block(qi):
        s = jnp.einsum('bqhmd,bkhmd->bhmqk', qi, kh,
                       preferred_element_type=jnp.float32) * scale
        p = jax.nn.softmax(s, axis=-1)
        a = p[:, :, 0] - lam * p[:, :, 1]
        return jnp.einsum('bhqk,bkhe->bqhe', a.astype(vh.dtype), vh)

    o = lax.map(block, qb)
    o = jnp.moveaxis(o, 0, 1).reshape(B, S, DIFF_HEADS, DIFF_V_DIM)
    o = rms_norm(o, g) * (1.0 - lam_init)
    return o.reshape(B, S, DIFF_WIDTH)


def fourier_mix(u, w):
    B, S, _ = u.shape
    uf = u.astype(jnp.float32).reshape(B, S, FOURIER_GROUPS, FOURIER_GROUP_DIM)
    z = jnp.fft.fft2(uf, axes=(1, 3), norm='ortho').real
    y = jnp.einsum('bsgc,gce->bsge', z.astype(u.dtype), w)
    return y.reshape(B, S, FOURIER_WIDTH)


def gla_causal_chunked(q, k, v, g):
    B, H, S, dk = q.shape
    dv = v.shape[-1]
    C = GLA_CHUNK
    N = S // C
    q = q.reshape(B, H, N, C, dk)
    k = k.reshape(B, H, N, C, dk)
    v = v.reshape(B, H, N, C, dv)
    b = lax.cumsum(g.reshape(B, H, N, C, dk), axis=3)
    q_t = q * jnp.exp(b)
    k_t = k * jnp.exp(-b)
    mask = jnp.tril(jnp.ones((C, C), jnp.float32))
    a = jnp.einsum('bhncd,bhned->bhnce', q_t, k_t) * mask
    o_intra = jnp.einsum('bhnce,bhnev->bhncv', a, v)
    b_last = b[:, :, :, -1:, :]
    chunk_kv = jnp.einsum('bhncd,bhncv->bhndv', k * jnp.exp(b_last - b), v)
    chunk_decay = jnp.exp(b_last[:, :, :, 0, :])

    def step(state, inp):
        dec, kv = inp
        return dec[..., None] * state + kv, state

    init = jnp.zeros((B, H, dk, dv), jnp.float32)
    _, s_prev = lax.scan(step, init, (jnp.moveaxis(chunk_decay, 2, 0),
                                      jnp.moveaxis(chunk_kv, 2, 0)))
    s_prev = jnp.moveaxis(s_prev, 0, 2)
    o_inter = jnp.einsum('bhncd,bhndv->bhncv', q_t, s_prev)
    return (o_intra + o_inter).reshape(B, H, S, dv)


def gla_bidirectional(q, k, v, r, z, w2, b2, g):
    B, S, _ = q.shape

    def heads(t, d):
        return t.astype(jnp.float32).reshape(B, S, GLA_HEADS, d).transpose(0, 2, 1, 3)

    qh = heads(q, GLA_K_DIM) * GLA_K_DIM ** -0.5
    kh = heads(k, GLA_K_DIM)
    vh = heads(v, GLA_V_DIM)
    zz = z.astype(jnp.float32).reshape(B, S, 2, GLA_GATE_RANK)
    logit = jnp.einsum('bsdr,drk->dbsk', zz, w2.astype(jnp.float32)) \
        + b2.astype(jnp.float32)[:, None, None, :]
    log_a = jax.nn.log_sigmoid(logit) / GLA_GATE_TAU
    ga_f = heads(log_a[0], GLA_K_DIM)
    ga_b = heads(log_a[1], GLA_K_DIM)
    flip = lambda t: jnp.flip(t, axis=2)
    o_f = gla_causal_chunked(qh, kh, vh, ga_f)
    o_b = flip(gla_causal_chunked(flip(qh), flip(kh), flip(vh), flip(ga_b)))
    o = (o_f + o_b).transpose(0, 2, 1, 3)
    gate = jax.nn.silu(r.astype(jnp.float32)).reshape(B, S, GLA_HEADS, GLA_V_DIM)
    o = rms_norm(o, g) * gate
    return o.reshape(B, S, GLA_WIDTH).astype(q.dtype)


def hybrid_mixer(x, w_in, lam_params, lam_init, diff_g, fourier_w, gla_w2, gla_b2,
                 gla_g, w_out, cos, sin):
    h = jnp.einsum('bsd,de->bse', x, w_in)
    dq, dk, dv, fu, gq, gk, gv, gr, gz = split_in_proj(h)
    o_diff = diff_attention(dq, dk, dv, lam_params, lam_init, diff_g, cos, sin)
    o_four = fourier_mix(fu, fourier_w)
    o_gla = gla_bidirectional(gq, gk, gv, gr, gz, gla_w2, gla_b2, gla_g)
    o = jnp.concatenate([o_diff, o_four.astype(o_diff.dtype), o_gla.astype(o_diff.dtype)], axis=-1)
    return jnp.einsum('bse,ed->bsd', o, w_out)


def swiglu_ffn(x, w_gate, w_up, w_down):
    hg = jnp.einsum('bsd,df->bsf', x, w_gate)
    hu = jnp.einsum('bsd,df->bsf', x, w_up)
    return jnp.einsum('bsf,fd->bsd', jax.nn.silu(hg) * hu, w_down)


def setup_inputs(seed: int = 0) -> dict:
    key = jax.random.key(seed)
    ks = jax.random.split(key, 16)
    f32 = jnp.float32
    nrm = lambda k, shape: jax.random.normal(k, shape, f32)
    x = nrm(ks[0], (BATCH, SEQ, D_MODEL))
    col_scale = jnp.concatenate([
        jnp.ones((2 * DIFF_QK_COLS,), f32),
        jnp.full((DIFF_WIDTH,), DEEPNORM_BETA, f32),
        jnp.ones((FOURIER_WIDTH + 2 * GLA_K_COLS,), f32),
        jnp.full((GLA_WIDTH,), DEEPNORM_BETA, f32),
        jnp.ones((GLA_WIDTH + 2 * GLA_GATE_RANK,), f32)])
    w_in = nrm(ks[1], (DEPTH, D_MODEL, IN_WIDTH)) * (D_MODEL ** -0.5) * col_scale
    diff_lambda = nrm(ks[2], (DEPTH, 4, DIFF_QK_DIM)) * 0.1
    diff_norm_g = 1.0 + 0.02 * nrm(ks[3], (DEPTH, DIFF_V_DIM))
    fourier_w = nrm(ks[4], (DEPTH, FOURIER_GROUPS, FOURIER_GROUP_DIM, FOURIER_GROUP_DIM)) \
        * (FOURIER_GROUP_DIM ** -0.5)
    gla_gate_w2 = nrm(ks[5], (DEPTH, 2, GLA_GATE_RANK, GLA_K_COLS)) * (GLA_GATE_RANK ** -0.5)
    gla_gate_b2 = 0.1 * nrm(ks[6], (DEPTH, 2, GLA_K_COLS))
    gla_norm_g = 1.0 + 0.02 * nrm(ks[7], (DEPTH, GLA_V_DIM))
    w_out = nrm(ks[8], (DEPTH, MIX_WIDTH, D_MODEL)) * (MIX_WIDTH ** -0.5) * DEEPNORM_BETA
    ln1_g = 1.0 + 0.02 * nrm(ks[9], (DEPTH, D_MODEL))
    ln1_b = 0.02 * nrm(ks[10], (DEPTH, D_MODEL))
    ffn_w_gate = nrm(ks[11], (DEPTH, D_MODEL, FFN_HIDDEN)) * (D_MODEL ** -0.5)
    ffn_w_up = nrm(ks[12], (DEPTH, D_MODEL, FFN_HIDDEN)) * (D_MODEL ** -0.5)
    ffn_w_down = nrm(ks[13], (DEPTH, FFN_HIDDEN, D_MODEL)) * (FFN_HIDDEN ** -0.5) * DEEPNORM_BETA
    ln2_g = 1.0 + 0.02 * nrm(ks[14], (DEPTH, D_MODEL))
    ln2_b = 0.02 * nrm(ks[15], (DEPTH, D_MODEL))
    return {"x": x, "w_in": w_in, "diff_lambda": diff_lambda, "diff_norm_g": diff_norm_g,
            "fourier_w": fourier_w, "gla_gate_w2": gla_gate_w2, "gla_gate_b2": gla_gate_b2,
            "gla_norm_g": gla_norm_g, "w_out": w_out, "ln1_g": ln1_g, "ln1_b": ln1_b,
            "ffn_w_gate": ffn_w_gate, "ffn_w_up": ffn_w_up, "ffn_w_down": ffn_w_down,
            "ln2_g": ln2_g, "ln2_b": ln2_b}


def reference(x, w_in, diff_lambda, diff_norm_g, fourier_w, gla_gate_w2, gla_gate_b2,
              gla_norm_g, w_out, ln1_g, ln1_b, ffn_w_gate, ffn_w_up, ffn_w_down,
              ln2_g, ln2_b):
    cos, sin = rope_tables(x.shape[1], DIFF_QK_DIM)
    for l in range(DEPTH):
        lam_init = 0.8 - 0.6 * math.exp(-0.3 * l)
        m = hybrid_mixer(x, w_in[l], diff_lambda[l], lam_init, diff_norm_g[l], fourier_w[l],
                         gla_gate_w2[l], gla_gate_b2[l], gla_norm_g[l], w_out[l], cos, sin)
        x = layer_norm(DEEPNORM_ALPHA * x + m.astype(x.dtype), ln1_g[l], ln1_b[l])
        f = swiglu_ffn(x, ffn_w_gate[l], ffn_w_up[l], ffn_w_down[l])
        x = layer_norm(DEEPNORM_ALPHA * x + f.astype(x.dtype), ln2_g[l], ln2_b[l])
    return x
```

```python
import functools
import math

import jax
import jax.numpy as jnp
import numpy as np
from jax import lax
from jax.experimental import pallas as pl
from jax.experimental.pallas import tpu as pltpu

D_MODEL = 1024
DEPTH = 2
DIFF_HEADS = 4
DIFF_QK_DIM = 64
DIFF_V_DIM = 128
DIFF_WIDTH = 512
DIFF_QK_COLS = 512
ROPE_THETA = 10000.0
FOURIER_GROUPS = 4
FOURIER_GROUP_DIM = 64
FOURIER_WIDTH = 256
GLA_HEADS = 4
GLA_V_DIM = 64
GLA_K_DIM = 32
GLA_WIDTH = 256
GLA_K_COLS = 128
GLA_GATE_RANK = 16
GLA_GATE_TAU = 16.0
GLA_CHUNK = 64
IN_WIDTH = 2592
FFN_HIDDEN = 2816
DEEPNORM_ALPHA = (2 * DEPTH) ** 0.25
LN_EPS = 1e-5

OFF_DQ, OFF_DK, OFF_DV, OFF_FU = 0, 512, 1024, 1536
OFF_GQ, OFF_GV, OFF_GR, OFF_GZ = 1792, 2048, 2304, 2560

LANES = 128
VMEM_LIMIT = 56 * 1024 * 1024

BF16 = jnp.bfloat16
F32 = jnp.float32


def _dot(a, b):
    return jnp.dot(a, b, preferred_element_type=F32)


def _dot_nt(a, b):
    return lax.dot_general(a, b, (((1,), (1,)), ((), ())), preferred_element_type=F32)


def _params(n_grid_dims):
    return pltpu.CompilerParams(
        dimension_semantics=("arbitrary",) * n_grid_dims,
        vmem_limit_bytes=VMEM_LIMIT)


def _rope_slab(t, cos, sin_signed, first_half):
    swapped = jnp.where(first_half, pltpu.roll(t, 96, 1), pltpu.roll(t, 32, 1))
    return t * cos + swapped * sin_signed


def _inproj_kernel(x_ref, w_ref, cos_ref, sin_ref,
                   q_ref, k_ref, v_ref, fu_ref, gqk_ref, gv_ref, gr_ref, gz_ref):
    xb = x_ref[...].astype(BF16)
    cos = cos_ref[...]
    sin_signed = sin_ref[...]
    lane = lax.broadcasted_iota(jnp.int32, cos.shape, 1)
    first_half = (lane % DIFF_QK_DIM) < (DIFF_QK_DIM // 2)
    qk_scale = DIFF_QK_DIM ** -0.5
    for hd in range(DIFF_HEADS):
        lo = hd * LANES
        hq = _dot(xb, w_ref[:, OFF_DQ + lo:OFF_DQ + lo + LANES])
        q_ref[:, lo:lo + LANES] = (_rope_slab(hq, cos, sin_signed, first_half) * qk_scale).astype(BF16)
        hk = _dot(xb, w_ref[:, OFF_DK + lo:OFF_DK + lo + LANES])
        k_ref[:, lo:lo + LANES] = _rope_slab(hk, cos, sin_signed, first_half).astype(BF16)
    v_ref[...] = _dot(xb, w_ref[:, OFF_DV:OFF_FU]).astype(BF16)
    fu_ref[...] = _dot(xb, w_ref[:, OFF_FU:OFF_GQ]).astype(BF16)
    gqk_ref[...] = _dot(xb, w_ref[:, OFF_GQ:OFF_GV])
    gv_ref[...] = _dot(xb, w_ref[:, OFF_GV:OFF_GR])
    gr_ref[...] = _dot(xb, w_ref[:, OFF_GR:OFF_GZ])
    gz_ref[...] = _dot(xb, w_ref[:, OFF_GZ:IN_WIDTH])


def _in_proj(x2d, w_bf, cos128, sin128, seq, tm=512):
    m = x2d.shape[0]
    n_seq_tiles = seq // tm
    row = lambda i: (i, 0)
    const = lambda i: (0, 0)
    tab = lambda i: (i % n_seq_tiles, 0)
    out_shapes = (
        jax.ShapeDtypeStruct((m, DIFF_QK_COLS), BF16),
        jax.ShapeDtypeStruct((m, DIFF_QK_COLS), BF16),
        jax.ShapeDtypeStruct((m, DIFF_WIDTH), BF16),
        jax.ShapeDtypeStruct((m, FOURIER_WIDTH), BF16),
        jax.ShapeDtypeStruct((m, 2 * GLA_K_COLS), F32),
        jax.ShapeDtypeStruct((m, GLA_WIDTH), F32),
        jax.ShapeDtypeStruct((m, GLA_WIDTH), F32),
        jax.ShapeDtypeStruct((m, 2 * GLA_GATE_RANK), F32),
    )
    return pl.pallas_call(
        _inproj_kernel,
        grid=(m // tm,),
        in_specs=[
            pl.BlockSpec((tm, D_MODEL), row),
            pl.BlockSpec((D_MODEL, IN_WIDTH), const),
            pl.BlockSpec((tm, LANES), tab),
            pl.BlockSpec((tm, LANES), tab),
        ],
        out_specs=tuple(pl.BlockSpec((tm, s.shape[1]), row) for s in out_shapes),
        out_shape=out_shapes,
        compiler_params=_params(1),
        name="in_proj",
    )(x2d, w_bf, cos128, sin128)


def _attn_kernel(lam_ref, g_ref, q_ref, k_ref, v_ref, o_ref, *, lam_init):
    lp = lam_ref[...]
    lam = (jnp.exp(jnp.sum(lp[0:1] * lp[1:2], axis=1, keepdims=True))
           - jnp.exp(jnp.sum(lp[2:3] * lp[3:4], axis=1, keepdims=True)) + lam_init)
    q = q_ref[0]
    k = k_ref[0]
    v = v_ref[0]
    lane = lax.broadcasted_iota(jnp.int32, q.shape, 1)
    zero = jnp.zeros_like(q)

    def softmax_map(in_map):
        s = _dot_nt(jnp.where(in_map, q, zero), k)
        p = jnp.exp(s - jnp.max(s, axis=1, keepdims=True))
        return p, jnp.sum(p, axis=1, keepdims=True)

    p0, l0 = softmax_map(lane < DIFF_QK_DIM)
    p1, l1 = softmax_map(lane >= DIFF_QK_DIM)
    a = p0 * (1.0 / l0) - p1 * (lam / l1)
    o = _dot(a.astype(BF16), v)
    ms = jnp.mean(o * o, axis=1, keepdims=True)
    y = o * lax.rsqrt(ms + LN_EPS) * g_ref[...] * (1.0 - lam_init)
    o_ref[0] = y.astype(BF16)


def _diff_attention(q, k, v, lam_params, g, lam_init, tq=256):
    b, s, _ = q.shape
    return pl.pallas_call(
        functools.partial(_attn_kernel, lam_init=lam_init),
        grid=(b, DIFF_HEADS, s // tq),
        in_specs=[
            pl.BlockSpec((4, DIFF_QK_DIM), lambda bi, h, i: (0, 0)),
            pl.BlockSpec((1, DIFF_V_DIM), lambda bi, h, i: (0, 0)),
            pl.BlockSpec((1, tq, LANES), lambda bi, h, i: (bi, i, h)),
            pl.BlockSpec((1, s, LANES), lambda bi, h, i: (bi, 0, h)),
            pl.BlockSpec((1, s, LANES), lambda bi, h, i: (bi, 0, h)),
        ],
        out_specs=pl.BlockSpec((1, tq, LANES), lambda bi, h, i: (bi, i, h)),
        out_shape=jax.ShapeDtypeStruct((b, s, DIFF_WIDTH), BF16),
        compiler_params=_params(3),
        name="diff_attn",
    )(lam_params, g, q, k, v)


def _fourier_kernel(u_ref, cc_ref, sc_ref, f_ref, w_ref, o_ref, ab_ref, *, scale):
    s = u_ref.shape[1]

    @pl.when(pl.program_id(1) == 0)
    def _():
        u = u_ref[0]
        ab_ref[0:s, :] = _dot(u, cc_ref[...]).astype(BF16)
        ab_ref[s:2 * s, :] = _dot(u, sc_ref[...]).astype(BF16)

    z = _dot(f_ref[...], ab_ref[...]) * scale
    o_ref[0] = _dot(z.astype(BF16), w_ref[...]).astype(BF16)


def _fourier(u, cc_bd, sc_bd, f_mat, w_bd, tk=512):
    b, s, _ = u.shape
    scale = 1.0 / math.sqrt(s * FOURIER_GROUP_DIM)
    const2 = lambda bi, i: (0, 0)
    return pl.pallas_call(
        functools.partial(_fourier_kernel, scale=scale),
        grid=(b, s // tk),
        in_specs=[
            pl.BlockSpec((1, s, FOURIER_WIDTH), lambda bi, i: (bi, 0, 0)),
            pl.BlockSpec((FOURIER_WIDTH, FOURIER_WIDTH), const2),
            pl.BlockSpec((FOURIER_WIDTH, FOURIER_WIDTH), const2),
            pl.BlockSpec((tk, 2 * s), lambda bi, i: (i, 0)),
            pl.BlockSpec((FOURIER_WIDTH, FOURIER_WIDTH), const2),
        ],
        out_specs=pl.BlockSpec((1, tk, FOURIER_WIDTH), lambda bi, i: (bi, i, 0)),
        out_shape=jax.ShapeDtypeStruct((b, s, FOURIER_WIDTH), BF16),
        scratch_shapes=[pltpu.VMEM((2 * s, FOURIER_WIDTH), BF16)],
        compiler_params=_params(2),
        name="fourier",
    )(u, cc_bd, sc_bd, f_mat, w_bd)


def _split3(g):
    hi = g.astype(BF16)
    r1 = g - hi.astype(F32)
    mid = r1.astype(BF16)
    lo = (r1 - mid.astype(F32)).astype(BF16)
    return hi, mid, lo


def _gla_kernel(qk_ref, v_ref, r_ref, z_ref, w2_ref, b2_ref, g_ref, o_ref,
                la_ref, vt_ref, acc_ref, st_ref):
    s = v_ref.shape[1]
    c = GLA_CHUNK
    pair = 2 * c
    n_pairs = s // pair
    kc = GLA_K_COLS

    logit = _dot(z_ref[0].astype(BF16), w2_ref[...]) + b2_ref[...]
    la_ref[...] = (jnp.minimum(logit, 0.0) - jnp.log1p(jnp.exp(-jnp.abs(logit)))) * (1.0 / GLA_GATE_TAU)
    vt_ref[...] = v_ref[0].T.astype(BF16)

    row_i = lax.broadcasted_iota(jnp.int32, (c, c), 0)
    col_i = lax.broadcasted_iota(jnp.int32, (c, c), 1)
    r_hk = lax.broadcasted_iota(jnp.int32, (GLA_HEADS * c, kc), 0) // c
    l_hk = lax.broadcasted_iota(jnp.int32, (GLA_HEADS * c, kc), 1) // GLA_K_DIM
    blk_k = r_hk == l_hk
    r_hv = lax.broadcasted_iota(jnp.int32, (GLA_HEADS * c, GLA_WIDTH), 0) // c
    l_hv = lax.broadcasted_iota(jnp.int32, (GLA_HEADS * c, GLA_WIDTH), 1) // GLA_V_DIM
    blk_v = r_hv == l_hv
    qi = lax.broadcasted_iota(jnp.int32, (c, GLA_HEADS * c), 0)
    kj = lax.broadcasted_iota(jnp.int32, (c, GLA_HEADS * c), 1) % c
    q_scale = GLA_K_DIM ** -0.5

    def direction(backward):
        if backward:
            cum_mat = (col_i >= row_i).astype(BF16)
            causal = kj >= qi
            gate_lo = kc
        else:
            cum_mat = (col_i <= row_i).astype(BF16)
            causal = kj <= qi
            gate_lo = 0
        st_ref[...] = jnp.zeros_like(st_ref)

        def pair_step(t, carry):
            p = (n_pairs - 1 - t) if backward else t
            rows = pl.ds(pl.multiple_of(p * pair, pair), pair)
            q2 = qk_ref[0, rows, 0:kc] * q_scale
            k2 = qk_ref[0, rows, kc:2 * kc]
            g2 = la_ref[rows, gate_lo:gate_lo + kc]
            v2 = v_ref[0, rows, :]
            vt2 = vt_ref[:, rows]
            outs = [None, None]
            for ci in ((1, 0) if backward else (0, 1)):
                sl = slice(ci * c, (ci + 1) * c)
                hi, mid, lo = _split3(g2[sl])
                c3 = _dot(cum_mat, jnp.concatenate([hi, mid, lo], axis=1))
                bcum = c3[:, 0:kc] + c3[:, kc:2 * kc] + c3[:, 2 * kc:3 * kc]
                eb = jnp.exp(bcum)
                qt = (q2[sl] * eb).astype(BF16)
                kt = k2[sl] * jnp.exp(-bcum)
                decay = eb[0:1] if backward else eb[c - 1:c]
                kt_bd = jnp.where(blk_k, jnp.concatenate([kt] * GLA_HEADS, axis=0), 0.0).astype(BF16)
                a = _dot_nt(qt, kt_bd)
                a = jnp.where(causal, a, 0.0).astype(BF16)
                v_bd = jnp.where(blk_v, jnp.concatenate([v2[sl]] * GLA_HEADS, axis=0), 0.0).astype(BF16)
                st = st_ref[...]
                o_c = _dot(a, v_bd) + _dot_nt(qt, st.astype(BF16))
                outs[ci] = o_c
                kt_pair = jnp.concatenate(
                    [kt if ci == 0 else jnp.zeros_like(kt), kt if ci == 1 else jnp.zeros_like(kt)],
                    axis=0).astype(BF16)
                kv = _dot(vt2, kt_pair)
                st_ref[...] = (st + jnp.where(blk_k, kv, 0.0)) * decay
            o2 = jnp.concatenate(outs, axis=0)
            if backward:
                acc_ref[rows, :] += o2
            else:
                acc_ref[rows, :] = o2
            return carry

        lax.fori_loop(0, n_pairs, pair_step, 0)

    direction(False)
    direction(True)

    o = acc_ref[...]
    o2 = o * o
    o2_hi = o2.astype(BF16)
    o2_lo = (o2 - o2_hi.astype(F32)).astype(BF16)
    gi = lax.broadcasted_iota(jnp.int32, (GLA_WIDTH, GLA_WIDTH), 0) // GLA_V_DIM
    gj = lax.broadcasted_iota(jnp.int32, (GLA_WIDTH, GLA_WIDTH), 1) // GLA_V_DIM
    ones_bd = (gi == gj).astype(BF16)
    ms = (_dot(o2_hi, ones_bd) + _dot(o2_lo, ones_bd)) * (1.0 / GLA_V_DIM)
    r = r_ref[0]
    gate = r * (1.0 / (1.0 + jnp.exp(-r)))
    o_ref[0] = (o * lax.rsqrt(ms + LN_EPS) * g_ref[...] * gate).astype(BF16)


def _gla(gqk, gv, gr, gz, w2_bd, b2_cat, g_tiled):
    b, s, _ = gv.shape
    per_b = lambda bi: (bi, 0, 0)
    const = lambda bi: (0, 0)
    return pl.pallas_call(
        _gla_kernel,
        grid=(b,),
        in_specs=[
            pl.BlockSpec((1, s, 2 * GLA_K_COLS), per_b),
            pl.BlockSpec((1, s, GLA_WIDTH), per_b),
            pl.BlockSpec((1, s, GLA_WIDTH), per_b),
            pl.BlockSpec((1, s, 2 * GLA_GATE_RANK), per_b),
            pl.BlockSpec((2 * GLA_GATE_RANK, 2 * GLA_K_COLS), const),
            pl.BlockSpec((1, 2 * GLA_K_COLS), const),
            pl.BlockSpec((1, GLA_WIDTH), const),
        ],
        out_specs=pl.BlockSpec((1, s, GLA_WIDTH), per_b),
        out_shape=jax.ShapeDtypeStruct((b, s, GLA_WIDTH), BF16),
        scratch_shapes=[
            pltpu.VMEM((s, 2 * GLA_K_COLS), F32),
            pltpu.VMEM((GLA_WIDTH, s), BF16),
            pltpu.VMEM((s, GLA_WIDTH), F32),
            pltpu.VMEM((GLA_WIDTH, GLA_K_COLS), F32),
        ],
        compiler_params=_params(1),
        name="gla",
    )(gqk, gv, gr, gz, w2_bd, b2_cat, g_tiled)


def _layer_norm(y, g, b):
    mu = jnp.mean(y, axis=1, keepdims=True)
    d = y - mu
    var = jnp.mean(d * d, axis=1, keepdims=True)
    return d * lax.rsqrt(var + LN_EPS) * g + b


def _outproj_kernel(od_ref, of_ref, og_ref, x_ref, w_ref, g_ref, b_ref, o_ref):
    m = (_dot(od_ref[...], w_ref[0:DIFF_WIDTH, :])
         + _dot(of_ref[...], w_ref[DIFF_WIDTH:DIFF_WIDTH + FOURIER_WIDTH, :])
         + _dot(og_ref[...], w_ref[DIFF_WIDTH + FOURIER_WIDTH:, :]))
    o_ref[...] = _layer_norm(DEEPNORM_ALPHA * x_ref[...] + m, g_ref[...], b_ref[...])


def _out_proj(o_diff, o_four, o_gla, x2d, w_bf, g, b, tm=512):
    m = x2d.shape[0]
    row = lambda i: (i, 0)
    const = lambda i: (0, 0)
    return pl.pallas_call(
        _outproj_kernel,
        grid=(m // tm,),
        in_specs=[
            pl.BlockSpec((tm, DIFF_WIDTH), row),
            pl.BlockSpec((tm, FOURIER_WIDTH), row),
            pl.BlockSpec((tm, GLA_WIDTH), row),
            pl.BlockSpec((tm, D_MODEL), row),
            pl.BlockSpec((D_MODEL, D_MODEL), const),
            pl.BlockSpec((1, D_MODEL), const),
            pl.BlockSpec((1, D_MODEL), const),
        ],
        out_specs=pl.BlockSpec((tm, D_MODEL), row),
        out_shape=jax.ShapeDtypeStruct((m, D_MODEL), F32),
        compiler_params=_params(1),
        name="out_proj",
    )(o_diff, o_four, o_gla, x2d, w_bf, g, b)


FFN_CHUNK = 256


def _ffn_kernel(x_ref, wg_ref, wu_ref, wd_ref, g_ref, b_ref, o_ref, acc_ref):
    x = x_ref[...]
    xb = x.astype(BF16)
    for ci in range(FFN_HIDDEN // FFN_CHUNK):
        cols = slice(ci * FFN_CHUNK, (ci + 1) * FFN_CHUNK)
        hg = _dot(xb, wg_ref[:, cols])
        hu = _dot(xb, wu_ref[:, cols])
        act = (hg * (1.0 / (1.0 + jnp.exp(-hg))) * hu).astype(BF16)
        part = _dot(act, wd_ref[cols, :])
        if ci == 0:
            acc_ref[...] = part
        else:
            acc_ref[...] += part
    o_ref[...] = _layer_norm(DEEPNORM_ALPHA * x + acc_ref[...], g_ref[...], b_ref[...])


def _ffn(x2d, wg_bf, wu_bf, wd_bf, g, b, tm=512):
    m = x2d.shape[0]
    row = lambda i: (i, 0)
    const = lambda i: (0, 0)
    return pl.pallas_call(
        _ffn_kernel,
        grid=(m // tm,),
        in_specs=[
            pl.BlockSpec((tm, D_MODEL), row),
            pl.BlockSpec((D_MODEL, FFN_HIDDEN), const),
            pl.BlockSpec((D_MODEL, FFN_HIDDEN), const),
            pl.BlockSpec((FFN_HIDDEN, D_MODEL), const),
            pl.BlockSpec((1, D_MODEL), const),
            pl.BlockSpec((1, D_MODEL), const),
        ],
        out_specs=pl.BlockSpec((tm, D_MODEL), row),
        out_shape=jax.ShapeDtypeStruct((m, D_MODEL), F32),
        scratch_shapes=[pltpu.VMEM((tm, D_MODEL), F32)],
        compiler_params=_params(1),
        name="ffn",
    )(x2d, wg_bf, wu_bf, wd_bf, g, b)


def _rope_tables(seq):
    half = DIFF_QK_DIM // 2
    pos = jnp.arange(seq, dtype=F32)
    inv_freq = ROPE_THETA ** (-jnp.arange(0, DIFF_QK_DIM, 2, dtype=F32) / DIFF_QK_DIM)
    ang = pos[:, None] * inv_freq[None, :]
    cos, sin = jnp.cos(ang), jnp.sin(ang)
    reps = LANES // DIFF_QK_DIM
    cos128 = jnp.tile(jnp.concatenate([cos, cos], axis=1), (1, reps))
    sin128 = jnp.tile(jnp.concatenate([-sin, sin], axis=1), (1, reps))
    assert cos128.shape == (seq, LANES) and half * 2 == DIFF_QK_DIM
    return cos128, sin128


def _dft_tables(seq):
    k = np.arange(seq, dtype=np.int64)
    ang = 2.0 * np.pi * ((k[:, None] * k[None, :]) % seq).astype(np.float64) / seq
    f_mat = np.concatenate([np.cos(ang), -np.sin(ang)], axis=1)
    c = np.arange(FOURIER_GROUP_DIM, dtype=np.int64)
    ang_c = 2.0 * np.pi * ((c[:, None] * c[None, :]) % FOURIER_GROUP_DIM).astype(np.float64) / FOURIER_GROUP_DIM
    eye = np.eye(FOURIER_GROUPS)
    cc_bd = np.kron(eye, np.cos(ang_c))
    sc_bd = np.kron(eye, np.sin(ang_c))
    return tuple(jnp.asarray(t, dtype=F32).astype(BF16) for t in (f_mat, cc_bd, sc_bd))


def _block_diag(blocks):
    n = len(blocks)
    rows = []
    for i, blk in enumerate(blocks):
        rows.append(jnp.concatenate(
            [blk if j == i else jnp.zeros((blk.shape[0], blocks[j].shape[1]), blk.dtype) for j in range(n)],
            axis=1))
    return jnp.concatenate(rows, axis=0)


def kernel(x, w_in, diff_lambda, diff_norm_g, fourier_w, gla_gate_w2, gla_gate_b2, gla_norm_g, w_out,
           ln1_g, ln1_b, ffn_w_gate, ffn_w_up, ffn_w_down, ln2_g, ln2_b):
    b, s, d = x.shape
    m = b * s
    cos128, sin128 = _rope_tables(s)
    f_mat, cc_bd, sc_bd = _dft_tables(s)
    x2d = x.reshape(m, d)
    for l in range(DEPTH):
        lam_init = 0.8 - 0.6 * math.exp(-0.3 * l)
        q, k, v, fu, gqk, gv, gr, gz = _in_proj(x2d, w_in[l].astype(BF16), cos128, sin128, s)
        o_diff = _diff_attention(
            q.reshape(b, s, -1), k.reshape(b, s, -1), v.reshape(b, s, -1),
            diff_lambda[l], diff_norm_g[l].reshape(1, -1), lam_init)
        w_four = _block_diag([fourier_w[l, g] for g in range(FOURIER_GROUPS)]).astype(BF16)
        o_four = _fourier(fu.reshape(b, s, -1), cc_bd, sc_bd, f_mat, w_four)
        w2_bd = _block_diag([gla_gate_w2[l, 0], gla_gate_w2[l, 1]]).astype(BF16)
        b2_cat = gla_gate_b2[l].reshape(1, -1)
        g_gla = jnp.tile(gla_norm_g[l], GLA_HEADS).reshape(1, -1)
        o_gla = _gla(gqk.reshape(b, s, -1), gv.reshape(b, s, -1), gr.reshape(b, s, -1),
                     gz.reshape(b, s, -1), w2_bd, b2_cat, g_gla)
        x2d = _out_proj(o_diff.reshape(m, -1), o_four.reshape(m, -1), o_gla.reshape(m, -1), x2d,
                        w_out[l].astype(BF16), ln1_g[l].reshape(1, -1), ln1_b[l].reshape(1, -1))
        x2d = _ffn(x2d, ffn_w_gate[l].astype(BF16), ffn_w_up[l].astype(BF16), ffn_w_down[l].astype(BF16),
                   ln2_g[l].reshape(1, -1), ln2_b[l].reshape(1, -1))
    return x2d.reshape(b, s, d)
```

```python
import functools
import math

import jax
import jax.numpy as jnp
import numpy as np
from jax import lax
from jax.experimental import pallas as pl
from jax.experimental.pallas import tpu as pltpu

D_MODEL = 1024
DEPTH = 2
DIFF_HEADS = 4
DIFF_QK_DIM = 64
DIFF_V_DIM = 128
DIFF_WIDTH = 512
DIFF_QK_COLS = 512
ROPE_THETA = 10000.0
FOURIER_GROUPS = 4
FOURIER_GROUP_DIM = 64
FOURIER_WIDTH = 256
GLA_HEADS = 4
GLA_V_DIM = 64
GLA_K_DIM = 32
GLA_WIDTH = 256
GLA_K_COLS = 128
GLA_GATE_RANK = 16
GLA_GATE_TAU = 16.0
GLA_CHUNK = 64
IN_WIDTH = 2592
FFN_HIDDEN = 2816
DEEPNORM_ALPHA = (2 * DEPTH) ** 0.25
LN_EPS = 1e-5

OFF_DQ, OFF_DK, OFF_DV, OFF_FU = 0, 512, 1024, 1536
OFF_GQ, OFF_GV, OFF_GR, OFF_GZ = 1792, 2048, 2304, 2560

LANES = 128
VMEM_LIMIT = 56 * 1024 * 1024

BF16 = jnp.bfloat16
F32 = jnp.float32


def _dot(a, b):
    return jnp.dot(a, b, preferred_element_type=F32)


def _dot_nt(a, b):
    return lax.dot_general(a, b, (((1,), (1,)), ((), ())), preferred_element_type=F32)


def _params(n_grid_dims):
    return pltpu.CompilerParams(
        dimension_semantics=("arbitrary",) * n_grid_dims,
        vmem_limit_bytes=VMEM_LIMIT)


def _rope_slab(t, cos, sin_signed, first_half):
    swapped = jnp.where(first_half, pltpu.roll(t, 96, 1), pltpu.roll(t, 32, 1))
    return t * cos + swapped * sin_signed


def _inproj_kernel(x_ref, w_ref, cos_ref, sin_ref,
                   q_ref, k_ref, v_ref, fu_ref, gqk_ref, gv_ref, gr_ref, gz_ref):
    xb = x_ref[...].astype(BF16)
    cos = cos_ref[...]
    sin_signed = sin_ref[...]
    lane = lax.broadcasted_iota(jnp.int32, cos.shape, 1)
    first_half = (lane % DIFF_QK_DIM) < (DIFF_QK_DIM // 2)
    qk_scale = DIFF_QK_DIM ** -0.5
    for hd in range(DIFF_HEADS):
        lo = hd * LANES
        hq = _dot(xb, w_ref[:, OFF_DQ + lo:OFF_DQ + lo + LANES])
        q_ref[0, hd] = (_rope_slab(hq, cos, sin_signed, first_half) * qk_scale).astype(BF16)
        hk = _dot(xb, w_ref[:, OFF_DK + lo:OFF_DK + lo + LANES])
        k_ref[0, hd] = _rope_slab(hk, cos, sin_signed, first_half).astype(BF16)
        v_ref[0, hd] = _dot(xb, w_ref[:, OFF_DV + lo:OFF_DV + lo + LANES]).astype(BF16)
    fu_ref[...] = _dot(xb, w_ref[:, OFF_FU:OFF_GQ]).astype(BF16)
    gqk_ref[...] = _dot(xb, w_ref[:, OFF_GQ:OFF_GV])
    gv_ref[...] = _dot(xb, w_ref[:, OFF_GV:OFF_GR])
    gr_ref[...] = _dot(xb, w_ref[:, OFF_GR:OFF_GZ])
    gz_ref[...] = _dot(xb, w_ref[:, OFF_GZ:IN_WIDTH])


def _in_proj(x2d, w_bf, cos128, sin128, seq, tm=512):
    m = x2d.shape[0]
    n_seq_tiles = seq // tm
    row = lambda i: (i, 0)
    const = lambda i: (0, 0)
    tab = lambda i: (i % n_seq_tiles, 0)
    heads_shape = jax.ShapeDtypeStruct((m // seq, DIFF_HEADS, seq, LANES), BF16)
    heads_spec = pl.BlockSpec((1, DIFF_HEADS, tm, LANES), lambda i: (i // n_seq_tiles, 0, i % n_seq_tiles, 0))
    out_shapes = (
        heads_shape, heads_shape, heads_shape,
        jax.ShapeDtypeStruct((m, FOURIER_WIDTH), BF16),
        jax.ShapeDtypeStruct((m, 2 * GLA_K_COLS), F32),
        jax.ShapeDtypeStruct((m, GLA_WIDTH), F32),
        jax.ShapeDtypeStruct((m, GLA_WIDTH), F32),
        jax.ShapeDtypeStruct((m, 2 * GLA_GATE_RANK), F32),
    )
    return pl.pallas_call(
        _inproj_kernel,
        grid=(m // tm,),
        in_specs=[
            pl.BlockSpec((tm, D_MODEL), row),
            pl.BlockSpec((D_MODEL, IN_WIDTH), const),
            pl.BlockSpec((tm, LANES), tab),
            pl.BlockSpec((tm, LANES), tab),
        ],
        out_specs=(heads_spec,) * 3 + tuple(pl.BlockSpec((tm, s.shape[1]), row) for s in out_shapes[3:]),
        out_shape=out_shapes,
        compiler_params=_params(1),
        name="in_proj",
    )(x2d, w_bf, cos128, sin128)


ATTN_TQ = 256


def _attn_kernel(lam_ref, g_ref, q_ref, k_ref, v_ref, o_ref, s0_ref, s1_ref, a0_ref, a1_ref, *, lam_init):
    tq = ATTN_TQ
    seq = k_ref.shape[2]
    tiles = seq // tq
    n_units = DIFF_HEADS * tiles
    lp = lam_ref[...]
    lam = (jnp.exp(jnp.sum(lp[0:1] * lp[1:2], axis=1, keepdims=True))
           - jnp.exp(jnp.sum(lp[2:3] * lp[3:4], axis=1, keepdims=True)) + lam_init)
    gain = g_ref[...] * (1.0 - lam_init)
    lane = lax.broadcasted_iota(jnp.int32, (tq, LANES), 1)
    first_map = lane < DIFF_QK_DIM

    def head_rows(u):
        h = u // tiles
        return h, pl.ds(pl.multiple_of((u % tiles) * tq, tq), tq)

    def scores(u, s_ref):
        h, rows = head_rows(u)
        q = q_ref[0, h, rows, :]
        k = k_ref[0, h]
        zero = jnp.zeros_like(q)
        s_ref[0] = _dot_nt(jnp.where(first_map, q, zero), k)
        s_ref[1] = _dot_nt(jnp.where(first_map, zero, q), k)

    def softmaxes(s_ref, a_ref):
        def one(m):
            s = s_ref[m]
            p = jnp.exp(s - jnp.max(s, axis=1, keepdims=True))
            return p, jnp.sum(p, axis=1, keepdims=True)
        p0, l0 = one(0)
        p1, l1 = one(1)
        a_ref[...] = (p0 * (1.0 / l0) - p1 * (lam / l1)).astype(BF16)

    def weighted_values(u, a_ref):
        h, rows = head_rows(u)
        o = _dot(a_ref[...], v_ref[0, h])
        ms = jnp.mean(o * o, axis=1, keepdims=True)
        o_ref[0, h, rows, :] = (o * lax.rsqrt(ms + LN_EPS) * gain).astype(BF16)

    def pair(j, first=False, last=False):
        u0 = 2 * j
        scores(u0 + 1, s1_ref)
        softmaxes(s0_ref, a0_ref)
        if not first:
            weighted_values(u0 - 1, a1_ref)
        if not last:
            scores(u0 + 2, s0_ref)
        softmaxes(s1_ref, a1_ref)
        weighted_values(u0, a0_ref)

    n_pairs = n_units // 2
    scores(0, s0_ref)
    pair(0, first=True)
    lax.fori_loop(1, n_pairs - 1, lambda j, c: (pair(j), c)[1], 0)
    pair(n_pairs - 1, last=True)
    weighted_values(n_units - 1, a1_ref)


def _diff_attention(q, k, v, lam_params, g, lam_init):
    b, _, s, _ = q.shape
    per_b = pl.BlockSpec((1, DIFF_HEADS, s, LANES), lambda bi: (bi, 0, 0, 0))
    return pl.pallas_call(
        functools.partial(_attn_kernel, lam_init=lam_init),
        grid=(b,),
        in_specs=[
            pl.BlockSpec((4, DIFF_QK_DIM), lambda bi: (0, 0)),
            pl.BlockSpec((1, DIFF_V_DIM), lambda bi: (0, 0)),
            per_b, per_b, per_b,
        ],
        out_specs=per_b,
        out_shape=jax.ShapeDtypeStruct((b, DIFF_HEADS, s, LANES), BF16),
        scratch_shapes=[
            pltpu.VMEM((2, ATTN_TQ, s), F32), pltpu.VMEM((2, ATTN_TQ, s), F32),
            pltpu.VMEM((ATTN_TQ, s), BF16), pltpu.VMEM((ATTN_TQ, s), BF16),
        ],
        compiler_params=_params(1),
        name="diff_attn",
    )(lam_params, g, q, k, v)


def _fourier_kernel(u_ref, cc_ref, sc_ref, f_ref, w_ref, o_ref, ab_ref, *, scale):
    s = u_ref.shape[1]

    @pl.when(pl.program_id(1) == 0)
    def _():
        u = u_ref[0]
        ab_ref[0:s, :] = _dot(u, cc_ref[...]).astype(BF16)
        ab_ref[s:2 * s, :] = _dot(u, sc_ref[...]).astype(BF16)

    z = _dot(f_ref[...], ab_ref[...]) * scale
    o_ref[0] = _dot(z.astype(BF16), w_ref[...]).astype(BF16)


def _fourier(u, cc_bd, sc_bd, f_mat, w_bd, tk=512):
    b, s, _ = u.shape
    scale = 1.0 / math.sqrt(s * FOURIER_GROUP_DIM)
    const2 = lambda bi, i: (0, 0)
    return pl.pallas_call(
        functools.partial(_fourier_kernel, scale=scale),
        grid=(b, s // tk),
        in_specs=[
            pl.BlockSpec((1, s, FOURIER_WIDTH), lambda bi, i: (bi, 0, 0)),
            pl.BlockSpec((FOURIER_WIDTH, FOURIER_WIDTH), const2),
            pl.BlockSpec((FOURIER_WIDTH, FOURIER_WIDTH), const2),
            pl.BlockSpec((tk, 2 * s), lambda bi, i: (i, 0)),
            pl.BlockSpec((FOURIER_WIDTH, FOURIER_WIDTH), const2),
        ],
        out_specs=pl.BlockSpec((1, tk, FOURIER_WIDTH), lambda bi, i: (bi, i, 0)),
        out_shape=jax.ShapeDtypeStruct((b, s, FOURIER_WIDTH), BF16),
        scratch_shapes=[pltpu.VMEM((2 * s, FOURIER_WIDTH), BF16)],
        compiler_params=_params(2),
        name="fourier",
    )(u, cc_bd, sc_bd, f_mat, w_bd)


def _split3(g):
    hi = g.astype(BF16)
    r1 = g - hi.astype(F32)
    mid = r1.astype(BF16)
    lo = (r1 - mid.astype(F32)).astype(BF16)
    return hi, mid, lo


def _gla_kernel(qk_ref, v_ref, r_ref, z_ref, w2_ref, b2_ref, g_ref, o_ref,
                la_ref, vt_ref, acc_ref, st_ref):
    s = v_ref.shape[1]
    c = GLA_CHUNK
    pair = 2 * c
    n_pairs = s // pair
    kc = GLA_K_COLS

    logit = _dot(z_ref[0].astype(BF16), w2_ref[...]) + b2_ref[...]
    la_ref[...] = (jnp.minimum(logit, 0.0) - jnp.log1p(jnp.exp(-jnp.abs(logit)))) * (1.0 / GLA_GATE_TAU)
    vt_ref[...] = v_ref[0].T.astype(BF16)

    row_i = lax.broadcasted_iota(jnp.int32, (c, c), 0)
    col_i = lax.broadcasted_iota(jnp.int32, (c, c), 1)
    r_hk = lax.broadcasted_iota(jnp.int32, (GLA_HEADS * c, kc), 0) // c
    l_hk = lax.broadcasted_iota(jnp.int32, (GLA_HEADS * c, kc), 1) // GLA_K_DIM
    blk_k = r_hk == l_hk
    r_hv = lax.broadcasted_iota(jnp.int32, (GLA_HEADS * c, GLA_WIDTH), 0) // c
    l_hv = lax.broadcasted_iota(jnp.int32, (GLA_HEADS * c, GLA_WIDTH), 1) // GLA_V_DIM
    blk_v = r_hv == l_hv
    qi = lax.broadcasted_iota(jnp.int32, (c, GLA_HEADS * c), 0)
    kj = lax.broadcasted_iota(jnp.int32, (c, GLA_HEADS * c), 1) % c
    q_scale = GLA_K_DIM ** -0.5

    def direction(backward):
        if backward:
            cum_mat = (col_i >= row_i).astype(BF16)
            causal = kj >= qi
            gate_lo = kc
        else:
            cum_mat = (col_i <= row_i).astype(BF16)
            causal = kj <= qi
            gate_lo = 0
        st_ref[...] = jnp.zeros_like(st_ref)

        def pair_step(t, carry):
            p = (n_pairs - 1 - t) if backward else t
            rows = pl.ds(pl.multiple_of(p * pair, pair), pair)
            q2 = qk_ref[0, rows, 0:kc] * q_scale
            k2 = qk_ref[0, rows, kc:2 * kc]
            g2 = la_ref[rows, gate_lo:gate_lo + kc]
            v2 = v_ref[0, rows, :]
            vt2 = vt_ref[:, rows]
            outs = [None, None]
            for ci in ((1, 0) if backward else (0, 1)):
                sl = slice(ci * c, (ci + 1) * c)
                hi, mid, lo = _split3(g2[sl])
                c3 = _dot(cum_mat, jnp.concatenate([hi, mid, lo], axis=1))
                bcum = c3[:, 0:kc] + c3[:, kc:2 * kc] + c3[:, 2 * kc:3 * kc]
                eb = jnp.exp(bcum)
                qt = (q2[sl] * eb).astype(BF16)
                kt = k2[sl] * jnp.exp(-bcum)
                decay = eb[0:1] if backward else eb[c - 1:c]
                kt_bd = jnp.where(blk_k, jnp.concatenate([kt] * GLA_HEADS, axis=0), 0.0).astype(BF16)
                a = _dot_nt(qt, kt_bd)
                a = jnp.where(causal, a, 0.0).astype(BF16)
                v_bd = jnp.where(blk_v, jnp.concatenate([v2[sl]] * GLA_HEADS, axis=0), 0.0).astype(BF16)
                st = st_ref[...]
                o_c = _dot(a, v_bd) + _dot_nt(qt, st.astype(BF16))
                outs[ci] = o_c
                kt_pair = jnp.concatenate(
                    [kt if ci == 0 else jnp.zeros_like(kt), kt if ci == 1 else jnp.zeros_like(kt)],
                    axis=0).astype(BF16)
                kv = _dot(vt2, kt_pair)
                st_ref[...] = (st + jnp.where(blk_k, kv, 0.0)) * decay
            o2 = jnp.concatenate(outs, axis=0)
            if backward:
                acc_ref[rows, :] += o2
            else:
                acc_ref[rows, :] = o2
            return carry

        lax.fori_loop(0, n_pairs, pair_step, 0)

    direction(False)
    direction(True)

    o = acc_ref[...]
    o2 = o * o
    o2_hi = o2.astype(BF16)
    o2_lo = (o2 - o2_hi.astype(F32)).astype(BF16)
    gi = lax.broadcasted_iota(jnp.int32, (GLA_WIDTH, GLA_WIDTH), 0) // GLA_V_DIM
    gj = lax.broadcasted_iota(jnp.int32, (GLA_WIDTH, GLA_WIDTH), 1) // GLA_V_DIM
    ones_bd = (gi == gj).astype(BF16)
    ms = (_dot(o2_hi, ones_bd) + _dot(o2_lo, ones_bd)) * (1.0 / GLA_V_DIM)
    r = r_ref[0]
    gate = r * (1.0 / (1.0 + jnp.exp(-r)))
    o_ref[0] = (o * lax.rsqrt(ms + LN_EPS) * g_ref[...] * gate).astype(BF16)


def _gla(gqk, gv, gr, gz, w2_bd, b2_cat, g_tiled):
    b, s, _ = gv.shape
    per_b = lambda bi: (bi, 0, 0)
    const = lambda bi: (0, 0)
    return pl.pallas_call(
        _gla_kernel,
        grid=(b,),
        in_specs=[
            pl.BlockSpec((1, s, 2 * GLA_K_COLS), per_b),
            pl.BlockSpec((1, s, GLA_WIDTH), per_b),
            pl.BlockSpec((1, s, GLA_WIDTH), per_b),
            pl.BlockSpec((1, s, 2 * GLA_GATE_RANK), per_b),
            pl.BlockSpec((2 * GLA_GATE_RANK, 2 * GLA_K_COLS), const),
            pl.BlockSpec((1, 2 * GLA_K_COLS), const),
            pl.BlockSpec((1, GLA_WIDTH), const),
        ],
        out_specs=pl.BlockSpec((1, s, GLA_WIDTH), per_b),
        out_shape=jax.ShapeDtypeStruct((b, s, GLA_WIDTH), BF16),
        scratch_shapes=[
            pltpu.VMEM((s, 2 * GLA_K_COLS), F32),
            pltpu.VMEM((GLA_WIDTH, s), BF16),
            pltpu.VMEM((s, GLA_WIDTH), F32),
            pltpu.VMEM((GLA_WIDTH, GLA_K_COLS), F32),
        ],
        compiler_params=_params(1),
        name="gla",
    )(gqk, gv, gr, gz, w2_bd, b2_cat, g_tiled)


def _layer_norm(y, g, b):
    mu = jnp.mean(y, axis=1, keepdims=True)
    d = y - mu
    var = jnp.mean(d * d, axis=1, keepdims=True)
    return d * lax.rsqrt(var + LN_EPS) * g + b


def _outproj_kernel(od_ref, of_ref, og_ref, x_ref, w_ref, g_ref, b_ref, o_ref):
    m = (_dot(of_ref[...], w_ref[DIFF_WIDTH:DIFF_WIDTH + FOURIER_WIDTH, :])
         + _dot(og_ref[...], w_ref[DIFF_WIDTH + FOURIER_WIDTH:, :]))
    for hd in range(DIFF_HEADS):
        m = m + _dot(od_ref[0, hd], w_ref[hd * DIFF_V_DIM:(hd + 1) * DIFF_V_DIM, :])
    o_ref[...] = _layer_norm(DEEPNORM_ALPHA * x_ref[...] + m, g_ref[...], b_ref[...])


def _out_proj(o_diff, o_four, o_gla, x2d, w_bf, g, b, tm=512):
    m = x2d.shape[0]
    seq = o_diff.shape[2]
    n_seq_tiles = seq // tm
    row = lambda i: (i, 0)
    const = lambda i: (0, 0)
    return pl.pallas_call(
        _outproj_kernel,
        grid=(m // tm,),
        in_specs=[
            pl.BlockSpec((1, DIFF_HEADS, tm, LANES), lambda i: (i // n_seq_tiles, 0, i % n_seq_tiles, 0)),
            pl.BlockSpec((tm, FOURIER_WIDTH), row),
            pl.BlockSpec((tm, GLA_WIDTH), row),
            pl.BlockSpec((tm, D_MODEL), row),
            pl.BlockSpec((D_MODEL, D_MODEL), const),
            pl.BlockSpec((1, D_MODEL), const),
            pl.BlockSpec((1, D_MODEL), const),
        ],
        out_specs=pl.BlockSpec((tm, D_MODEL), row),
        out_shape=jax.ShapeDtypeStruct((m, D_MODEL), F32),
        compiler_params=_params(1),
        name="out_proj",
    )(o_diff, o_four, o_gla, x2d, w_bf, g, b)


FFN_CHUNK = 256


def _ffn_kernel(x_ref, wg_ref, wu_ref, wd_ref, g_ref, b_ref, o_ref, acc_ref):
    x = x_ref[...]
    xb = x.astype(BF16)
    for ci in range(FFN_HIDDEN // FFN_CHUNK):
        cols = slice(ci * FFN_CHUNK, (ci + 1) * FFN_CHUNK)
        hg = _dot(xb, wg_ref[:, cols])
        hu = _dot(xb, wu_ref[:, cols])
        act = (hg * (1.0 / (1.0 + jnp.exp(-hg))) * hu).astype(BF16)
        part = _dot(act, wd_ref[cols, :])
        if ci == 0:
            acc_ref[...] = part
        else:
            acc_ref[...] += part
    o_ref[...] = _layer_norm(DEEPNORM_ALPHA * x + acc_ref[...], g_ref[...], b_ref[...])


def _ffn(x2d, wg_bf, wu_bf, wd_bf, g, b, tm=512):
    m = x2d.shape[0]
    row = lambda i: (i, 0)
    const = lambda i: (0, 0)
    return pl.pallas_call(
        _ffn_kernel,
        grid=(m // tm,),
        in_specs=[
            pl.BlockSpec((tm, D_MODEL), row),
            pl.BlockSpec((D_MODEL, FFN_HIDDEN), const),
            pl.BlockSpec((D_MODEL, FFN_HIDDEN), const),
            pl.BlockSpec((FFN_HIDDEN, D_MODEL), const),
            pl.BlockSpec((1, D_MODEL), const),
            pl.BlockSpec((1, D_MODEL), const),
        ],
        out_specs=pl.BlockSpec((tm, D_MODEL), row),
        out_shape=jax.ShapeDtypeStruct((m, D_MODEL), F32),
        scratch_shapes=[pltpu.VMEM((tm, D_MODEL), F32)],
        compiler_params=_params(1),
        name="ffn",
    )(x2d, wg_bf, wu_bf, wd_bf, g, b)


def _rope_tables(seq):
    half = DIFF_QK_DIM // 2
    pos = jnp.arange(seq, dtype=F32)
    inv_freq = ROPE_THETA ** (-jnp.arange(0, DIFF_QK_DIM, 2, dtype=F32) / DIFF_QK_DIM)
    ang = pos[:, None] * inv_freq[None, :]
    cos, sin = jnp.cos(ang), jnp.sin(ang)
    reps = LANES // DIFF_QK_DIM
    cos128 = jnp.tile(jnp.concatenate([cos, cos], axis=1), (1, reps))
    sin128 = jnp.tile(jnp.concatenate([-sin, sin], axis=1), (1, reps))
    assert cos128.shape == (seq, LANES) and half * 2 == DIFF_QK_DIM
    return cos128, sin128


def _dft_tables(seq):
    k = np.arange(seq, dtype=np.int64)
    ang = 2.0 * np.pi * ((k[:, None] * k[None, :]) % seq).astype(np.float64) / seq
    f_mat = np.concatenate([np.cos(ang), -np.sin(ang)], axis=1)
    c = np.arange(FOURIER_GROUP_DIM, dtype=np.int64)
    ang_c = 2.0 * np.pi * ((c[:, None] * c[None, :]) % FOURIER_GROUP_DIM).astype(np.float64) / FOURIER_GROUP_DIM
    eye = np.eye(FOURIER_GROUPS)
    cc_bd = np.kron(eye, np.cos(ang_c))
    sc_bd = np.kron(eye, np.sin(ang_c))
    return tuple(jnp.asarray(t, dtype=F32).astype(BF16) for t in (f_mat, cc_bd, sc_bd))


def _block_diag(blocks):
    n = len(blocks)
    rows = []
    for i, blk in enumerate(blocks):
        rows.append(jnp.concatenate(
            [blk if j == i else jnp.zeros((blk.shape[0], blocks[j].shape[1]), blk.dtype) for j in range(n)],
            axis=1))
    return jnp.concatenate(rows, axis=0)


def kernel(x, w_in, diff_lambda, diff_norm_g, fourier_w, gla_gate_w2, gla_gate_b2, gla_norm_g, w_out,
           ln1_g, ln1_b, ffn_w_gate, ffn_w_up, ffn_w_down, ln2_g, ln2_b):
    b, s, d = x.shape
    m = b * s
    cos128, sin128 = _rope_tables(s)
    f_mat, cc_bd, sc_bd = _dft_tables(s)
    x2d = x.reshape(m, d)
    for l in range(DEPTH):
        lam_init = 0.8 - 0.6 * math.exp(-0.3 * l)
        q, k, v, fu, gqk, gv, gr, gz = _in_proj(x2d, w_in[l].astype(BF16), cos128, sin128, s)
        o_diff = _diff_attention(q, k, v, diff_lambda[l], diff_norm_g[l].reshape(1, -1), lam_init)
        w_four = _block_diag([fourier_w[l, g] for g in range(FOURIER_GROUPS)]).astype(BF16)
        o_four = _fourier(fu.reshape(b, s, -1), cc_bd, sc_bd, f_mat, w_four)
        w2_bd = _block_diag([gla_gate_w2[l, 0], gla_gate_w2[l, 1]]).astype(BF16)
        b2_cat = gla_gate_b2[l].reshape(1, -1)
        g_gla = jnp.tile(gla_norm_g[l], GLA_HEADS).reshape(1, -1)
        o_gla = _gla(gqk.reshape(b, s, -1), gv.reshape(b, s, -1), gr.reshape(b, s, -1),
                     gz.reshape(b, s, -1), w2_bd, b2_cat, g_gla)
        x2d = _out_proj(o_diff, o_four.reshape(m, -1), o_gla.reshape(m, -1), x2d,
                        w_out[l].astype(BF16), ln1_g[l].reshape(1, -1), ln1_b[l].reshape(1, -1))
        x2d = _ffn(x2d, ffn_w_gate[l].astype(BF16), ffn_w_up[l].astype(BF16), ffn_w_down[l].astype(BF16),
                   ln2_g[l].reshape(1, -1), ln2_b[l].reshape(1, -1))
    return x2d.reshape(b, s, d)
```

```python
import functools
import math

import jax
import jax.numpy as jnp
import numpy as np
from jax import lax
from jax.experimental import pallas as pl
from jax.experimental.pallas import tpu as pltpu

D_MODEL = 1024
DEPTH = 2
DIFF_HEADS = 4
DIFF_QK_DIM = 64
DIFF_V_DIM = 128
DIFF_WIDTH = 512
DIFF_QK_COLS = 512
ROPE_THETA = 10000.0
FOURIER_GROUPS = 4
FOURIER_GROUP_DIM = 64
FOURIER_WIDTH = 256
GLA_HEADS = 4
GLA_V_DIM = 64
GLA_K_DIM = 32
GLA_WIDTH = 256
GLA_K_COLS = 128
GLA_GATE_RANK = 16
GLA_GATE_TAU = 16.0
GLA_CHUNK = 64
IN_WIDTH = 2592
FFN_HIDDEN = 2816
DEEPNORM_ALPHA = (2 * DEPTH) ** 0.25
LN_EPS = 1e-5

OFF_DQ, OFF_DK, OFF_DV, OFF_FU = 0, 512, 1024, 1536
OFF_GQ, OFF_GV, OFF_GR, OFF_GZ = 1792, 2048, 2304, 2560

LANES = 128
VMEM_LIMIT = 56 * 1024 * 1024

BF16 = jnp.bfloat16
F32 = jnp.float32


def _dot(a, b):
    return jnp.dot(a, b, preferred_element_type=F32)


def _dot_nt(a, b):
    return lax.dot_general(a, b, (((1,), (1,)), ((), ())), preferred_element_type=F32)


def _params(n_grid_dims):
    return pltpu.CompilerParams(
        dimension_semantics=("arbitrary",) * n_grid_dims,
        vmem_limit_bytes=VMEM_LIMIT)


def _rope_slab(t, cos, sin_signed, first_half):
    swapped = jnp.where(first_half, pltpu.roll(t, 96, 1), pltpu.roll(t, 32, 1))
    return t * cos + swapped * sin_signed


def _inproj_kernel(x_ref, w_ref, cos_ref, sin_ref,
                   q_ref, k_ref, v_ref, fu_ref, gqk_ref, gv_ref, gr_ref, gz_ref):
    xb = x_ref[...].astype(BF16)
    cos = cos_ref[...]
    sin_signed = sin_ref[...]
    lane = lax.broadcasted_iota(jnp.int32, cos.shape, 1)
    first_half = (lane % DIFF_QK_DIM) < (DIFF_QK_DIM // 2)
    qk_scale = DIFF_QK_DIM ** -0.5 * math.log2(math.e)
    hq = _dot(xb, w_ref[:, OFF_DQ:OFF_DK])
    hk = _dot(xb, w_ref[:, OFF_DK:OFF_DV])
    hv = _dot(xb, w_ref[:, OFF_DV:OFF_FU])
    for hd in range(DIFF_HEADS):
        slab = slice(hd * LANES, (hd + 1) * LANES)
        q_ref[0, hd] = (_rope_slab(hq[:, slab], cos, sin_signed, first_half) * qk_scale).astype(BF16)
        k_ref[0, hd] = _rope_slab(hk[:, slab], cos, sin_signed, first_half).astype(BF16)
        v_ref[0, hd] = hv[:, slab].astype(BF16)
    fu_ref[...] = _dot(xb, w_ref[:, OFF_FU:OFF_GQ]).astype(BF16)
    gqk_ref[...] = _dot(xb, w_ref[:, OFF_GQ:OFF_GV])
    gv_ref[...] = _dot(xb, w_ref[:, OFF_GV:OFF_GR])
    gr_ref[...] = _dot(xb, w_ref[:, OFF_GR:OFF_GZ])
    gz_ref[...] = _dot(xb, w_ref[:, OFF_GZ:IN_WIDTH])


def _in_proj(x2d, w_bf, cos128, sin128, seq, tm=512):
    m = x2d.shape[0]
    n_seq_tiles = seq // tm
    row = lambda i: (i, 0)
    const = lambda i: (0, 0)
    tab = lambda i: (i % n_seq_tiles, 0)
    heads_shape = jax.ShapeDtypeStruct((m // seq, DIFF_HEADS, seq, LANES), BF16)
    heads_spec = pl.BlockSpec((1, DIFF_HEADS, tm, LANES), lambda i: (i // n_seq_tiles, 0, i % n_seq_tiles, 0))
    out_shapes = (
        heads_shape, heads_shape, heads_shape,
        jax.ShapeDtypeStruct((m, FOURIER_WIDTH), BF16),
        jax.ShapeDtypeStruct((m, 2 * GLA_K_COLS), F32),
        jax.ShapeDtypeStruct((m, GLA_WIDTH), F32),
        jax.ShapeDtypeStruct((m, GLA_WIDTH), F32),
        jax.ShapeDtypeStruct((m, 2 * GLA_GATE_RANK), F32),
    )
    return pl.pallas_call(
        _inproj_kernel,
        grid=(m // tm,),
        in_specs=[
            pl.BlockSpec((tm, D_MODEL), row),
            pl.BlockSpec((D_MODEL, IN_WIDTH), const),
            pl.BlockSpec((tm, LANES), tab),
            pl.BlockSpec((tm, LANES), tab),
        ],
        out_specs=(heads_spec,) * 3 + tuple(pl.BlockSpec((tm, s.shape[1]), row) for s in out_shapes[3:]),
        out_shape=out_shapes,
        compiler_params=_params(1),
        name="in_proj",
    )(x2d, w_bf, cos128, sin128)


ATTN_TQ = 256


def _attn_kernel(lam_ref, g_ref, q_ref, k_ref, v_ref, o_ref, s0_ref, s1_ref, a0_ref, a1_ref, *, lam_init):
    tq = ATTN_TQ
    seq = k_ref.shape[2]
    tiles = seq // tq
    n_units = DIFF_HEADS * tiles
    lp = lam_ref[...]
    lam = (jnp.exp(jnp.sum(lp[0:1] * lp[1:2], axis=1, keepdims=True))
           - jnp.exp(jnp.sum(lp[2:3] * lp[3:4], axis=1, keepdims=True)) + lam_init)
    gain = g_ref[...] * (1.0 - lam_init)
    lane = lax.broadcasted_iota(jnp.int32, (tq, LANES), 1)
    first_map = lane < DIFF_QK_DIM

    def head_rows(u):
        h = u // tiles
        return h, pl.ds(pl.multiple_of((u % tiles) * tq, tq), tq)

    def scores(u, s_ref):
        h, rows = head_rows(u)
        q = q_ref[0, h, rows, :]
        k = k_ref[0, h]
        zero = jnp.zeros_like(q)
        s_ref[0] = _dot_nt(jnp.where(first_map, q, zero), k)
        s_ref[1] = _dot_nt(jnp.where(first_map, zero, q), k)

    def softmaxes(s_ref, a_ref):
        def one(m):
            s = s_ref[m]
            p = jnp.exp2(s - jnp.max(s, axis=1, keepdims=True))
            return p.astype(BF16), jnp.sum(p, axis=1, keepdims=True)
        p0, l0 = one(0)
        p1, l1 = one(1)
        a_ref[...] = p0 * (1.0 / l0).astype(BF16) - p1 * (lam / l1).astype(BF16)

    def weighted_values(u, a_ref):
        h, rows = head_rows(u)
        o = _dot(a_ref[...], v_ref[0, h])
        ms = jnp.mean(o * o, axis=1, keepdims=True)
        o_ref[0, h, rows, :] = (o * lax.rsqrt(ms + LN_EPS) * gain).astype(BF16)

    def pair(j, first=False, last=False):
        u0 = 2 * j
        scores(u0 + 1, s1_ref)
        softmaxes(s0_ref, a0_ref)
        if not first:
            weighted_values(u0 - 1, a1_ref)
        if not last:
            scores(u0 + 2, s0_ref)
        softmaxes(s1_ref, a1_ref)
        weighted_values(u0, a0_ref)

    n_pairs = n_units // 2
    scores(0, s0_ref)
    pair(0, first=True)
    lax.fori_loop(1, n_pairs - 1, lambda j, c: (pair(j), c)[1], 0)
    pair(n_pairs - 1, last=True)
    weighted_values(n_units - 1, a1_ref)


def _diff_attention(q, k, v, lam_params, g, lam_init):
    b, _, s, _ = q.shape
    per_b = pl.BlockSpec((1, DIFF_HEADS, s, LANES), lambda bi: (bi, 0, 0, 0))
    return pl.pallas_call(
        functools.partial(_attn_kernel, lam_init=lam_init),
        grid=(b,),
        in_specs=[
            pl.BlockSpec((4, DIFF_QK_DIM), lambda bi: (0, 0)),
            pl.BlockSpec((1, DIFF_V_DIM), lambda bi: (0, 0)),
            per_b, per_b, per_b,
        ],
        out_specs=per_b,
        out_shape=jax.ShapeDtypeStruct((b, DIFF_HEADS, s, LANES), BF16),
        scratch_shapes=[
            pltpu.VMEM((2, ATTN_TQ, s), F32), pltpu.VMEM((2, ATTN_TQ, s), F32),
            pltpu.VMEM((ATTN_TQ, s), BF16), pltpu.VMEM((ATTN_TQ, s), BF16),
        ],
        compiler_params=_params(1),
        name="diff_attn",
    )(lam_params, g, q, k, v)


def _fourier_kernel(u_ref, cc_ref, sc_ref, f_ref, w_ref, o_ref, ab_ref, *, scale):
    s = u_ref.shape[1]

    @pl.when(pl.program_id(1) == 0)
    def _():
        u = u_ref[0]
        ab_ref[0:s, :] = _dot(u, cc_ref[...]).astype(BF16)
        ab_ref[s:2 * s, :] = _dot(u, sc_ref[...]).astype(BF16)

    z = _dot(f_ref[...], ab_ref[...]) * scale
    o_ref[0] = _dot(z.astype(BF16), w_ref[...]).astype(BF16)


def _fourier(u, cc_bd, sc_bd, f_mat, w_bd, tk=512):
    b, s, _ = u.shape
    scale = 1.0 / math.sqrt(s * FOURIER_GROUP_DIM)
    const2 = lambda bi, i: (0, 0)
    return pl.pallas_call(
        functools.partial(_fourier_kernel, scale=scale),
        grid=(b, s // tk),
        in_specs=[
            pl.BlockSpec((1, s, FOURIER_WIDTH), lambda bi, i: (bi, 0, 0)),
            pl.BlockSpec((FOURIER_WIDTH, FOURIER_WIDTH), const2),
            pl.BlockSpec((FOURIER_WIDTH, FOURIER_WIDTH), const2),
            pl.BlockSpec((tk, 2 * s), lambda bi, i: (i, 0)),
            pl.BlockSpec((FOURIER_WIDTH, FOURIER_WIDTH), const2),
        ],
        out_specs=pl.BlockSpec((1, tk, FOURIER_WIDTH), lambda bi, i: (bi, i, 0)),
        out_shape=jax.ShapeDtypeStruct((b, s, FOURIER_WIDTH), BF16),
        scratch_shapes=[pltpu.VMEM((2 * s, FOURIER_WIDTH), BF16)],
        compiler_params=_params(2),
        name="fourier",
    )(u, cc_bd, sc_bd, f_mat, w_bd)


def _split3(g):
    hi = g.astype(BF16)
    r1 = g - hi.astype(F32)
    mid = r1.astype(BF16)
    lo = (r1 - mid.astype(F32)).astype(BF16)
    return hi, mid, lo


def _gla_kernel(qk_ref, v_ref, r_ref, z_ref, w2_ref, b2_ref, g_ref, o_ref,
                la_ref, vt_ref, acc_ref, st_ref):
    s = v_ref.shape[1]
    c = GLA_CHUNK
    pair = 2 * c
    n_pairs = s // pair
    kc = GLA_K_COLS

    logit = _dot(z_ref[0].astype(BF16), w2_ref[...]) + b2_ref[...]
    la_ref[...] = (jnp.minimum(logit, 0.0) - jnp.log1p(jnp.exp(-jnp.abs(logit)))) * (1.0 / GLA_GATE_TAU)
    vt_ref[...] = v_ref[0].T.astype(BF16)

    row_i = lax.broadcasted_iota(jnp.int32, (c, c), 0)
    col_i = lax.broadcasted_iota(jnp.int32, (c, c), 1)
    r_hk = lax.broadcasted_iota(jnp.int32, (GLA_HEADS * c, kc), 0) // c
    l_hk = lax.broadcasted_iota(jnp.int32, (GLA_HEADS * c, kc), 1) // GLA_K_DIM
    blk_k = r_hk == l_hk
    r_hv = lax.broadcasted_iota(jnp.int32, (GLA_HEADS * c, GLA_WIDTH), 0) // c
    l_hv = lax.broadcasted_iota(jnp.int32, (GLA_HEADS * c, GLA_WIDTH), 1) // GLA_V_DIM
    blk_v = r_hv == l_hv
    qi = lax.broadcasted_iota(jnp.int32, (c, GLA_HEADS * c), 0)
    kj = lax.broadcasted_iota(jnp.int32, (c, GLA_HEADS * c), 1) % c
    q_scale = GLA_K_DIM ** -0.5

    def direction(backward):
        if backward:
            cum_mat = (col_i >= row_i).astype(BF16)
            causal = kj >= qi
            gate_lo = kc
        else:
            cum_mat = (col_i <= row_i).astype(BF16)
            causal = kj <= qi
            gate_lo = 0
        st_ref[...] = jnp.zeros_like(st_ref)

        def pair_step(t, carry):
            p = (n_pairs - 1 - t) if backward else t
            rows = pl.ds(pl.multiple_of(p * pair, pair), pair)
            q2 = qk_ref[0, rows, 0:kc] * q_scale
            k2 = qk_ref[0, rows, kc:2 * kc]
            g2 = la_ref[rows, gate_lo:gate_lo + kc]
            v2 = v_ref[0, rows, :]
            vt2 = vt_ref[:, rows]
            outs = [None, None]
            for ci in ((1, 0) if backward else (0, 1)):
                sl = slice(ci * c, (ci + 1) * c)
                hi, mid, lo = _split3(g2[sl])
                c3 = _dot(cum_mat, jnp.concatenate([hi, mid, lo], axis=1))
                bcum = c3[:, 0:kc] + c3[:, kc:2 * kc] + c3[:, 2 * kc:3 * kc]
                eb = jnp.exp(bcum)
                qt = (q2[sl] * eb).astype(BF16)
                kt = k2[sl] * jnp.exp(-bcum)
                decay = eb[0:1] if backward else eb[c - 1:c]
                kt_bd = jnp.where(blk_k, jnp.concatenate([kt] * GLA_HEADS, axis=0), 0.0).astype(BF16)
                a = _dot_nt(qt, kt_bd)
                a = jnp.where(causal, a, 0.0).astype(BF16)
                v_bd = jnp.where(blk_v, jnp.concatenate([v2[sl]] * GLA_HEADS, axis=0), 0.0).astype(BF16)
                st = st_ref[...]
                o_c = _dot(a, v_bd) + _dot_nt(qt, st.astype(BF16))
                outs[ci] = o_c
                kt_pair = jnp.concatenate(
                    [kt if ci == 0 else jnp.zeros_like(kt), kt if ci == 1 else jnp.zeros_like(kt)],
                    axis=0).astype(BF16)
                kv = _dot(vt2, kt_pair)
                st_ref[...] = (st + jnp.where(blk_k, kv, 0.0)) * decay
            o2 = jnp.concatenate(outs, axis=0)
            if backward:
                acc_ref[rows, :] += o2
            else:
                acc_ref[rows, :] = o2
            return carry

        lax.fori_loop(0, n_pairs, pair_step, 0)

    direction(False)
    direction(True)

    o = acc_ref[...]
    o2 = o * o
    o2_hi = o2.astype(BF16)
    o2_lo = (o2 - o2_hi.astype(F32)).astype(BF16)
    gi = lax.broadcasted_iota(jnp.int32, (GLA_WIDTH, GLA_WIDTH), 0) // GLA_V_DIM
    gj = lax.broadcasted_iota(jnp.int32, (GLA_WIDTH, GLA_WIDTH), 1) // GLA_V_DIM
    ones_bd = (gi == gj).astype(BF16)
    ms = (_dot(o2_hi, ones_bd) + _dot(o2_lo, ones_bd)) * (1.0 / GLA_V_DIM)
    r = r_ref[0]
    gate = r * (1.0 / (1.0 + jnp.exp(-r)))
    o_ref[0] = (o * lax.rsqrt(ms + LN_EPS) * g_ref[...] * gate).astype(BF16)


def _gla(gqk, gv, gr, gz, w2_bd, b2_cat, g_tiled):
    b, s, _ = gv.shape
    per_b = lambda bi: (bi, 0, 0)
    const = lambda bi: (0, 0)
    return pl.pallas_call(
        _gla_kernel,
        grid=(b,),
        in_specs=[
            pl.BlockSpec((1, s, 2 * GLA_K_COLS), per_b),
            pl.BlockSpec((1, s, GLA_WIDTH), per_b),
            pl.BlockSpec((1, s, GLA_WIDTH), per_b),
            pl.BlockSpec((1, s, 2 * GLA_GATE_RANK), per_b),
            pl.BlockSpec((2 * GLA_GATE_RANK, 2 * GLA_K_COLS), const),
            pl.BlockSpec((1, 2 * GLA_K_COLS), const),
            pl.BlockSpec((1, GLA_WIDTH), const),
        ],
        out_specs=pl.BlockSpec((1, s, GLA_WIDTH), per_b),
        out_shape=jax.ShapeDtypeStruct((b, s, GLA_WIDTH), BF16),
        scratch_shapes=[
            pltpu.VMEM((s, 2 * GLA_K_COLS), F32),
            pltpu.VMEM((GLA_WIDTH, s), BF16),
            pltpu.VMEM((s, GLA_WIDTH), F32),
            pltpu.VMEM((GLA_WIDTH, GLA_K_COLS), F32),
        ],
        compiler_params=_params(1),
        name="gla",
    )(gqk, gv, gr, gz, w2_bd, b2_cat, g_tiled)


def _layer_norm(y, g, b):
    mu = jnp.mean(y, axis=1, keepdims=True)
    d = y - mu
    var = jnp.mean(d * d, axis=1, keepdims=True)
    return d * lax.rsqrt(var + LN_EPS) * g + b


def _outproj_kernel(od_ref, of_ref, og_ref, x_ref, w_ref, g_ref, b_ref, o_ref):
    mixed = jnp.concatenate([od_ref[0, hd] for hd in range(DIFF_HEADS)] + [of_ref[...], og_ref[...]], axis=1)
    m = _dot(mixed, w_ref[...])
    o_ref[...] = _layer_norm(DEEPNORM_ALPHA * x_ref[...] + m, g_ref[...], b_ref[...])


def _out_proj(o_diff, o_four, o_gla, x2d, w_bf, g, b, tm=512):
    m = x2d.shape[0]
    seq = o_diff.shape[2]
    n_seq_tiles = seq // tm
    row = lambda i: (i, 0)
    const = lambda i: (0, 0)
    return pl.pallas_call(
        _outproj_kernel,
        grid=(m // tm,),
        in_specs=[
            pl.BlockSpec((1, DIFF_HEADS, tm, LANES), lambda i: (i // n_seq_tiles, 0, i % n_seq_tiles, 0)),
            pl.BlockSpec((tm, FOURIER_WIDTH), row),
            pl.BlockSpec((tm, GLA_WIDTH), row),
            pl.BlockSpec((tm, D_MODEL), row),
            pl.BlockSpec((D_MODEL, D_MODEL), const),
            pl.BlockSpec((1, D_MODEL), const),
            pl.BlockSpec((1, D_MODEL), const),
        ],
        out_specs=pl.BlockSpec((tm, D_MODEL), row),
        out_shape=jax.ShapeDtypeStruct((m, D_MODEL), F32),
        compiler_params=_params(1),
        name="out_proj",
    )(o_diff, o_four, o_gla, x2d, w_bf, g, b)


FFN_CHUNK = 256


def _ffn_kernel(x_ref, wg_ref, wu_ref, wd_ref, g_ref, b_ref, o_ref, acc_ref):
    x = x_ref[...]
    xb = x.astype(BF16)
    for ci in range(FFN_HIDDEN // FFN_CHUNK):
        cols = slice(ci * FFN_CHUNK, (ci + 1) * FFN_CHUNK)
        hg = _dot(xb, wg_ref[:, cols])
        hu = _dot(xb, wu_ref[:, cols])
        act = (hg * (1.0 / (1.0 + jnp.exp(-hg))) * hu).astype(BF16)
        part = _dot(act, wd_ref[cols, :])
        if ci == 0:
            acc_ref[...] = part
        else:
            acc_ref[...] += part
    o_ref[...] = _layer_norm(DEEPNORM_ALPHA * x + acc_ref[...], g_ref[...], b_ref[...])


def _ffn(x2d, wg_bf, wu_bf, wd_bf, g, b, tm=512):
    m = x2d.shape[0]
    row = lambda i: (i, 0)
    const = lambda i: (0, 0)
    return pl.pallas_call(
        _ffn_kernel,
        grid=(m // tm,),
        in_specs=[
            pl.BlockSpec((tm, D_MODEL), row),
            pl.BlockSpec((D_MODEL, FFN_HIDDEN), const),
            pl.BlockSpec((D_MODEL, FFN_HIDDEN), const),
            pl.BlockSpec((FFN_HIDDEN, D_MODEL), const),
            pl.BlockSpec((1, D_MODEL), const),
            pl.BlockSpec((1, D_MODEL), const),
        ],
        out_specs=pl.BlockSpec((tm, D_MODEL), row),
        out_shape=jax.ShapeDtypeStruct((m, D_MODEL), F32),
        scratch_shapes=[pltpu.VMEM((tm, D_MODEL), F32)],
        compiler_params=_params(1),
        name="ffn",
    )(x2d, wg_bf, wu_bf, wd_bf, g, b)


def _rope_tables(seq):
    half = DIFF_QK_DIM // 2
    pos = jnp.arange(seq, dtype=F32)
    inv_freq = ROPE_THETA ** (-jnp.arange(0, DIFF_QK_DIM, 2, dtype=F32) / DIFF_QK_DIM)
    ang = pos[:, None] * inv_freq[None, :]
    cos, sin = jnp.cos(ang), jnp.sin(ang)
    reps = LANES // DIFF_QK_DIM
    cos128 = jnp.tile(jnp.concatenate([cos, cos], axis=1), (1, reps))
    sin128 = jnp.tile(jnp.concatenate([-sin, sin], axis=1), (1, reps))
    assert cos128.shape == (seq, LANES) and half * 2 == DIFF_QK_DIM
    return cos128, sin128


def _dft_tables(seq):
    k = np.arange(seq, dtype=np.int64)
    ang = 2.0 * np.pi * ((k[:, None] * k[None, :]) % seq).astype(np.float64) / seq
    f_mat = np.concatenate([np.cos(ang), -np.sin(ang)], axis=1)
    c = np.arange(FOURIER_GROUP_DIM, dtype=np.int64)
    ang_c = 2.0 * np.pi * ((c[:, None] * c[None, :]) % FOURIER_GROUP_DIM).astype(np.float64) / FOURIER_GROUP_DIM
    eye = np.eye(FOURIER_GROUPS)
    cc_bd = np.kron(eye, np.cos(ang_c))
    sc_bd = np.kron(eye, np.sin(ang_c))
    return tuple(jnp.asarray(t, dtype=F32).astype(BF16) for t in (f_mat, cc_bd, sc_bd))


def _block_diag(blocks):
    n = len(blocks)
    rows = []
    for i, blk in enumerate(blocks):
        rows.append(jnp.concatenate(
            [blk if j == i else jnp.zeros((blk.shape[0], blocks[j].shape[1]), blk.dtype) for j in range(n)],
            axis=1))
    return jnp.concatenate(rows, axis=0)


def kernel(x, w_in, diff_lambda, diff_norm_g, fourier_w, gla_gate_w2, gla_gate_b2, gla_norm_g, w_out,
           ln1_g, ln1_b, ffn_w_gate, ffn_w_up, ffn_w_down, ln2_g, ln2_b):
    b, s, d = x.shape
    m = b * s
    cos128, sin128 = _rope_tables(s)
    f_mat, cc_bd, sc_bd = _dft_tables(s)
    x2d = x.reshape(m, d)
    for l in range(DEPTH):
        lam_init = 0.8 - 0.6 * math.exp(-0.3 * l)
        q, k, v, fu, gqk, gv, gr, gz = _in_proj(x2d, w_in[l].astype(BF16), cos128, sin128, s)
        o_diff = _diff_attention(q, k, v, diff_lambda[l], diff_norm_g[l].reshape(1, -1), lam_init)
        w_four = _block_diag([fourier_w[l, g] for g in range(FOURIER_GROUPS)]).astype(BF16)
        o_four = _fourier(fu.reshape(b, s, -1), cc_bd, sc_bd, f_mat, w_four)
        w2_bd = _block_diag([gla_gate_w2[l, 0], gla_gate_w2[l, 1]]).astype(BF16)
        b2_cat = gla_gate_b2[l].reshape(1, -1)
        g_gla = jnp.tile(gla_norm_g[l], GLA_HEADS).reshape(1, -1)
        o_gla = _gla(gqk.reshape(b, s, -1), gv.reshape(b, s, -1), gr.reshape(b, s, -1),
                     gz.reshape(b, s, -1), w2_bd, b2_cat, g_gla)
        x2d = _out_proj(o_diff, o_four.reshape(m, -1), o_gla.reshape(m, -1), x2d,
                        w_out[l].astype(BF16), ln1_g[l].reshape(1, -1), ln1_b[l].reshape(1, -1))
        x2d = _ffn(x2d, ffn_w_gate[l].astype(BF16), ffn_w_up[l].astype(BF16), ffn_w_down[l].astype(BF16),
                   ln2_g[l].reshape(1, -1), ln2_b[l].reshape(1, -1))
    return x2d.reshape(b, s, d)
```

```python
import functools
import math

import jax
import jax.numpy as jnp
import numpy as np
from jax import lax
from jax.experimental import pallas as pl
from jax.experimental.pallas import tpu as pltpu

D_MODEL = 1024
DEPTH = 2
DIFF_HEADS = 4
DIFF_QK_DIM = 64
DIFF_V_DIM = 128
DIFF_WIDTH = 512
DIFF_QK_COLS = 512
ROPE_THETA = 10000.0
FOURIER_GROUPS = 4
FOURIER_GROUP_DIM = 64
FOURIER_WIDTH = 256
GLA_HEADS = 4
GLA_V_DIM = 64
GLA_K_DIM = 32
GLA_WIDTH = 256
GLA_K_COLS = 128
GLA_GATE_RANK = 16
GLA_GATE_TAU = 16.0
GLA_CHUNK = 64
IN_WIDTH = 2592
FFN_HIDDEN = 2816
DEEPNORM_ALPHA = (2 * DEPTH) ** 0.25
LN_EPS = 1e-5

OFF_DQ, OFF_DK, OFF_DV, OFF_FU = 0, 512, 1024, 1536
OFF_GQ, OFF_GV, OFF_GR, OFF_GZ = 1792, 2048, 2304, 2560

LANES = 128
VMEM_LIMIT = 56 * 1024 * 1024

BF16 = jnp.bfloat16
F32 = jnp.float32


def _dot(a, b):
    return jnp.dot(a, b, preferred_element_type=F32)


def _dot_nt(a, b):
    return lax.dot_general(a, b, (((1,), (1,)), ((), ())), preferred_element_type=F32)


def _params(n_grid_dims):
    return pltpu.CompilerParams(
        dimension_semantics=("arbitrary",) * n_grid_dims,
        vmem_limit_bytes=VMEM_LIMIT)


def _rope_slab(t, cos, sin_signed, first_half):
    swapped = jnp.where(first_half, pltpu.roll(t, 96, 1), pltpu.roll(t, 32, 1))
    return t * cos + swapped * sin_signed


ATTN_TQ = 256
ATTN_KC = 256


def _inproj_kernel(x_ref, wqv_ref, w_ref, cos_ref, sin_ref, cost_ref, sint_ref,
                   qt_ref, k_ref, vt_ref, fu_ref, gqk_ref, gv_ref, gr_ref, gz_ref):
    xb = x_ref[...].astype(BF16)
    tm = xb.shape[0]
    cos = cos_ref[...]
    sin_signed = sin_ref[...]
    lane = lax.broadcasted_iota(jnp.int32, cos.shape, 1)
    first_half = (lane % DIFF_QK_DIM) < (DIFF_QK_DIM // 2)
    qk_scale = DIFF_QK_DIM ** -0.5 * math.log2(math.e)
    hqv_t = _dot_nt(wqv_ref[...], xb)
    cos_t = cost_ref[...]
    sin_t = sint_ref[...]
    half = DIFF_QK_DIM // 2
    hk = _dot(xb, w_ref[:, OFF_DK:OFF_DV])
    for hd in range(DIFF_HEADS):
        qt = hqv_t[hd * LANES:(hd + 1) * LANES]
        swapped = jnp.concatenate(
            [qt[half:2 * half], qt[0:half], qt[3 * half:4 * half], qt[2 * half:3 * half]], axis=0)
        qt = ((qt * cos_t + swapped * sin_t) * qk_scale).astype(BF16)
        vt = hqv_t[DIFF_QK_COLS + hd * LANES:DIFF_QK_COLS + (hd + 1) * LANES].astype(BF16)
        for j in range(tm // ATTN_TQ):
            qt_ref[0, hd, j] = qt[:, j * ATTN_TQ:(j + 1) * ATTN_TQ]
        for j in range(tm // ATTN_KC):
            vt_ref[0, hd, j] = vt[:, j * ATTN_KC:(j + 1) * ATTN_KC]
        slab = slice(hd * LANES, (hd + 1) * LANES)
        k_ref[0, hd] = _rope_slab(hk[:, slab], cos, sin_signed, first_half).astype(BF16)
    fu_ref[...] = _dot(xb, w_ref[:, OFF_FU:OFF_GQ]).astype(BF16)
    gqk_ref[...] = _dot(xb, w_ref[:, OFF_GQ:OFF_GV])
    gv_ref[...] = _dot(xb, w_ref[:, OFF_GV:OFF_GR])
    gr_ref[...] = _dot(xb, w_ref[:, OFF_GR:OFF_GZ])
    gz_ref[...] = _dot(xb, w_ref[:, OFF_GZ:IN_WIDTH])


def _in_proj(x2d, wqv_t, w_bf, rope, seq, tm=512):
    m = x2d.shape[0]
    batch = m // seq
    n_seq_tiles = seq // tm
    cos128, sin128, cos_t, sin_t = rope
    row = lambda i: (i, 0)
    const = lambda i: (0, 0)
    tab = lambda i: (i % n_seq_tiles, 0)
    tab_t = lambda i: (0, i % n_seq_tiles)
    tiled = lambda t: (jax.ShapeDtypeStruct((batch, DIFF_HEADS, seq // t, LANES, t), BF16),
                       pl.BlockSpec((1, DIFF_HEADS, tm // t, LANES, t),
                                    lambda i: (i // n_seq_tiles, 0, i % n_seq_tiles, 0, 0)))
    qt_shape, qt_spec = tiled(ATTN_TQ)
    vt_shape, vt_spec = tiled(ATTN_KC)
    k_shape = jax.ShapeDtypeStruct((batch, DIFF_HEADS, seq, LANES), BF16)
    k_spec = pl.BlockSpec((1, DIFF_HEADS, tm, LANES), lambda i: (i // n_seq_tiles, 0, i % n_seq_tiles, 0))
    flat_shapes = (
        jax.ShapeDtypeStruct((m, FOURIER_WIDTH), BF16),
        jax.ShapeDtypeStruct((m, 2 * GLA_K_COLS), F32),
        jax.ShapeDtypeStruct((m, GLA_WIDTH), F32),
        jax.ShapeDtypeStruct((m, GLA_WIDTH), F32),
        jax.ShapeDtypeStruct((m, 2 * GLA_GATE_RANK), F32),
    )
    return pl.pallas_call(
        _inproj_kernel,
        grid=(m // tm,),
        in_specs=[
            pl.BlockSpec((tm, D_MODEL), row),
            pl.BlockSpec((DIFF_QK_COLS + DIFF_WIDTH, D_MODEL), const),
            pl.BlockSpec((D_MODEL, IN_WIDTH), const),
            pl.BlockSpec((tm, LANES), tab),
            pl.BlockSpec((tm, LANES), tab),
            pl.BlockSpec((LANES, tm), tab_t),
            pl.BlockSpec((LANES, tm), tab_t),
        ],
        out_specs=(qt_spec, k_spec, vt_spec) + tuple(pl.BlockSpec((tm, s.shape[1]), row) for s in flat_shapes),
        out_shape=(qt_shape, k_shape, vt_shape) + flat_shapes,
        compiler_params=_params(1),
        name="in_proj",
    )(x2d, wqv_t, w_bf, cos128, sin128, cos_t, sin_t)


ATTN_SUB = 128
ATTN_GROUP = 4


def _attn_kernel(lam_ref, g_ref, qt_ref, k_ref, vt_ref, o_ref,
                 sa_ref, sb_ref, pa_ref, pb_ref, mx_ref, coef_ref, acc_ref, *, lam_init):
    tq, kc, sub = ATTN_TQ, ATTN_KC, ATTN_SUB
    seq = k_ref.shape[2]
    tiles = seq // tq
    n_units = DIFF_HEADS * tiles
    n_chunks = seq // kc
    lp = lam_ref[...]
    lam = (jnp.exp(jnp.sum(lp[0:1] * lp[1:2], axis=1, keepdims=True))
           - jnp.exp(jnp.sum(lp[2:3] * lp[3:4], axis=1, keepdims=True)) + lam_init)
    gain = g_ref[...] * (1.0 - lam_init)
    feature = lax.broadcasted_iota(jnp.int32, (LANES, tq), 0)
    first_map = feature < DIFF_QK_DIM
    s_bufs = (sa_ref, sb_ref)
    p_bufs = (pa_ref, pb_ref)

    def step(k, par, do_a=True, do_b=True, do_c=True):
        s_w, s_r = s_bufs[par], s_bufs[1 - par]
        p_w, p_r = p_bufs[1 - par], p_bufs[par]
        if do_a:
            h_a = k // tiles
            qt = qt_ref[0, h_a, k % tiles]
            zero = jnp.zeros_like(qt)
            qw = (jnp.where(first_map, qt, zero), jnp.where(first_map, zero, qt))
        if do_b:
            m8 = (mx_ref[1 - par, 0], mx_ref[1 - par, 1])
        if do_c:
            h_c = (k - 2) // tiles
            c16 = (coef_ref[par, 0], coef_ref[par, 1])

        def chunk(c, carry):
            mx = [carry[0], carry[1]]
            ls = [carry[2], carry[3]]
            acc = carry[4]
            for part in range(kc // sub):
                rows = pl.ds(pl.multiple_of(c * kc + part * sub, sub), sub)
                if do_a:
                    keys = k_ref[0, h_a, rows, :]
                    for m in range(2):
                        st = _dot(keys, qw[m])
                        s_w[m, rows, :] = st
                        mx[m] = jnp.maximum(mx[m], jnp.max(st.reshape(sub // 8, 8, tq), axis=0))
                if do_b:
                    for m in range(2):
                        p = jnp.exp2(s_r[m, rows, :].reshape(sub // 8, 8, tq) - m8[m][None])
                        ls[m] = ls[m] + jnp.sum(p, axis=0)
                        p_w[m, rows, :] = p.reshape(sub, tq).astype(BF16)
            if do_c:
                rows = pl.ds(pl.multiple_of(c * kc, kc), kc)
                a = (p_r[0, rows, :].reshape(kc // 16, 16, tq) * c16[0][None]
                     - p_r[1, rows, :].reshape(kc // 16, 16, tq) * c16[1][None])
                part_o = _dot(vt_ref[0, h_c, c], a.reshape(kc, tq))
                acc = part_o if acc is None else acc + part_o
            return mx[0], mx[1], ls[0], ls[1], acc

        lowest = jnp.full((8, tq), -jnp.inf, F32)
        nothing = jnp.zeros((8, tq), F32)
        def group(gi, carry):
            stats = carry + (None,)
            for c in range(ATTN_GROUP):
                stats = chunk(gi * ATTN_GROUP + c, stats)
            if do_c:
                acc_ref[...] += stats[4]
            return stats[:4]

        if do_c:
            acc_ref[...] = jnp.zeros_like(acc_ref)
        mx0, mx1, ls0, ls1 = lax.fori_loop(0, n_chunks // ATTN_GROUP, group, (lowest, lowest, nothing, nothing))
        acc = acc_ref[...] if do_c else None
        if do_a:
            mx_ref[par, 0] = jnp.broadcast_to(jnp.max(mx0, axis=0, keepdims=True), (8, tq))
            mx_ref[par, 1] = jnp.broadcast_to(jnp.max(mx1, axis=0, keepdims=True), (8, tq))
        if do_b:
            l0 = jnp.sum(ls0, axis=0, keepdims=True)
            l1 = jnp.sum(ls1, axis=0, keepdims=True)
            coef_ref[1 - par, 0] = jnp.broadcast_to(1.0 / l0, (16, tq)).astype(BF16)
            coef_ref[1 - par, 1] = jnp.broadcast_to(lam / l1, (16, tq)).astype(BF16)
        if do_c:
            o = acc.T
            ms = jnp.mean(o * o, axis=1, keepdims=True)
            q_start = ((k - 2) % tiles) * tq
            q_rows = pl.ds(q_start if isinstance(q_start, int) else pl.multiple_of(q_start, tq), tq)
            o_ref[0, h_c, q_rows, :] = (o * lax.rsqrt(ms + LN_EPS) * gain).astype(BF16)

    def step_pair(j, carry):
        step(2 * j, 0)
        step(2 * j + 1, 1)
        return carry

    step(0, 0, do_b=False, do_c=False)
    step(1, 1, do_c=False)
    lax.fori_loop(1, n_units // 2, step_pair, 0)
    step(n_units, 0, do_a=False)
    step(n_units + 1, 1, do_a=False, do_b=False)


def _diff_attention(qt, k, vt, lam_params, g, lam_init):
    b, _, s, _ = k.shape
    whole = lambda a: pl.BlockSpec((1,) + a.shape[1:], lambda bi: (bi,) + (0,) * (a.ndim - 1))
    return pl.pallas_call(
        functools.partial(_attn_kernel, lam_init=lam_init),
        grid=(b,),
        in_specs=[
            pl.BlockSpec((4, DIFF_QK_DIM), lambda bi: (0, 0)),
            pl.BlockSpec((1, DIFF_V_DIM), lambda bi: (0, 0)),
            whole(qt), whole(k), whole(vt),
        ],
        out_specs=whole(k),
        out_shape=jax.ShapeDtypeStruct((b, DIFF_HEADS, s, LANES), BF16),
        scratch_shapes=[
            pltpu.VMEM((2, s, ATTN_TQ), F32), pltpu.VMEM((2, s, ATTN_TQ), F32),
            pltpu.VMEM((2, s, ATTN_TQ), BF16), pltpu.VMEM((2, s, ATTN_TQ), BF16),
            pltpu.VMEM((2, 2, 8, ATTN_TQ), F32),
            pltpu.VMEM((2, 2, 16, ATTN_TQ), BF16),
            pltpu.VMEM((DIFF_V_DIM, ATTN_TQ), F32),
        ],
        compiler_params=_params(1),
        name="diff_attn",
    )(lam_params, g, qt, k, vt)


def _fourier_kernel(u_ref, cc_ref, sc_ref, f_ref, w_ref, o_ref, ab_ref, *, scale):
    s = u_ref.shape[1]

    @pl.when(pl.program_id(1) == 0)
    def _():
        u = u_ref[0]
        ab_ref[0:s, :] = _dot(u, cc_ref[...]).astype(BF16)
        ab_ref[s:2 * s, :] = _dot(u, sc_ref[...]).astype(BF16)

    z = _dot(f_ref[...], ab_ref[...]) * scale
    o_ref[0] = _dot(z.astype(BF16), w_ref[...]).astype(BF16)


def _fourier(u, cc_bd, sc_bd, f_mat, w_bd, tk=512):
    b, s, _ = u.shape
    scale = 1.0 / math.sqrt(s * FOURIER_GROUP_DIM)
    const2 = lambda bi, i: (0, 0)
    return pl.pallas_call(
        functools.partial(_fourier_kernel, scale=scale),
        grid=(b, s // tk),
        in_specs=[
            pl.BlockSpec((1, s, FOURIER_WIDTH), lambda bi, i: (bi, 0, 0)),
            pl.BlockSpec((FOURIER_WIDTH, FOURIER_WIDTH), const2),
            pl.BlockSpec((FOURIER_WIDTH, FOURIER_WIDTH), const2),
            pl.BlockSpec((tk, 2 * s), lambda bi, i: (i, 0)),
            pl.BlockSpec((FOURIER_WIDTH, FOURIER_WIDTH), const2),
        ],
        out_specs=pl.BlockSpec((1, tk, FOURIER_WIDTH), lambda bi, i: (bi, i, 0)),
        out_shape=jax.ShapeDtypeStruct((b, s, FOURIER_WIDTH), BF16),
        scratch_shapes=[pltpu.VMEM((2 * s, FOURIER_WIDTH), BF16)],
        compiler_params=_params(2),
        name="fourier",
    )(u, cc_bd, sc_bd, f_mat, w_bd)


def _split3(g):
    hi = g.astype(BF16)
    r1 = g - hi.astype(F32)
    mid = r1.astype(BF16)
    lo = (r1 - mid.astype(F32)).astype(BF16)
    return hi, mid, lo


def _gla_kernel(qk_ref, v_ref, r_ref, z_ref, w2_ref, b2_ref, g_ref, o_ref,
                la_ref, vt_ref, acc_ref, st_ref):
    s = v_ref.shape[1]
    c = GLA_CHUNK
    pair = 2 * c
    n_pairs = s // pair
    kc = GLA_K_COLS

    logit = _dot(z_ref[0].astype(BF16), w2_ref[...]) + b2_ref[...]
    la_ref[...] = (jnp.minimum(logit, 0.0) - jnp.log1p(jnp.exp(-jnp.abs(logit)))) * (1.0 / GLA_GATE_TAU)
    vt_ref[...] = v_ref[0].T.astype(BF16)

    row_i = lax.broadcasted_iota(jnp.int32, (c, c), 0)
    col_i = lax.broadcasted_iota(jnp.int32, (c, c), 1)
    r_hk = lax.broadcasted_iota(jnp.int32, (GLA_HEADS * c, kc), 0) // c
    l_hk = lax.broadcasted_iota(jnp.int32, (GLA_HEADS * c, kc), 1) // GLA_K_DIM
    blk_k = r_hk == l_hk
    r_hv = lax.broadcasted_iota(jnp.int32, (GLA_HEADS * c, GLA_WIDTH), 0) // c
    l_hv = lax.broadcasted_iota(jnp.int32, (GLA_HEADS * c, GLA_WIDTH), 1) // GLA_V_DIM
    blk_v = r_hv == l_hv
    qi = lax.broadcasted_iota(jnp.int32, (c, GLA_HEADS * c), 0)
    kj = lax.broadcasted_iota(jnp.int32, (c, GLA_HEADS * c), 1) % c
    q_scale = GLA_K_DIM ** -0.5

    def direction(backward):
        if backward:
            cum_mat = (col_i >= row_i).astype(BF16)
            causal = kj >= qi
            gate_lo = kc
        else:
            cum_mat = (col_i <= row_i).astype(BF16)
            causal = kj <= qi
            gate_lo = 0
        st_ref[...] = jnp.zeros_like(st_ref)

        def pair_step(t, carry):
            p = (n_pairs - 1 - t) if backward else t
            rows = pl.ds(pl.multiple_of(p * pair, pair), pair)
            q2 = qk_ref[0, rows, 0:kc] * q_scale
            k2 = qk_ref[0, rows, kc:2 * kc]
            g2 = la_ref[rows, gate_lo:gate_lo + kc]
            v2 = v_ref[0, rows, :]
            vt2 = vt_ref[:, rows]
            outs = [None, None]
            for ci in ((1, 0) if backward else (0, 1)):
                sl = slice(ci * c, (ci + 1) * c)
                hi, mid, lo = _split3(g2[sl])
                c3 = _dot(cum_mat, jnp.concatenate([hi, mid, lo], axis=1))
                bcum = c3[:, 0:kc] + c3[:, kc:2 * kc] + c3[:, 2 * kc:3 * kc]
                eb = jnp.exp(bcum)
                qt = (q2[sl] * eb).astype(BF16)
                kt = k2[sl] * jnp.exp(-bcum)
                decay = eb[0:1] if backward else eb[c - 1:c]
                kt_bd = jnp.where(blk_k, jnp.concatenate([kt] * GLA_HEADS, axis=0), 0.0).astype(BF16)
                a = _dot_nt(qt, kt_bd)
                a = jnp.where(causal, a, 0.0).astype(BF16)
                v_bd = jnp.where(blk_v, jnp.concatenate([v2[sl]] * GLA_HEADS, axis=0), 0.0).astype(BF16)
                st = st_ref[...]
                o_c = _dot(a, v_bd) + _dot_nt(qt, st.astype(BF16))
                outs[ci] = o_c
                kt_pair = jnp.concatenate(
                    [kt if ci == 0 else jnp.zeros_like(kt), kt if ci == 1 else jnp.zeros_like(kt)],
                    axis=0).astype(BF16)
                kv = _dot(vt2, kt_pair)
                st_ref[...] = (st + jnp.where(blk_k, kv, 0.0)) * decay
            o2 = jnp.concatenate(outs, axis=0)
            if backward:
                acc_ref[rows, :] += o2
            else:
                acc_ref[rows, :] = o2
            return carry

        lax.fori_loop(0, n_pairs, pair_step, 0)

    direction(False)
    direction(True)

    o = acc_ref[...]
    o2 = o * o
    o2_hi = o2.astype(BF16)
    o2_lo = (o2 - o2_hi.astype(F32)).astype(BF16)
    gi = lax.broadcasted_iota(jnp.int32, (GLA_WIDTH, GLA_WIDTH), 0) // GLA_V_DIM
    gj = lax.broadcasted_iota(jnp.int32, (GLA_WIDTH, GLA_WIDTH), 1) // GLA_V_DIM
    ones_bd = (gi == gj).astype(BF16)
    ms = (_dot(o2_hi, ones_bd) + _dot(o2_lo, ones_bd)) * (1.0 / GLA_V_DIM)
    r = r_ref[0]
    gate = r * (1.0 / (1.0 + jnp.exp(-r)))
    o_ref[0] = (o * lax.rsqrt(ms + LN_EPS) * g_ref[...] * gate).astype(BF16)


def _gla(gqk, gv, gr, gz, w2_bd, b2_cat, g_tiled):
    b, s, _ = gv.shape
    per_b = lambda bi: (bi, 0, 0)
    const = lambda bi: (0, 0)
    return pl.pallas_call(
        _gla_kernel,
        grid=(b,),
        in_specs=[
            pl.BlockSpec((1, s, 2 * GLA_K_COLS), per_b),
            pl.BlockSpec((1, s, GLA_WIDTH), per_b),
            pl.BlockSpec((1, s, GLA_WIDTH), per_b),
            pl.BlockSpec((1, s, 2 * GLA_GATE_RANK), per_b),
            pl.BlockSpec((2 * GLA_GATE_RANK, 2 * GLA_K_COLS), const),
            pl.BlockSpec((1, 2 * GLA_K_COLS), const),
            pl.BlockSpec((1, GLA_WIDTH), const),
        ],
        out_specs=pl.BlockSpec((1, s, GLA_WIDTH), per_b),
        out_shape=jax.ShapeDtypeStruct((b, s, GLA_WIDTH), BF16),
        scratch_shapes=[
            pltpu.VMEM((s, 2 * GLA_K_COLS), F32),
            pltpu.VMEM((GLA_WIDTH, s), BF16),
            pltpu.VMEM((s, GLA_WIDTH), F32),
            pltpu.VMEM((GLA_WIDTH, GLA_K_COLS), F32),
        ],
        compiler_params=_params(1),
        name="gla",
    )(gqk, gv, gr, gz, w2_bd, b2_cat, g_tiled)


def _layer_norm(y, g, b):
    mu = jnp.mean(y, axis=1, keepdims=True)
    d = y - mu
    var = jnp.mean(d * d, axis=1, keepdims=True)
    return d * lax.rsqrt(var + LN_EPS) * g + b


def _outproj_kernel(od_ref, of_ref, og_ref, x_ref, w_ref, g_ref, b_ref, o_ref):
    mixed = jnp.concatenate([od_ref[0, hd] for hd in range(DIFF_HEADS)] + [of_ref[...], og_ref[...]], axis=1)
    m = _dot(mixed, w_ref[...])
    o_ref[...] = _layer_norm(DEEPNORM_ALPHA * x_ref[...] + m, g_ref[...], b_ref[...])


def _out_proj(o_diff, o_four, o_gla, x2d, w_bf, g, b, tm=512):
    m = x2d.shape[0]
    seq = o_diff.shape[2]
    n_seq_tiles = seq // tm
    row = lambda i: (i, 0)
    const = lambda i: (0, 0)
    return pl.pallas_call(
        _outproj_kernel,
        grid=(m // tm,),
        in_specs=[
            pl.BlockSpec((1, DIFF_HEADS, tm, LANES), lambda i: (i // n_seq_tiles, 0, i % n_seq_tiles, 0)),
            pl.BlockSpec((tm, FOURIER_WIDTH), row),
            pl.BlockSpec((tm, GLA_WIDTH), row),
            pl.BlockSpec((tm, D_MODEL), row),
            pl.BlockSpec((D_MODEL, D_MODEL), const),
            pl.BlockSpec((1, D_MODEL), const),
            pl.BlockSpec((1, D_MODEL), const),
        ],
        out_specs=pl.BlockSpec((tm, D_MODEL), row),
        out_shape=jax.ShapeDtypeStruct((m, D_MODEL), F32),
        compiler_params=_params(1),
        name="out_proj",
    )(o_diff, o_four, o_gla, x2d, w_bf, g, b)


FFN_CHUNK = 256


def _ffn_kernel(x_ref, wg_ref, wu_ref, wd_ref, g_ref, b_ref, o_ref, acc_ref):
    x = x_ref[...]
    xb = x.astype(BF16)
    for ci in range(FFN_HIDDEN // FFN_CHUNK):
        cols = slice(ci * FFN_CHUNK, (ci + 1) * FFN_CHUNK)
        hg = _dot(xb, wg_ref[:, cols])
        hu = _dot(xb, wu_ref[:, cols])
        act = (hg * (1.0 / (1.0 + jnp.exp(-hg))) * hu).astype(BF16)
        part = _dot(act, wd_ref[cols, :])
        if ci == 0:
            acc_ref[...] = part
        else:
            acc_ref[...] += part
    o_ref[...] = _layer_norm(DEEPNORM_ALPHA * x + acc_ref[...], g_ref[...], b_ref[...])


def _ffn(x2d, wg_bf, wu_bf, wd_bf, g, b, tm=512):
    m = x2d.shape[0]
    row = lambda i: (i, 0)
    const = lambda i: (0, 0)
    return pl.pallas_call(
        _ffn_kernel,
        grid=(m // tm,),
        in_specs=[
            pl.BlockSpec((tm, D_MODEL), row),
            pl.BlockSpec((D_MODEL, FFN_HIDDEN), const),
            pl.BlockSpec((D_MODEL, FFN_HIDDEN), const),
            pl.BlockSpec((FFN_HIDDEN, D_MODEL), const),
            pl.BlockSpec((1, D_MODEL), const),
            pl.BlockSpec((1, D_MODEL), const),
        ],
        out_specs=pl.BlockSpec((tm, D_MODEL), row),
        out_shape=jax.ShapeDtypeStruct((m, D_MODEL), F32),
        scratch_shapes=[pltpu.VMEM((tm, D_MODEL), F32)],
        compiler_params=_params(1),
        name="ffn",
    )(x2d, wg_bf, wu_bf, wd_bf, g, b)


def _rope_tables(seq):
    half = DIFF_QK_DIM // 2
    pos = jnp.arange(seq, dtype=F32)
    inv_freq = ROPE_THETA ** (-jnp.arange(0, DIFF_QK_DIM, 2, dtype=F32) / DIFF_QK_DIM)
    ang = pos[:, None] * inv_freq[None, :]
    cos, sin = jnp.cos(ang), jnp.sin(ang)
    reps = LANES // DIFF_QK_DIM
    cos128 = jnp.tile(jnp.concatenate([cos, cos], axis=1), (1, reps))
    sin128 = jnp.tile(jnp.concatenate([-sin, sin], axis=1), (1, reps))
    assert cos128.shape == (seq, LANES) and half * 2 == DIFF_QK_DIM
    return cos128, sin128, cos128.T, sin128.T


def _dft_tables(seq):
    k = np.arange(seq, dtype=np.int64)
    ang = 2.0 * np.pi * ((k[:, None] * k[None, :]) % seq).astype(np.float64) / seq
    f_mat = np.concatenate([np.cos(ang), -np.sin(ang)], axis=1)
    c = np.arange(FOURIER_GROUP_DIM, dtype=np.int64)
    ang_c = 2.0 * np.pi * ((c[:, None] * c[None, :]) % FOURIER_GROUP_DIM).astype(np.float64) / FOURIER_GROUP_DIM
    eye = np.eye(FOURIER_GROUPS)
    cc_bd = np.kron(eye, np.cos(ang_c))
    sc_bd = np.kron(eye, np.sin(ang_c))
    return tuple(jnp.asarray(t, dtype=F32).astype(BF16) for t in (f_mat, cc_bd, sc_bd))


def _block_diag(blocks):
    n = len(blocks)
    rows = []
    for i, blk in enumerate(blocks):
        rows.append(jnp.concatenate(
            [blk if j == i else jnp.zeros((blk.shape[0], blocks[j].shape[1]), blk.dtype) for j in range(n)],
            axis=1))
    return jnp.concatenate(rows, axis=0)


def kernel(x, w_in, diff_lambda, diff_norm_g, fourier_w, gla_gate_w2, gla_gate_b2, gla_norm_g, w_out,
           ln1_g, ln1_b, ffn_w_gate, ffn_w_up, ffn_w_down, ln2_g, ln2_b):
    b, s, d = x.shape
    m = b * s
    rope = _rope_tables(s)
    f_mat, cc_bd, sc_bd = _dft_tables(s)
    x2d = x.reshape(m, d)
    for l in range(DEPTH):
        lam_init = 0.8 - 0.6 * math.exp(-0.3 * l)
        w_bf = w_in[l].astype(BF16)
        wqv_t = jnp.concatenate([w_bf[:, OFF_DQ:OFF_DK], w_bf[:, OFF_DV:OFF_FU]], axis=1).T
        qt, k, vt, fu, gqk, gv, gr, gz = _in_proj(x2d, wqv_t, w_bf, rope, s)
        o_diff = _diff_attention(qt, k, vt, diff_lambda[l], diff_norm_g[l].reshape(1, -1), lam_init)
        w_four = _block_diag([fourier_w[l, g] for g in range(FOURIER_GROUPS)]).astype(BF16)
        o_four = _fourier(fu.reshape(b, s, -1), cc_bd, sc_bd, f_mat, w_four)
        w2_bd = _block_diag([gla_gate_w2[l, 0], gla_gate_w2[l, 1]]).astype(BF16)
        b2_cat = gla_gate_b2[l].reshape(1, -1)
        g_gla = jnp.tile(gla_norm_g[l], GLA_HEADS).reshape(1, -1)
        o_gla = _gla(gqk.reshape(b, s, -1), gv.reshape(b, s, -1), gr.reshape(b, s, -1),
                     gz.reshape(b, s, -1), w2_bd, b2_cat, g_gla)
        x2d = _out_proj(o_diff, o_four.reshape(m, -1), o_gla.reshape(m, -1), x2d,
                        w_out[l].astype(BF16), ln1_g[l].reshape(1, -1), ln1_b[l].reshape(1, -1))
        x2d = _ffn(x2d, ffn_w_gate[l].astype(BF16), ffn_w_up[l].astype(BF16), ffn_w_down[l].astype(BF16),
                   ln2_g[l].reshape(1, -1), ln2_b[l].reshape(1, -1))
    return x2d.reshape(b, s, d)
```

```python
import functools
import math

import jax
import jax.numpy as jnp
import numpy as np
from jax import lax
from jax.experimental import pallas as pl
from jax.experimental.pallas import tpu as pltpu

D_MODEL = 1024
DEPTH = 2
DIFF_HEADS = 4
DIFF_QK_DIM = 64
DIFF_V_DIM = 128
DIFF_WIDTH = 512
DIFF_QK_COLS = 512
ROPE_THETA = 10000.0
FOURIER_GROUPS = 4
FOURIER_GROUP_DIM = 64
FOURIER_WIDTH = 256
GLA_HEADS = 4
GLA_V_DIM = 64
GLA_K_DIM = 32
GLA_WIDTH = 256
GLA_K_COLS = 128
GLA_GATE_RANK = 16
GLA_GATE_TAU = 16.0
GLA_CHUNK = 64
IN_WIDTH = 2592
FFN_HIDDEN = 2816
DEEPNORM_ALPHA = (2 * DEPTH) ** 0.25
LN_EPS = 1e-5

OFF_DQ, OFF_DK, OFF_DV, OFF_FU = 0, 512, 1024, 1536
OFF_GQ, OFF_GV, OFF_GR, OFF_GZ = 1792, 2048, 2304, 2560

LANES = 128
VMEM_LIMIT = 56 * 1024 * 1024

BF16 = jnp.bfloat16
F32 = jnp.float32


def _dot(a, b):
    return jnp.dot(a, b, preferred_element_type=F32)


def _dot_nt(a, b):
    return lax.dot_general(a, b, (((1,), (1,)), ((), ())), preferred_element_type=F32)


def _params(n_grid_dims):
    return pltpu.CompilerParams(
        dimension_semantics=("arbitrary",) * n_grid_dims,
        vmem_limit_bytes=VMEM_LIMIT)


def _rope_slab(t, cos, sin_signed, first_half):
    swapped = jnp.where(first_half, pltpu.roll(t, 96, 1), pltpu.roll(t, 32, 1))
    return t * cos + swapped * sin_signed


ATTN_TQ = 256
ATTN_KC = 256


def _inproj_kernel(x_ref, wqv_ref, w_ref, cos_ref, sin_ref, cost_ref, sint_ref,
                   qt_ref, k_ref, vt_ref, fu_ref, gqk_ref, gv_ref, gr_ref, gz_ref):
    xb = x_ref[...].astype(BF16)
    tm = xb.shape[0]
    cos = cos_ref[...]
    sin_signed = sin_ref[...]
    lane = lax.broadcasted_iota(jnp.int32, cos.shape, 1)
    first_half = (lane % DIFF_QK_DIM) < (DIFF_QK_DIM // 2)
    qk_scale = DIFF_QK_DIM ** -0.5 * math.log2(math.e)
    hqv_t = _dot_nt(wqv_ref[...], xb)
    cos_t = cost_ref[...]
    sin_t = sint_ref[...]
    half = DIFF_QK_DIM // 2
    hk = _dot(xb, w_ref[:, OFF_DK:OFF_DV])
    for hd in range(DIFF_HEADS):
        qt = hqv_t[hd * LANES:(hd + 1) * LANES]
        swapped = jnp.concatenate(
            [qt[half:2 * half], qt[0:half], qt[3 * half:4 * half], qt[2 * half:3 * half]], axis=0)
        qt = ((qt * cos_t + swapped * sin_t) * qk_scale).astype(BF16)
        vt = hqv_t[DIFF_QK_COLS + hd * LANES:DIFF_QK_COLS + (hd + 1) * LANES].astype(BF16)
        for j in range(tm // ATTN_TQ):
            qt_ref[0, hd, j] = qt[:, j * ATTN_TQ:(j + 1) * ATTN_TQ]
        for j in range(tm // ATTN_KC):
            vt_ref[0, hd, j] = vt[:, j * ATTN_KC:(j + 1) * ATTN_KC]
        slab = slice(hd * LANES, (hd + 1) * LANES)
        k_ref[0, hd] = _rope_slab(hk[:, slab], cos, sin_signed, first_half).astype(BF16)
    fu_ref[...] = _dot(xb, w_ref[:, OFF_FU:OFF_GQ]).astype(BF16)
    gqk_ref[...] = _dot(xb, w_ref[:, OFF_GQ:OFF_GV])
    gv_ref[...] = _dot(xb, w_ref[:, OFF_GV:OFF_GR])
    gr_ref[...] = _dot(xb, w_ref[:, OFF_GR:OFF_GZ])
    gz_ref[...] = _dot(xb, w_ref[:, OFF_GZ:IN_WIDTH])


def _in_proj(x2d, wqv_t, w_bf, rope, seq, tm=512):
    m = x2d.shape[0]
    batch = m // seq
    n_seq_tiles = seq // tm
    cos128, sin128, cos_t, sin_t = rope
    row = lambda i: (i, 0)
    const = lambda i: (0, 0)
    tab = lambda i: (i % n_seq_tiles, 0)
    tab_t = lambda i: (0, i % n_seq_tiles)
    tiled = lambda t: (jax.ShapeDtypeStruct((batch, DIFF_HEADS, seq // t, LANES, t), BF16),
                       pl.BlockSpec((1, DIFF_HEADS, tm // t, LANES, t),
                                    lambda i: (i // n_seq_tiles, 0, i % n_seq_tiles, 0, 0)))
    qt_shape, qt_spec = tiled(ATTN_TQ)
    vt_shape, vt_spec = tiled(ATTN_KC)
    k_shape = jax.ShapeDtypeStruct((batch, DIFF_HEADS, seq, LANES), BF16)
    k_spec = pl.BlockSpec((1, DIFF_HEADS, tm, LANES), lambda i: (i // n_seq_tiles, 0, i % n_seq_tiles, 0))
    flat_shapes = (
        jax.ShapeDtypeStruct((m, FOURIER_WIDTH), BF16),
        jax.ShapeDtypeStruct((m, 2 * GLA_K_COLS), F32),
        jax.ShapeDtypeStruct((m, GLA_WIDTH), F32),
        jax.ShapeDtypeStruct((m, GLA_WIDTH), F32),
        jax.ShapeDtypeStruct((m, 2 * GLA_GATE_RANK), F32),
    )
    return pl.pallas_call(
        _inproj_kernel,
        grid=(m // tm,),
        in_specs=[
            pl.BlockSpec((tm, D_MODEL), row),
            pl.BlockSpec((DIFF_QK_COLS + DIFF_WIDTH, D_MODEL), const),
            pl.BlockSpec((D_MODEL, IN_WIDTH), const),
            pl.BlockSpec((tm, LANES), tab),
            pl.BlockSpec((tm, LANES), tab),
            pl.BlockSpec((LANES, tm), tab_t),
            pl.BlockSpec((LANES, tm), tab_t),
        ],
        out_specs=(qt_spec, k_spec, vt_spec) + tuple(pl.BlockSpec((tm, s.shape[1]), row) for s in flat_shapes),
        out_shape=(qt_shape, k_shape, vt_shape) + flat_shapes,
        compiler_params=_params(1),
        name="in_proj",
    )(x2d, wqv_t, w_bf, cos128, sin128, cos_t, sin_t)


ATTN_SUB = 128
ATTN_GROUP = 4


def _attn_kernel(lam_ref, g_ref, qt_ref, k_ref, vt_ref, o_ref,
                 sa_ref, sb_ref, pa_ref, pb_ref, mx_ref, coef_ref, acc_ref, *, lam_init):
    tq, kc, sub = ATTN_TQ, ATTN_KC, ATTN_SUB
    seq = k_ref.shape[2]
    tiles = seq // tq
    n_units = DIFF_HEADS * tiles
    n_chunks = seq // kc
    lp = lam_ref[...]
    lam = (jnp.exp(jnp.sum(lp[0:1] * lp[1:2], axis=1, keepdims=True))
           - jnp.exp(jnp.sum(lp[2:3] * lp[3:4], axis=1, keepdims=True)) + lam_init)
    gain = g_ref[...] * (1.0 - lam_init)
    feature = lax.broadcasted_iota(jnp.int32, (LANES, tq), 0)
    first_map = feature < DIFF_QK_DIM
    s_bufs = (sa_ref, sb_ref)
    p_bufs = (pa_ref, pb_ref)

    def step(k, par, do_a=True, do_b=True, do_c=True):
        s_w, s_r = s_bufs[par], s_bufs[1 - par]
        p_w, p_r = p_bufs[1 - par], p_bufs[par]
        if do_a:
            h_a = k // tiles
            qt = qt_ref[0, h_a, k % tiles]
            zero = jnp.zeros_like(qt)
            qw = (jnp.where(first_map, qt, zero), jnp.where(first_map, zero, qt))
        if do_b:
            m8 = (mx_ref[1 - par, 0], mx_ref[1 - par, 1])
        if do_c:
            h_c = (k - 2) // tiles
            c16 = (coef_ref[par, 0], coef_ref[par, 1])

        def chunk(c, carry):
            mx = [carry[0], carry[1]]
            ls = [carry[2], carry[3]]
            acc = carry[4]
            for part in range(kc // sub):
                rows = pl.ds(pl.multiple_of(c * kc + part * sub, sub), sub)
                if do_a:
                    keys = k_ref[0, h_a, rows, :]
                    for m in range(2):
                        st = _dot(keys, qw[m])
                        s_w[m, rows, :] = st
                        mx[m] = jnp.maximum(mx[m], jnp.max(st.reshape(sub // 8, 8, tq), axis=0))
                if do_b:
                    for m in range(2):
                        p = jnp.exp2(s_r[m, rows, :].reshape(sub // 8, 8, tq) - m8[m][None])
                        ls[m] = ls[m] + jnp.sum(p, axis=0)
                        p_w[m, rows, :] = p.reshape(sub, tq).astype(BF16)
            if do_c:
                rows = pl.ds(pl.multiple_of(c * kc, kc), kc)
                a = (p_r[0, rows, :].reshape(kc // 16, 16, tq) * c16[0][None]
                     - p_r[1, rows, :].reshape(kc // 16, 16, tq) * c16[1][None])
                part_o = _dot(vt_ref[0, h_c, c], a.reshape(kc, tq))
                acc = part_o if acc is None else acc + part_o
            return mx[0], mx[1], ls[0], ls[1], acc

        lowest = jnp.full((8, tq), -jnp.inf, F32)
        nothing = jnp.zeros((8, tq), F32)
        def group(gi, carry):
            stats = carry + (None,)
            for c in range(ATTN_GROUP):
                stats = chunk(gi * ATTN_GROUP + c, stats)
            if do_c:
                acc_ref[...] += stats[4]
            return stats[:4]

        if do_c:
            acc_ref[...] = jnp.zeros_like(acc_ref)
        mx0, mx1, ls0, ls1 = lax.fori_loop(0, n_chunks // ATTN_GROUP, group, (lowest, lowest, nothing, nothing))
        acc = acc_ref[...] if do_c else None
        if do_a:
            mx_ref[par, 0] = jnp.broadcast_to(jnp.max(mx0, axis=0, keepdims=True), (8, tq))
            mx_ref[par, 1] = jnp.broadcast_to(jnp.max(mx1, axis=0, keepdims=True), (8, tq))
        if do_b:
            l0 = jnp.sum(ls0, axis=0, keepdims=True)
            l1 = jnp.sum(ls1, axis=0, keepdims=True)
            coef_ref[1 - par, 0] = jnp.broadcast_to(1.0 / l0, (16, tq)).astype(BF16)
            coef_ref[1 - par, 1] = jnp.broadcast_to(lam / l1, (16, tq)).astype(BF16)
        if do_c:
            o = acc.T
            ms = jnp.mean(o * o, axis=1, keepdims=True)
            q_start = ((k - 2) % tiles) * tq
            q_rows = pl.ds(q_start if isinstance(q_start, int) else pl.multiple_of(q_start, tq), tq)
            o_ref[0, h_c, q_rows, :] = (o * lax.rsqrt(ms + LN_EPS) * gain).astype(BF16)

    def step_pair(j, carry):
        step(2 * j, 0)
        step(2 * j + 1, 1)
        return carry

    step(0, 0, do_b=False, do_c=False)
    step(1, 1, do_c=False)
    lax.fori_loop(1, n_units // 2, step_pair, 0)
    step(n_units, 0, do_a=False)
    step(n_units + 1, 1, do_a=False, do_b=False)


def _diff_attention(qt, k, vt, lam_params, g, lam_init):
    b, _, s, _ = k.shape
    whole = lambda a: pl.BlockSpec((1,) + a.shape[1:], lambda bi: (bi,) + (0,) * (a.ndim - 1))
    return pl.pallas_call(
        functools.partial(_attn_kernel, lam_init=lam_init),
        grid=(b,),
        in_specs=[
            pl.BlockSpec((4, DIFF_QK_DIM), lambda bi: (0, 0)),
            pl.BlockSpec((1, DIFF_V_DIM), lambda bi: (0, 0)),
            whole(qt), whole(k), whole(vt),
        ],
        out_specs=whole(k),
        out_shape=jax.ShapeDtypeStruct((b, DIFF_HEADS, s, LANES), BF16),
        scratch_shapes=[
            pltpu.VMEM((2, s, ATTN_TQ), F32), pltpu.VMEM((2, s, ATTN_TQ), F32),
            pltpu.VMEM((2, s, ATTN_TQ), BF16), pltpu.VMEM((2, s, ATTN_TQ), BF16),
            pltpu.VMEM((2, 2, 8, ATTN_TQ), F32),
            pltpu.VMEM((2, 2, 16, ATTN_TQ), BF16),
            pltpu.VMEM((DIFF_V_DIM, ATTN_TQ), F32),
        ],
        compiler_params=_params(1),
        name="diff_attn",
    )(lam_params, g, qt, k, vt)


def _fourier_kernel(u_ref, cc_ref, sc_ref, f_ref, w_ref, o_ref, ab_ref, *, scale):
    s = u_ref.shape[1]

    @pl.when(pl.program_id(1) == 0)
    def _():
        u = u_ref[0]
        ab_ref[0:s, :] = _dot(u, cc_ref[...]).astype(BF16)
        ab_ref[s:2 * s, :] = _dot(u, sc_ref[...]).astype(BF16)

    z = _dot(f_ref[...], ab_ref[...]) * scale
    o_ref[0] = _dot(z.astype(BF16), w_ref[...]).astype(BF16)


def _fourier(u, cc_bd, sc_bd, f_mat, w_bd, tk=512):
    b, s, _ = u.shape
    scale = 1.0 / math.sqrt(s * FOURIER_GROUP_DIM)
    const2 = lambda bi, i: (0, 0)
    return pl.pallas_call(
        functools.partial(_fourier_kernel, scale=scale),
        grid=(b, s // tk),
        in_specs=[
            pl.BlockSpec((1, s, FOURIER_WIDTH), lambda bi, i: (bi, 0, 0)),
            pl.BlockSpec((FOURIER_WIDTH, FOURIER_WIDTH), const2),
            pl.BlockSpec((FOURIER_WIDTH, FOURIER_WIDTH), const2),
            pl.BlockSpec((tk, 2 * s), lambda bi, i: (i, 0)),
            pl.BlockSpec((FOURIER_WIDTH, FOURIER_WIDTH), const2),
        ],
        out_specs=pl.BlockSpec((1, tk, FOURIER_WIDTH), lambda bi, i: (bi, i, 0)),
        out_shape=jax.ShapeDtypeStruct((b, s, FOURIER_WIDTH), BF16),
        scratch_shapes=[pltpu.VMEM((2 * s, FOURIER_WIDTH), BF16)],
        compiler_params=_params(2),
        name="fourier",
    )(u, cc_bd, sc_bd, f_mat, w_bd)


def _split3(g):
    hi = g.astype(BF16)
    r1 = g - hi.astype(F32)
    mid = r1.astype(BF16)
    lo = (r1 - mid.astype(F32)).astype(BF16)
    return hi, mid, lo


def _gla_kernel(qk_ref, v_ref, r_ref, z_ref, w2_ref, b2_ref, g_ref, o_ref,
                la_ref, vt_ref, acc_ref, qt_ref, kt_ref, a_ref, kv_ref, dec_ref, st_ref):
    s = v_ref.shape[1]
    c = GLA_CHUNK
    pair = 2 * c
    n_pairs = s // pair
    n_chunks = s // c
    kc = GLA_K_COLS

    logit = _dot(z_ref[0].astype(BF16), w2_ref[...]) + b2_ref[...]
    la_ref[...] = (jnp.minimum(logit, 0.0) - jnp.log1p(jnp.exp(-jnp.abs(logit)))) * (1.0 / GLA_GATE_TAU)
    vt_ref[...] = v_ref[0].T.astype(BF16)

    row_i = lax.broadcasted_iota(jnp.int32, (pair, pair), 0)
    col_i = lax.broadcasted_iota(jnp.int32, (pair, pair), 1)
    same_chunk = (row_i // c) == (col_i // c)
    cum_mats = ((same_chunk & (col_i <= row_i)).astype(BF16), (same_chunk & (col_i >= row_i)).astype(BF16))
    r_hk = lax.broadcasted_iota(jnp.int32, (GLA_HEADS * c, kc), 0) // c
    l_hk = lax.broadcasted_iota(jnp.int32, (GLA_HEADS * c, kc), 1) // GLA_K_DIM
    blk_k = r_hk == l_hk
    r_hv = lax.broadcasted_iota(jnp.int32, (GLA_HEADS * c, GLA_WIDTH), 0) // c
    l_hv = lax.broadcasted_iota(jnp.int32, (GLA_HEADS * c, GLA_WIDTH), 1) // GLA_V_DIM
    blk_v = r_hv == l_hv
    qi = lax.broadcasted_iota(jnp.int32, (c, GLA_HEADS * c), 0)
    kj = lax.broadcasted_iota(jnp.int32, (c, GLA_HEADS * c), 1) % c
    q_scale = GLA_K_DIM ** -0.5

    causal = (kj <= qi, kj >= qi)

    def pair_rows(p):
        start = p * pair
        return pl.ds(start if isinstance(start, int) else pl.multiple_of(start, pair), pair)

    def decayed_qk(p):
        rows = pair_rows(p)
        q2 = qk_ref[0, rows, 0:kc] * q_scale
        k2 = qk_ref[0, rows, kc:2 * kc]
        for d in range(2):
            hi, mid, lo = _split3(la_ref[rows, d * kc:(d + 1) * kc])
            c3 = _dot(cum_mats[d], jnp.concatenate([hi, mid, lo], axis=1))
            bcum = c3[:, 0:kc] + c3[:, kc:2 * kc] + c3[:, 2 * kc:3 * kc]
            eb = jnp.exp(bcum)
            qt_ref[d, rows, :] = (q2 * eb).astype(BF16)
            kt_ref[d, rows, :] = k2 * jnp.exp(-bcum)
            for ci in range(2):
                edge = ci * c if d == 1 else (ci + 1) * c - 1
                dec_ref[d, 2 * p + ci] = jnp.broadcast_to(eb[edge:edge + 1], (8, kc))

    def intra_scores(p):
        for ci in range(2):
            start = p * pair + ci * c
            rows = pl.ds(start if isinstance(start, int) else pl.multiple_of(start, c), c)
            for d in range(2):
                kt_bd = jnp.where(blk_k, jnp.concatenate([kt_ref[d, rows, :]] * GLA_HEADS, axis=0), 0.0)
                a = _dot_nt(qt_ref[d, rows, :], kt_bd.astype(BF16))
                a_ref[2 * p + ci, d * c:(d + 1) * c, :] = jnp.where(causal[d], a, 0.0).astype(BF16)

    def intra_out_and_kv(p):
        rows2 = pair_rows(p)
        v2 = v_ref[0, rows2, :]
        vt2 = vt_ref[:, rows2]
        kt2 = jnp.concatenate([kt_ref[0, rows2, :], kt_ref[1, rows2, :]], axis=1)
        for ci in range(2):
            sl = slice(ci * c, (ci + 1) * c)
            start = p * pair + ci * c
            rows = pl.ds(start if isinstance(start, int) else pl.multiple_of(start, c), c)
            v_bd = jnp.where(blk_v, jnp.concatenate([v2[sl]] * GLA_HEADS, axis=0), 0.0).astype(BF16)
            o_both = _dot(a_ref[2 * p + ci], v_bd)
            acc_ref[rows, :] = o_both[0:c] + o_both[c:2 * c]
            gap = jnp.zeros((c, 2 * kc), F32)
            kt_pair = jnp.concatenate([kt2[sl], gap] if ci == 0 else [gap, kt2[sl]], axis=0).astype(BF16)
            kv = _dot(vt2, kt_pair)
            kv_ref[0, 2 * p + ci] = jnp.where(blk_k, kv[:, 0:kc], 0.0)
            kv_ref[1, 2 * p + ci] = jnp.where(blk_k, kv[:, kc:2 * kc], 0.0)

    decayed_qk(0)
    intra_scores(0)
    decayed_qk(1)

    def intra_step(p, carry):
        intra_out_and_kv(p)
        intra_scores(p + 1)
        decayed_qk(p + 2)
        return carry

    lax.fori_loop(0, n_pairs - 2, intra_step, 0)
    intra_out_and_kv(n_pairs - 2)
    intra_scores(n_pairs - 1)
    intra_out_and_kv(n_pairs - 1)

    st_ref[...] = jnp.zeros_like(st_ref)

    def scan_step(i, carry):
        for d in range(2):
            n = i if d == 0 else n_chunks - 1 - i
            rows = pl.ds(pl.multiple_of(n * c, c), c)
            st = st_ref[d]
            acc_ref[rows, :] += _dot_nt(qt_ref[d, rows, :], st.astype(BF16))
            st = (st + kv_ref[d, n]).reshape(GLA_WIDTH // 8, 8, kc) * dec_ref[d, n][None]
            st_ref[d] = st.reshape(GLA_WIDTH, kc)
        return carry

    lax.fori_loop(0, n_chunks, scan_step, 0, unroll=2)

    o = acc_ref[...]
    o2 = o * o
    o2_hi = o2.astype(BF16)
    o2_lo = (o2 - o2_hi.astype(F32)).astype(BF16)
    gi = lax.broadcasted_iota(jnp.int32, (GLA_WIDTH, GLA_WIDTH), 0) // GLA_V_DIM
    gj = lax.broadcasted_iota(jnp.int32, (GLA_WIDTH, GLA_WIDTH), 1) // GLA_V_DIM
    ones_bd = (gi == gj).astype(BF16)
    ms = (_dot(o2_hi, ones_bd) + _dot(o2_lo, ones_bd)) * (1.0 / GLA_V_DIM)
    r = r_ref[0]
    gate = r * (1.0 / (1.0 + jnp.exp(-r)))
    o_ref[0] = (o * lax.rsqrt(ms + LN_EPS) * g_ref[...] * gate).astype(BF16)


def _gla(gqk, gv, gr, gz, w2_bd, b2_cat, g_tiled):
    b, s, _ = gv.shape
    per_b = lambda bi: (bi, 0, 0)
    const = lambda bi: (0, 0)
    return pl.pallas_call(
        _gla_kernel,
        grid=(b,),
        in_specs=[
            pl.BlockSpec((1, s, 2 * GLA_K_COLS), per_b),
            pl.BlockSpec((1, s, GLA_WIDTH), per_b),
            pl.BlockSpec((1, s, GLA_WIDTH), per_b),
            pl.BlockSpec((1, s, 2 * GLA_GATE_RANK), per_b),
            pl.BlockSpec((2 * GLA_GATE_RANK, 2 * GLA_K_COLS), const),
            pl.BlockSpec((1, 2 * GLA_K_COLS), const),
            pl.BlockSpec((1, GLA_WIDTH), const),
        ],
        out_specs=pl.BlockSpec((1, s, GLA_WIDTH), per_b),
        out_shape=jax.ShapeDtypeStruct((b, s, GLA_WIDTH), BF16),
        scratch_shapes=[
            pltpu.VMEM((s, 2 * GLA_K_COLS), F32),
            pltpu.VMEM((GLA_WIDTH, s), BF16),
            pltpu.VMEM((s, GLA_WIDTH), F32),
            pltpu.VMEM((2, s, GLA_K_COLS), BF16),
            pltpu.VMEM((2, s, GLA_K_COLS), F32),
            pltpu.VMEM((s // GLA_CHUNK, 2 * GLA_CHUNK, GLA_WIDTH), BF16),
            pltpu.VMEM((2, s // GLA_CHUNK, GLA_WIDTH, GLA_K_COLS), F32),
            pltpu.VMEM((2, s // GLA_CHUNK, 8, GLA_K_COLS), F32),
            pltpu.VMEM((2, GLA_WIDTH, GLA_K_COLS), F32),
        ],
        compiler_params=_params(1),
        name="gla",
    )(gqk, gv, gr, gz, w2_bd, b2_cat, g_tiled)


def _layer_norm(y, g, b):
    mu = jnp.mean(y, axis=1, keepdims=True)
    d = y - mu
    var = jnp.mean(d * d, axis=1, keepdims=True)
    return d * lax.rsqrt(var + LN_EPS) * g + b


def _outproj_kernel(od_ref, of_ref, og_ref, x_ref, w_ref, g_ref, b_ref, o_ref):
    mixed = jnp.concatenate([od_ref[0, hd] for hd in range(DIFF_HEADS)] + [of_ref[...], og_ref[...]], axis=1)
    m = _dot(mixed, w_ref[...])
    o_ref[...] = _layer_norm(DEEPNORM_ALPHA * x_ref[...] + m, g_ref[...], b_ref[...])


def _out_proj(o_diff, o_four, o_gla, x2d, w_bf, g, b, tm=512):
    m = x2d.shape[0]
    seq = o_diff.shape[2]
    n_seq_tiles = seq // tm
    row = lambda i: (i, 0)
    const = lambda i: (0, 0)
    return pl.pallas_call(
        _outproj_kernel,
        grid=(m // tm,),
        in_specs=[
            pl.BlockSpec((1, DIFF_HEADS, tm, LANES), lambda i: (i // n_seq_tiles, 0, i % n_seq_tiles, 0)),
            pl.BlockSpec((tm, FOURIER_WIDTH), row),
            pl.BlockSpec((tm, GLA_WIDTH), row),
            pl.BlockSpec((tm, D_MODEL), row),
            pl.BlockSpec((D_MODEL, D_MODEL), const),
            pl.BlockSpec((1, D_MODEL), const),
            pl.BlockSpec((1, D_MODEL), const),
        ],
        out_specs=pl.BlockSpec((tm, D_MODEL), row),
        out_shape=jax.ShapeDtypeStruct((m, D_MODEL), F32),
        compiler_params=_params(1),
        name="out_proj",
    )(o_diff, o_four, o_gla, x2d, w_bf, g, b)


FFN_CHUNK = 256


def _ffn_kernel(x_ref, wg_ref, wu_ref, wd_ref, g_ref, b_ref, o_ref, acc_ref):
    x = x_ref[...]
    xb = x.astype(BF16)
    for ci in range(FFN_HIDDEN // FFN_CHUNK):
        cols = slice(ci * FFN_CHUNK, (ci + 1) * FFN_CHUNK)
        hg = _dot(xb, wg_ref[:, cols])
        hu = _dot(xb, wu_ref[:, cols])
        act = (hg * (1.0 / (1.0 + jnp.exp(-hg))) * hu).astype(BF16)
        part = _dot(act, wd_ref[cols, :])
        if ci == 0:
            acc_ref[...] = part
        else:
            acc_ref[...] += part
    o_ref[...] = _layer_norm(DEEPNORM_ALPHA * x + acc_ref[...], g_ref[...], b_ref[...])


def _ffn(x2d, wg_bf, wu_bf, wd_bf, g, b, tm=512):
    m = x2d.shape[0]
    row = lambda i: (i, 0)
    const = lambda i: (0, 0)
    return pl.pallas_call(
        _ffn_kernel,
        grid=(m // tm,),
        in_specs=[
            pl.BlockSpec((tm, D_MODEL), row),
            pl.BlockSpec((D_MODEL, FFN_HIDDEN), const),
            pl.BlockSpec((D_MODEL, FFN_HIDDEN), const),
            pl.BlockSpec((FFN_HIDDEN, D_MODEL), const),
            pl.BlockSpec((1, D_MODEL), const),
            pl.BlockSpec((1, D_MODEL), const),
        ],
        out_specs=pl.BlockSpec((tm, D_MODEL), row),
        out_shape=jax.ShapeDtypeStruct((m, D_MODEL), F32),
        scratch_shapes=[pltpu.VMEM((tm, D_MODEL), F32)],
        compiler_params=_params(1),
        name="ffn",
    )(x2d, wg_bf, wu_bf, wd_bf, g, b)


def _rope_tables(seq):
    half = DIFF_QK_DIM // 2
    pos = jnp.arange(seq, dtype=F32)
    inv_freq = ROPE_THETA ** (-jnp.arange(0, DIFF_QK_DIM, 2, dtype=F32) / DIFF_QK_DIM)
    ang = pos[:, None] * inv_freq[None, :]
    cos, sin = jnp.cos(ang), jnp.sin(ang)
    reps = LANES // DIFF_QK_DIM
    cos128 = jnp.tile(jnp.concatenate([cos, cos], axis=1), (1, reps))
    sin128 = jnp.tile(jnp.concatenate([-sin, sin], axis=1), (1, reps))
    assert cos128.shape == (seq, LANES) and half * 2 == DIFF_QK_DIM
    return cos128, sin128, cos128.T, sin128.T


def _dft_tables(seq):
    k = np.arange(seq, dtype=np.int64)
    ang = 2.0 * np.pi * ((k[:, None] * k[None, :]) % seq).astype(np.float64) / seq
    f_mat = np.concatenate([np.cos(ang), -np.sin(ang)], axis=1)
    c = np.arange(FOURIER_GROUP_DIM, dtype=np.int64)
    ang_c = 2.0 * np.pi * ((c[:, None] * c[None, :]) % FOURIER_GROUP_DIM).astype(np.float64) / FOURIER_GROUP_DIM
    eye = np.eye(FOURIER_GROUPS)
    cc_bd = np.kron(eye, np.cos(ang_c))
    sc_bd = np.kron(eye, np.sin(ang_c))
    return tuple(jnp.asarray(t, dtype=F32).astype(BF16) for t in (f_mat, cc_bd, sc_bd))


def _block_diag(blocks):
    n = len(blocks)
    rows = []
    for i, blk in enumerate(blocks):
        rows.append(jnp.concatenate(
            [blk if j == i else jnp.zeros((blk.shape[0], blocks[j].shape[1]), blk.dtype) for j in range(n)],
            axis=1))
    return jnp.concatenate(rows, axis=0)


def kernel(x, w_in, diff_lambda, diff_norm_g, fourier_w, gla_gate_w2, gla_gate_b2, gla_norm_g, w_out,
           ln1_g, ln1_b, ffn_w_gate, ffn_w_up, ffn_w_down, ln2_g, ln2_b):
    b, s, d = x.shape
    m = b * s
    rope = _rope_tables(s)
    f_mat, cc_bd, sc_bd = _dft_tables(s)
    x2d = x.reshape(m, d)
    for l in range(DEPTH):
        lam_init = 0.8 - 0.6 * math.exp(-0.3 * l)
        w_bf = w_in[l].astype(BF16)
        wqv_t = jnp.concatenate([w_bf[:, OFF_DQ:OFF_DK], w_bf[:, OFF_DV:OFF_FU]], axis=1).T
        qt, k, vt, fu, gqk, gv, gr, gz = _in_proj(x2d, wqv_t, w_bf, rope, s)
        o_diff = _diff_attention(qt, k, vt, diff_lambda[l], diff_norm_g[l].reshape(1, -1), lam_init)
        w_four = _block_diag([fourier_w[l, g] for g in range(FOURIER_GROUPS)]).astype(BF16)
        o_four = _fourier(fu.reshape(b, s, -1), cc_bd, sc_bd, f_mat, w_four)
        w2_bd = _block_diag([gla_gate_w2[l, 0], gla_gate_w2[l, 1]]).astype(BF16)
        b2_cat = gla_gate_b2[l].reshape(1, -1)
        g_gla = jnp.tile(gla_norm_g[l], GLA_HEADS).reshape(1, -1)
        o_gla = _gla(gqk.reshape(b, s, -1), gv.reshape(b, s, -1), gr.reshape(b, s, -1),
                     gz.reshape(b, s, -1), w2_bd, b2_cat, g_gla)
        x2d = _out_proj(o_diff, o_four.reshape(m, -1), o_gla.reshape(m, -1), x2d,
                        w_out[l].astype(BF16), ln1_g[l].reshape(1, -1), ln1_b[l].reshape(1, -1))
        x2d = _ffn(x2d, ffn_w_gate[l].astype(BF16), ffn_w_up[l].astype(BF16), ffn_w_down[l].astype(BF16),
                   ln2_g[l].reshape(1, -1), ln2_b[l].reshape(1, -1))
    return x2d.reshape(b, s, d)
```

```python
import functools
import math

import jax
import jax.numpy as jnp
import numpy as np
from jax import lax
from jax.experimental import pallas as pl
from jax.experimental.pallas import tpu as pltpu

D_MODEL = 1024
DEPTH = 2
DIFF_HEADS = 4
DIFF_QK_DIM = 64
DIFF_V_DIM = 128
DIFF_WIDTH = 512
DIFF_QK_COLS = 512
ROPE_THETA = 10000.0
FOURIER_GROUPS = 4
FOURIER_GROUP_DIM = 64
FOURIER_WIDTH = 256
GLA_HEADS = 4
GLA_V_DIM = 64
GLA_K_DIM = 32
GLA_WIDTH = 256
GLA_K_COLS = 128
GLA_GATE_RANK = 16
GLA_GATE_TAU = 16.0
GLA_CHUNK = 64
IN_WIDTH = 2592
FFN_HIDDEN = 2816
DEEPNORM_ALPHA = (2 * DEPTH) ** 0.25
LN_EPS = 1e-5

OFF_DQ, OFF_DK, OFF_DV, OFF_FU = 0, 512, 1024, 1536
OFF_GQ, OFF_GV, OFF_GR, OFF_GZ = 1792, 2048, 2304, 2560

LANES = 128
VMEM_LIMIT = 56 * 1024 * 1024

BF16 = jnp.bfloat16
F32 = jnp.float32


def _dot(a, b):
    return jnp.dot(a, b, preferred_element_type=F32)


def _dot_nt(a, b):
    return lax.dot_general(a, b, (((1,), (1,)), ((), ())), preferred_element_type=F32)


def _params(n_grid_dims):
    return pltpu.CompilerParams(
        dimension_semantics=("arbitrary",) * n_grid_dims,
        vmem_limit_bytes=VMEM_LIMIT)


def _rope_slab(t, cos, sin_signed, first_half):
    swapped = jnp.where(first_half, pltpu.roll(t, 96, 1), pltpu.roll(t, 32, 1))
    return t * cos + swapped * sin_signed


ATTN_TQ = 256
ATTN_KC = 256


def _inproj_kernel(x_ref, wqv_ref, w_ref, cos_ref, sin_ref, cost_ref, sint_ref,
                   qt_ref, k_ref, vt_ref, fu_ref, gqk_ref, gv_ref, gr_ref, gz_ref):
    xb = x_ref[...].astype(BF16)
    tm = xb.shape[0]
    cos = cos_ref[...]
    sin_signed = sin_ref[...]
    lane = lax.broadcasted_iota(jnp.int32, cos.shape, 1)
    first_half = (lane % DIFF_QK_DIM) < (DIFF_QK_DIM // 2)
    qk_scale = DIFF_QK_DIM ** -0.5 * math.log2(math.e)
    hqv_t = _dot_nt(wqv_ref[...], xb)
    cos_t = cost_ref[...]
    sin_t = sint_ref[...]
    half = DIFF_QK_DIM // 2
    hk = _dot(xb, w_ref[:, OFF_DK:OFF_DV])
    for hd in range(DIFF_HEADS):
        qt = hqv_t[hd * LANES:(hd + 1) * LANES]
        swapped = jnp.concatenate(
            [qt[half:2 * half], qt[0:half], qt[3 * half:4 * half], qt[2 * half:3 * half]], axis=0)
        qt = ((qt * cos_t + swapped * sin_t) * qk_scale).astype(BF16)
        vt = hqv_t[DIFF_QK_COLS + hd * LANES:DIFF_QK_COLS + (hd + 1) * LANES].astype(BF16)
        for j in range(tm // ATTN_TQ):
            qt_ref[0, hd, j] = qt[:, j * ATTN_TQ:(j + 1) * ATTN_TQ]
        for j in range(tm // ATTN_KC):
            vt_ref[0, hd, j] = vt[:, j * ATTN_KC:(j + 1) * ATTN_KC]
        slab = slice(hd * LANES, (hd + 1) * LANES)
        k_ref[0, hd] = _rope_slab(hk[:, slab], cos, sin_signed, first_half).astype(BF16)
    fu_ref[...] = _dot(xb, w_ref[:, OFF_FU:OFF_GQ]).astype(BF16)
    gqk_ref[...] = _dot(xb, w_ref[:, OFF_GQ:OFF_GV])
    gv_ref[...] = _dot(xb, w_ref[:, OFF_GV:OFF_GR])
    gr_ref[...] = _dot(xb, w_ref[:, OFF_GR:OFF_GZ])
    gz_ref[...] = _dot(xb, w_ref[:, OFF_GZ:IN_WIDTH])


def _in_proj(x2d, wqv_t, w_bf, rope, seq, tm=512):
    m = x2d.shape[0]
    batch = m // seq
    n_seq_tiles = seq // tm
    cos128, sin128, cos_t, sin_t = rope
    row = lambda i: (i, 0)
    const = lambda i: (0, 0)
    tab = lambda i: (i % n_seq_tiles, 0)
    tab_t = lambda i: (0, i % n_seq_tiles)
    tiled = lambda t: (jax.ShapeDtypeStruct((batch, DIFF_HEADS, seq // t, LANES, t), BF16),
                       pl.BlockSpec((1, DIFF_HEADS, tm // t, LANES, t),
                                    lambda i: (i // n_seq_tiles, 0, i % n_seq_tiles, 0, 0)))
    qt_shape, qt_spec = tiled(ATTN_TQ)
    vt_shape, vt_spec = tiled(ATTN_KC)
    k_shape = jax.ShapeDtypeStruct((batch, DIFF_HEADS, seq, LANES), BF16)
    k_spec = pl.BlockSpec((1, DIFF_HEADS, tm, LANES), lambda i: (i // n_seq_tiles, 0, i % n_seq_tiles, 0))
    flat_shapes = (
        jax.ShapeDtypeStruct((m, FOURIER_WIDTH), BF16),
        jax.ShapeDtypeStruct((m, 2 * GLA_K_COLS), F32),
        jax.ShapeDtypeStruct((m, GLA_WIDTH), F32),
        jax.ShapeDtypeStruct((m, GLA_WIDTH), F32),
        jax.ShapeDtypeStruct((m, 2 * GLA_GATE_RANK), F32),
    )
    return pl.pallas_call(
        _inproj_kernel,
        grid=(m // tm,),
        in_specs=[
            pl.BlockSpec((tm, D_MODEL), row),
            pl.BlockSpec((DIFF_QK_COLS + DIFF_WIDTH, D_MODEL), const),
            pl.BlockSpec((D_MODEL, IN_WIDTH), const),
            pl.BlockSpec((tm, LANES), tab),
            pl.BlockSpec((tm, LANES), tab),
            pl.BlockSpec((LANES, tm), tab_t),
            pl.BlockSpec((LANES, tm), tab_t),
        ],
        out_specs=(qt_spec, k_spec, vt_spec) + tuple(pl.BlockSpec((tm, s.shape[1]), row) for s in flat_shapes),
        out_shape=(qt_shape, k_shape, vt_shape) + flat_shapes,
        compiler_params=_params(1),
        name="in_proj",
    )(x2d, wqv_t, w_bf, cos128, sin128, cos_t, sin_t)


ATTN_SUB = 128
ATTN_GROUP = 4


def _attn_kernel(lam_ref, g_ref, qt_ref, k_ref, vt_ref, o_ref,
                 sa_ref, sb_ref, pa_ref, pb_ref, mx_ref, coef_ref, acc_ref, *, lam_init):
    tq, kc, sub = ATTN_TQ, ATTN_KC, ATTN_SUB
    seq = k_ref.shape[2]
    tiles = seq // tq
    n_units = DIFF_HEADS * tiles
    n_chunks = seq // kc
    lp = lam_ref[...]
    lam = (jnp.exp(jnp.sum(lp[0:1] * lp[1:2], axis=1, keepdims=True))
           - jnp.exp(jnp.sum(lp[2:3] * lp[3:4], axis=1, keepdims=True)) + lam_init)
    gain = g_ref[...] * (1.0 - lam_init)
    feature = lax.broadcasted_iota(jnp.int32, (LANES, tq), 0)
    first_map = feature < DIFF_QK_DIM
    s_bufs = (sa_ref, sb_ref)
    p_bufs = (pa_ref, pb_ref)

    def step(k, par, do_a=True, do_b=True, do_c=True):
        s_w, s_r = s_bufs[par], s_bufs[1 - par]
        p_w, p_r = p_bufs[1 - par], p_bufs[par]
        if do_a:
            h_a = k // tiles
            qt = qt_ref[0, h_a, k % tiles]
            zero = jnp.zeros_like(qt)
            qw = (jnp.where(first_map, qt, zero), jnp.where(first_map, zero, qt))
        if do_b:
            m8 = (mx_ref[1 - par, 0], mx_ref[1 - par, 1])
        if do_c:
            h_c = (k - 2) // tiles
            c16 = (coef_ref[par, 0], coef_ref[par, 1])

        def chunk(c, carry):
            mx = [carry[0], carry[1]]
            ls = [carry[2], carry[3]]
            acc = carry[4]
            for part in range(kc // sub):
                rows = pl.ds(pl.multiple_of(c * kc + part * sub, sub), sub)
                if do_a:
                    keys = k_ref[0, h_a, rows, :]
                    for m in range(2):
                        st = _dot(keys, qw[m])
                        s_w[m, rows, :] = st
                        mx[m] = jnp.maximum(mx[m], jnp.max(st.reshape(sub // 8, 8, tq), axis=0))
                if do_b:
                    for m in range(2):
                        p = jnp.exp2(s_r[m, rows, :].reshape(sub // 8, 8, tq) - m8[m][None])
                        ls[m] = ls[m] + jnp.sum(p, axis=0)
                        p_w[m, rows, :] = p.reshape(sub, tq).astype(BF16)
            if do_c:
                rows = pl.ds(pl.multiple_of(c * kc, kc), kc)
                a = (p_r[0, rows, :].reshape(kc // 16, 16, tq) * c16[0][None]
                     - p_r[1, rows, :].reshape(kc // 16, 16, tq) * c16[1][None])
                part_o = _dot(vt_ref[0, h_c, c], a.reshape(kc, tq))
                acc = part_o if acc is None else acc + part_o
            return mx[0], mx[1], ls[0], ls[1], acc

        lowest = jnp.full((8, tq), -jnp.inf, F32)
        nothing = jnp.zeros((8, tq), F32)
        def group(gi, carry):
            stats = carry + (None,)
            for c in range(ATTN_GROUP):
                stats = chunk(gi * ATTN_GROUP + c, stats)
            if do_c:
                acc_ref[...] += stats[4]
            return stats[:4]

        if do_c:
            acc_ref[...] = jnp.zeros_like(acc_ref)
        mx0, mx1, ls0, ls1 = lax.fori_loop(0, n_chunks // ATTN_GROUP, group, (lowest, lowest, nothing, nothing))
        acc = acc_ref[...] if do_c else None
        if do_a:
            mx_ref[par, 0] = jnp.broadcast_to(jnp.max(mx0, axis=0, keepdims=True), (8, tq))
            mx_ref[par, 1] = jnp.broadcast_to(jnp.max(mx1, axis=0, keepdims=True), (8, tq))
        if do_b:
            l0 = jnp.sum(ls0, axis=0, keepdims=True)
            l1 = jnp.sum(ls1, axis=0, keepdims=True)
            coef_ref[1 - par, 0] = jnp.broadcast_to(1.0 / l0, (16, tq)).astype(BF16)
            coef_ref[1 - par, 1] = jnp.broadcast_to(lam / l1, (16, tq)).astype(BF16)
        if do_c:
            o = acc.T
            ms = jnp.mean(o * o, axis=1, keepdims=True)
            q_start = ((k - 2) % tiles) * tq
            q_rows = pl.ds(q_start if isinstance(q_start, int) else pl.multiple_of(q_start, tq), tq)
            o_ref[0, h_c, q_rows, :] = (o * lax.rsqrt(ms + LN_EPS) * gain).astype(BF16)

    def step_pair(j, carry):
        step(2 * j, 0)
        step(2 * j + 1, 1)
        return carry

    step(0, 0, do_b=False, do_c=False)
    step(1, 1, do_c=False)
    lax.fori_loop(1, n_units // 2, step_pair, 0)
    step(n_units, 0, do_a=False)
    step(n_units + 1, 1, do_a=False, do_b=False)


def _diff_attention(qt, k, vt, lam_params, g, lam_init):
    b, _, s, _ = k.shape
    whole = lambda a: pl.BlockSpec((1,) + a.shape[1:], lambda bi: (bi,) + (0,) * (a.ndim - 1))
    return pl.pallas_call(
        functools.partial(_attn_kernel, lam_init=lam_init),
        grid=(b,),
        in_specs=[
            pl.BlockSpec((4, DIFF_QK_DIM), lambda bi: (0, 0)),
            pl.BlockSpec((1, DIFF_V_DIM), lambda bi: (0, 0)),
            whole(qt), whole(k), whole(vt),
        ],
        out_specs=whole(k),
        out_shape=jax.ShapeDtypeStruct((b, DIFF_HEADS, s, LANES), BF16),
        scratch_shapes=[
            pltpu.VMEM((2, s, ATTN_TQ), F32), pltpu.VMEM((2, s, ATTN_TQ), F32),
            pltpu.VMEM((2, s, ATTN_TQ), BF16), pltpu.VMEM((2, s, ATTN_TQ), BF16),
            pltpu.VMEM((2, 2, 8, ATTN_TQ), F32),
            pltpu.VMEM((2, 2, 16, ATTN_TQ), BF16),
            pltpu.VMEM((DIFF_V_DIM, ATTN_TQ), F32),
        ],
        compiler_params=_params(1),
        name="diff_attn",
    )(lam_params, g, qt, k, vt)


def _fourier_kernel(u_ref, cc_ref, sc_ref, f_ref, w_ref, o_ref, ab_ref, *, scale):
    s = u_ref.shape[1]
    bi = pl.program_id(1)

    @pl.when(pl.program_id(0) == 0)
    def _():
        u = u_ref[0]
        ab_ref[bi, 0:s, :] = _dot(u, cc_ref[...]).astype(BF16)
        ab_ref[bi, s:2 * s, :] = _dot(u, sc_ref[...]).astype(BF16)

    z = _dot(f_ref[...], ab_ref[bi]) * scale
    o_ref[0] = _dot(z.astype(BF16), w_ref[...]).astype(BF16)


def _fourier(u, cc_bd, sc_bd, f_mat, w_bd, tk=512):
    b, s, _ = u.shape
    scale = 1.0 / math.sqrt(s * FOURIER_GROUP_DIM)
    const2 = lambda i, bi: (0, 0)
    return pl.pallas_call(
        functools.partial(_fourier_kernel, scale=scale),
        grid=(s // tk, b),
        in_specs=[
            pl.BlockSpec((1, s, FOURIER_WIDTH), lambda i, bi: (jnp.where(i == 0, bi, 0), 0, 0)),
            pl.BlockSpec((FOURIER_WIDTH, FOURIER_WIDTH), const2),
            pl.BlockSpec((FOURIER_WIDTH, FOURIER_WIDTH), const2),
            pl.BlockSpec((tk, 2 * s), lambda i, bi: (i, 0)),
            pl.BlockSpec((FOURIER_WIDTH, FOURIER_WIDTH), const2),
        ],
        out_specs=pl.BlockSpec((1, tk, FOURIER_WIDTH), lambda i, bi: (bi, i, 0)),
        out_shape=jax.ShapeDtypeStruct((b, s, FOURIER_WIDTH), BF16),
        scratch_shapes=[pltpu.VMEM((b, 2 * s, FOURIER_WIDTH), BF16)],
        compiler_params=_params(2),
        name="fourier",
    )(u, cc_bd, sc_bd, f_mat, w_bd)


def _split3(g):
    hi = g.astype(BF16)
    r1 = g - hi.astype(F32)
    mid = r1.astype(BF16)
    lo = (r1 - mid.astype(F32)).astype(BF16)
    return hi, mid, lo


def _gla_kernel(qk_ref, v_ref, r_ref, z_ref, w2_ref, b2_ref, g_ref, o_ref,
                la_ref, vt_ref, acc_ref, qt_ref, kt_ref, a_ref, kv_ref, dec_ref, st_ref):
    s = v_ref.shape[1]
    c = GLA_CHUNK
    pair = 2 * c
    n_pairs = s // pair
    n_chunks = s // c
    kc = GLA_K_COLS

    logit = _dot(z_ref[0].astype(BF16), w2_ref[...]) + b2_ref[...]
    la_ref[...] = (jnp.minimum(logit, 0.0) - jnp.log1p(jnp.exp(-jnp.abs(logit)))) * (1.0 / GLA_GATE_TAU)
    vt_ref[...] = v_ref[0].T.astype(BF16)

    row_i = lax.broadcasted_iota(jnp.int32, (pair, pair), 0)
    col_i = lax.broadcasted_iota(jnp.int32, (pair, pair), 1)
    same_chunk = (row_i // c) == (col_i // c)
    cum_mats = ((same_chunk & (col_i <= row_i)).astype(BF16), (same_chunk & (col_i >= row_i)).astype(BF16))
    r_hk = lax.broadcasted_iota(jnp.int32, (GLA_HEADS * c, kc), 0) // c
    l_hk = lax.broadcasted_iota(jnp.int32, (GLA_HEADS * c, kc), 1) // GLA_K_DIM
    blk_k = r_hk == l_hk
    r_hv = lax.broadcasted_iota(jnp.int32, (GLA_HEADS * c, GLA_WIDTH), 0) // c
    l_hv = lax.broadcasted_iota(jnp.int32, (GLA_HEADS * c, GLA_WIDTH), 1) // GLA_V_DIM
    blk_v = r_hv == l_hv
    qi = lax.broadcasted_iota(jnp.int32, (c, GLA_HEADS * c), 0)
    kj = lax.broadcasted_iota(jnp.int32, (c, GLA_HEADS * c), 1) % c
    q_scale = GLA_K_DIM ** -0.5

    causal = (kj <= qi, kj >= qi)

    def pair_rows(p):
        start = p * pair
        return pl.ds(start if isinstance(start, int) else pl.multiple_of(start, pair), pair)

    def decayed_qk(p):
        rows = pair_rows(p)
        q2 = qk_ref[0, rows, 0:kc] * q_scale
        k2 = qk_ref[0, rows, kc:2 * kc]
        for d in range(2):
            hi, mid, lo = _split3(la_ref[rows, d * kc:(d + 1) * kc])
            c3 = _dot(cum_mats[d], jnp.concatenate([hi, mid, lo], axis=1))
            bcum = c3[:, 0:kc] + c3[:, kc:2 * kc] + c3[:, 2 * kc:3 * kc]
            eb = jnp.exp(bcum)
            qt_ref[d, rows, :] = (q2 * eb).astype(BF16)
            kt_ref[d, rows, :] = k2 * jnp.exp(-bcum)
            for ci in range(2):
                edge = ci * c if d == 1 else (ci + 1) * c - 1
                dec_ref[d, 2 * p + ci] = jnp.broadcast_to(eb[edge:edge + 1], (8, kc))

    def intra_scores(p):
        for ci in range(2):
            start = p * pair + ci * c
            rows = pl.ds(start if isinstance(start, int) else pl.multiple_of(start, c), c)
            for d in range(2):
                kt_bd = jnp.where(blk_k, jnp.concatenate([kt_ref[d, rows, :]] * GLA_HEADS, axis=0), 0.0)
                a = _dot_nt(qt_ref[d, rows, :], kt_bd.astype(BF16))
                a_ref[2 * p + ci, d * c:(d + 1) * c, :] = jnp.where(causal[d], a, 0.0).astype(BF16)

    def intra_out_and_kv(p):
        rows2 = pair_rows(p)
        v2 = v_ref[0, rows2, :]
        vt2 = vt_ref[:, rows2]
        kt2 = jnp.concatenate([kt_ref[0, rows2, :], kt_ref[1, rows2, :]], axis=1)
        for ci in range(2):
            sl = slice(ci * c, (ci + 1) * c)
            start = p * pair + ci * c
            rows = pl.ds(start if isinstance(start, int) else pl.multiple_of(start, c), c)
            v_bd = jnp.where(blk_v, jnp.concatenate([v2[sl]] * GLA_HEADS, axis=0), 0.0).astype(BF16)
            o_both = _dot(a_ref[2 * p + ci], v_bd)
            acc_ref[rows, :] = o_both[0:c] + o_both[c:2 * c]
            gap = jnp.zeros((c, 2 * kc), F32)
            kt_pair = jnp.concatenate([kt2[sl], gap] if ci == 0 else [gap, kt2[sl]], axis=0).astype(BF16)
            kv = _dot(vt2, kt_pair)
            kv_ref[0, 2 * p + ci] = jnp.where(blk_k, kv[:, 0:kc], 0.0)
            kv_ref[1, 2 * p + ci] = jnp.where(blk_k, kv[:, kc:2 * kc], 0.0)

    decayed_qk(0)
    intra_scores(0)
    decayed_qk(1)

    def intra_step(p, carry):
        intra_out_and_kv(p)
        intra_scores(p + 1)
        decayed_qk(p + 2)
        return carry

    lax.fori_loop(0, n_pairs - 2, intra_step, 0)
    intra_out_and_kv(n_pairs - 2)
    intra_scores(n_pairs - 1)
    intra_out_and_kv(n_pairs - 1)

    st_ref[...] = jnp.zeros_like(st_ref)

    def scan_step(i, carry):
        for d in range(2):
            n = i if d == 0 else n_chunks - 1 - i
            rows = pl.ds(pl.multiple_of(n * c, c), c)
            st = st_ref[d]
            acc_ref[rows, :] += _dot_nt(qt_ref[d, rows, :], st.astype(BF16))
            st = (st + kv_ref[d, n]).reshape(GLA_WIDTH // 8, 8, kc) * dec_ref[d, n][None]
            st_ref[d] = st.reshape(GLA_WIDTH, kc)
        return carry

    lax.fori_loop(0, n_chunks, scan_step, 0, unroll=2)

    o = acc_ref[...]
    o2 = o * o
    o2_hi = o2.astype(BF16)
    o2_lo = (o2 - o2_hi.astype(F32)).astype(BF16)
    gi = lax.broadcasted_iota(jnp.int32, (GLA_WIDTH, GLA_WIDTH), 0) // GLA_V_DIM
    gj = lax.broadcasted_iota(jnp.int32, (GLA_WIDTH, GLA_WIDTH), 1) // GLA_V_DIM
    ones_bd = (gi == gj).astype(BF16)
    ms = (_dot(o2_hi, ones_bd) + _dot(o2_lo, ones_bd)) * (1.0 / GLA_V_DIM)
    r = r_ref[0]
    gate = r * (1.0 / (1.0 + jnp.exp(-r)))
    o_ref[0] = (o * lax.rsqrt(ms + LN_EPS) * g_ref[...] * gate).astype(BF16)


def _gla(gqk, gv, gr, gz, w2_bd, b2_cat, g_tiled):
    b, s, _ = gv.shape
    per_b = lambda bi: (bi, 0, 0)
    const = lambda bi: (0, 0)
    return pl.pallas_call(
        _gla_kernel,
        grid=(b,),
        in_specs=[
            pl.BlockSpec((1, s, 2 * GLA_K_COLS), per_b),
            pl.BlockSpec((1, s, GLA_WIDTH), per_b),
            pl.BlockSpec((1, s, GLA_WIDTH), per_b),
            pl.BlockSpec((1, s, 2 * GLA_GATE_RANK), per_b),
            pl.BlockSpec((2 * GLA_GATE_RANK, 2 * GLA_K_COLS), const),
            pl.BlockSpec((1, 2 * GLA_K_COLS), const),
            pl.BlockSpec((1, GLA_WIDTH), const),
        ],
        out_specs=pl.BlockSpec((1, s, GLA_WIDTH), per_b),
        out_shape=jax.ShapeDtypeStruct((b, s, GLA_WIDTH), BF16),
        scratch_shapes=[
            pltpu.VMEM((s, 2 * GLA_K_COLS), F32),
            pltpu.VMEM((GLA_WIDTH, s), BF16),
            pltpu.VMEM((s, GLA_WIDTH), F32),
            pltpu.VMEM((2, s, GLA_K_COLS), BF16),
            pltpu.VMEM((2, s, GLA_K_COLS), F32),
            pltpu.VMEM((s // GLA_CHUNK, 2 * GLA_CHUNK, GLA_WIDTH), BF16),
            pltpu.VMEM((2, s // GLA_CHUNK, GLA_WIDTH, GLA_K_COLS), F32),
            pltpu.VMEM((2, s // GLA_CHUNK, 8, GLA_K_COLS), F32),
            pltpu.VMEM((2, GLA_WIDTH, GLA_K_COLS), F32),
        ],
        compiler_params=_params(1),
        name="gla",
    )(gqk, gv, gr, gz, w2_bd, b2_cat, g_tiled)


def _layer_norm(y, g, b):
    mu = jnp.mean(y, axis=1, keepdims=True)
    d = y - mu
    var = jnp.mean(d * d, axis=1, keepdims=True)
    return d * lax.rsqrt(var + LN_EPS) * g + b


ROW_SUB = 512


def _outproj_kernel(od_ref, of_ref, og_ref, x_ref, w_ref, g_ref, b_ref, o_ref):
    for r in range(x_ref.shape[0] // ROW_SUB):
        rows = slice(r * ROW_SUB, (r + 1) * ROW_SUB)
        mixed = jnp.concatenate([od_ref[0, hd, rows, :] for hd in range(DIFF_HEADS)]
                                + [of_ref[rows, :], og_ref[rows, :]], axis=1)
        m = _dot(mixed, w_ref[...])
        o_ref[rows, :] = _layer_norm(DEEPNORM_ALPHA * x_ref[rows, :] + m, g_ref[...], b_ref[...])


def _out_proj(o_diff, o_four, o_gla, x2d, w_bf, g, b, tm=1024):
    m = x2d.shape[0]
    seq = o_diff.shape[2]
    n_seq_tiles = seq // tm
    row = lambda i: (i, 0)
    const = lambda i: (0, 0)
    return pl.pallas_call(
        _outproj_kernel,
        grid=(m // tm,),
        in_specs=[
            pl.BlockSpec((1, DIFF_HEADS, tm, LANES), lambda i: (i // n_seq_tiles, 0, i % n_seq_tiles, 0)),
            pl.BlockSpec((tm, FOURIER_WIDTH), row),
            pl.BlockSpec((tm, GLA_WIDTH), row),
            pl.BlockSpec((tm, D_MODEL), row),
            pl.BlockSpec((D_MODEL, D_MODEL), const),
            pl.BlockSpec((1, D_MODEL), const),
            pl.BlockSpec((1, D_MODEL), const),
        ],
        out_specs=pl.BlockSpec((tm, D_MODEL), row),
        out_shape=jax.ShapeDtypeStruct((m, D_MODEL), F32),
        compiler_params=_params(1),
        name="out_proj",
    )(o_diff, o_four, o_gla, x2d, w_bf, g, b)


FFN_CHUNK = 256


def _ffn_kernel(x_ref, wg_ref, wu_ref, wd_ref, g_ref, b_ref, o_ref, acc_ref):
    for r in range(x_ref.shape[0] // ROW_SUB):
        rows = slice(r * ROW_SUB, (r + 1) * ROW_SUB)
        x = x_ref[rows, :]
        xb = x.astype(BF16)
        for ci in range(FFN_HIDDEN // FFN_CHUNK):
            cols = slice(ci * FFN_CHUNK, (ci + 1) * FFN_CHUNK)
            hg = _dot(xb, wg_ref[:, cols])
            hu = _dot(xb, wu_ref[:, cols])
            act = (hg * (1.0 / (1.0 + jnp.exp(-hg))) * hu).astype(BF16)
            part = _dot(act, wd_ref[cols, :])
            if ci == 0:
                acc_ref[rows, :] = part
            else:
                acc_ref[rows, :] += part
        o_ref[rows, :] = _layer_norm(DEEPNORM_ALPHA * x + acc_ref[rows, :], g_ref[...], b_ref[...])


def _ffn(x2d, wg_bf, wu_bf, wd_bf, g, b, tm=1024):
    m = x2d.shape[0]
    row = lambda i: (i, 0)
    const = lambda i: (0, 0)
    resident = lambda shape: pl.BlockSpec(shape, const, pipeline_mode=pl.Buffered(1))
    return pl.pallas_call(
        _ffn_kernel,
        grid=(m // tm,),
        in_specs=[
            pl.BlockSpec((tm, D_MODEL), row),
            resident((D_MODEL, FFN_HIDDEN)),
            resident((D_MODEL, FFN_HIDDEN)),
            resident((FFN_HIDDEN, D_MODEL)),
            pl.BlockSpec((1, D_MODEL), const),
            pl.BlockSpec((1, D_MODEL), const),
        ],
        out_specs=pl.BlockSpec((tm, D_MODEL), row),
        out_shape=jax.ShapeDtypeStruct((m, D_MODEL), F32),
        scratch_shapes=[pltpu.VMEM((tm, D_MODEL), F32)],
        compiler_params=_params(1),
        name="ffn",
    )(x2d, wg_bf, wu_bf, wd_bf, g, b)


def _rope_tables(seq):
    half = DIFF_QK_DIM // 2
    pos = jnp.arange(seq, dtype=F32)
    inv_freq = ROPE_THETA ** (-jnp.arange(0, DIFF_QK_DIM, 2, dtype=F32) / DIFF_QK_DIM)
    ang = pos[:, None] * inv_freq[None, :]
    cos, sin = jnp.cos(ang), jnp.sin(ang)
    reps = LANES // DIFF_QK_DIM
    cos128 = jnp.tile(jnp.concatenate([cos, cos], axis=1), (1, reps))
    sin128 = jnp.tile(jnp.concatenate([-sin, sin], axis=1), (1, reps))
    assert cos128.shape == (seq, LANES) and half * 2 == DIFF_QK_DIM
    return cos128, sin128, cos128.T, sin128.T


def _dft_tables(seq):
    k = np.arange(seq, dtype=np.int64)
    ang = 2.0 * np.pi * ((k[:, None] * k[None, :]) % seq).astype(np.float64) / seq
    f_mat = np.concatenate([np.cos(ang), -np.sin(ang)], axis=1)
    c = np.arange(FOURIER_GROUP_DIM, dtype=np.int64)
    ang_c = 2.0 * np.pi * ((c[:, None] * c[None, :]) % FOURIER_GROUP_DIM).astype(np.float64) / FOURIER_GROUP_DIM
    eye = np.eye(FOURIER_GROUPS)
    cc_bd = np.kron(eye, np.cos(ang_c))
    sc_bd = np.kron(eye, np.sin(ang_c))
    return tuple(jnp.asarray(t, dtype=F32).astype(BF16) for t in (f_mat, cc_bd, sc_bd))


def _block_diag(blocks):
    n = len(blocks)
    rows = []
    for i, blk in enumerate(blocks):
        rows.append(jnp.concatenate(
            [blk if j == i else jnp.zeros((blk.shape[0], blocks[j].shape[1]), blk.dtype) for j in range(n)],
            axis=1))
    return jnp.concatenate(rows, axis=0)


def kernel(x, w_in, diff_lambda, diff_norm_g, fourier_w, gla_gate_w2, gla_gate_b2, gla_norm_g, w_out,
           ln1_g, ln1_b, ffn_w_gate, ffn_w_up, ffn_w_down, ln2_g, ln2_b):
    b, s, d = x.shape
    m = b * s
    rope = _rope_tables(s)
    f_mat, cc_bd, sc_bd = _dft_tables(s)
    x2d = x.reshape(m, d)
    for l in range(DEPTH):
        lam_init = 0.8 - 0.6 * math.exp(-0.3 * l)
        w_bf = w_in[l].astype(BF16)
        wqv_t = jnp.concatenate([w_bf[:, OFF_DQ:OFF_DK], w_bf[:, OFF_DV:OFF_FU]], axis=1).T
        qt, k, vt, fu, gqk, gv, gr, gz = _in_proj(x2d, wqv_t, w_bf, rope, s)
        o_diff = _diff_attention(qt, k, vt, diff_lambda[l], diff_norm_g[l].reshape(1, -1), lam_init)
        w_four = _block_diag([fourier_w[l, g] for g in range(FOURIER_GROUPS)]).astype(BF16)
        o_four = _fourier(fu.reshape(b, s, -1), cc_bd, sc_bd, f_mat, w_four)
        w2_bd = _block_diag([gla_gate_w2[l, 0], gla_gate_w2[l, 1]]).astype(BF16)
        b2_cat = gla_gate_b2[l].reshape(1, -1)
        g_gla = jnp.tile(gla_norm_g[l], GLA_HEADS).reshape(1, -1)
        o_gla = _gla(gqk.reshape(b, s, -1), gv.reshape(b, s, -1), gr.reshape(b, s, -1),
                     gz.reshape(b, s, -1), w2_bd, b2_cat, g_gla)
        x2d = _out_proj(o_diff, o_four.reshape(m, -1), o_gla.reshape(m, -1), x2d,
                        w_out[l].astype(BF16), ln1_g[l].reshape(1, -1), ln1_b[l].reshape(1, -1))
        x2d = _ffn(x2d, ffn_w_gate[l].astype(BF16), ffn_w_up[l].astype(BF16), ffn_w_down[l].astype(BF16),
                   ln2_g[l].reshape(1, -1), ln2_b[l].reshape(1, -1))
    return x2d.reshape(b, s, d)
```

```python
import functools
import math

import jax
import jax.numpy as jnp
import numpy as np
from jax import lax
from jax.experimental import pallas as pl
from jax.experimental.pallas import tpu as pltpu

D_MODEL = 1024
DEPTH = 2
DIFF_HEADS = 4
DIFF_QK_DIM = 64
DIFF_V_DIM = 128
DIFF_WIDTH = 512
DIFF_QK_COLS = 512
ROPE_THETA = 10000.0
FOURIER_GROUPS = 4
FOURIER_GROUP_DIM = 64
FOURIER_WIDTH = 256
GLA_HEADS = 4
GLA_V_DIM = 64
GLA_K_DIM = 32
GLA_WIDTH = 256
GLA_K_COLS = 128
GLA_GATE_RANK = 16
GLA_GATE_TAU = 16.0
GLA_CHUNK = 64
IN_WIDTH = 2592
FFN_HIDDEN = 2816
DEEPNORM_ALPHA = (2 * DEPTH) ** 0.25
LN_EPS = 1e-5

OFF_DQ, OFF_DK, OFF_DV, OFF_FU = 0, 512, 1024, 1536
OFF_GQ, OFF_GV, OFF_GR, OFF_GZ = 1792, 2048, 2304, 2560

LANES = 128
VMEM_LIMIT = 56 * 1024 * 1024

BF16 = jnp.bfloat16
F32 = jnp.float32


def _dot(a, b):
    return jnp.dot(a, b, preferred_element_type=F32)


def _dot_nt(a, b):
    return lax.dot_general(a, b, (((1,), (1,)), ((), ())), preferred_element_type=F32)


def _params(n_grid_dims):
    return pltpu.CompilerParams(
        dimension_semantics=("arbitrary",) * n_grid_dims,
        vmem_limit_bytes=VMEM_LIMIT)


def _rope_slab(t, cos, sin_signed, first_half):
    swapped = jnp.where(first_half, pltpu.roll(t, 96, 1), pltpu.roll(t, 32, 1))
    return t * cos + swapped * sin_signed


ATTN_TQ = 256
ATTN_KC = 256


def _inproj_kernel(x_ref, wqv_ref, w_ref, cos_ref, sin_ref, cost_ref, sint_ref,
                   qt_ref, k_ref, vt_ref, fu_ref, gqk_ref, gv_ref, gr_ref, gz_ref):
    xb = x_ref[...].astype(BF16)
    tm = xb.shape[0]
    cos = cos_ref[...]
    sin_signed = sin_ref[...]
    lane = lax.broadcasted_iota(jnp.int32, cos.shape, 1)
    first_half = (lane % DIFF_QK_DIM) < (DIFF_QK_DIM // 2)
    qk_scale = DIFF_QK_DIM ** -0.5 * math.log2(math.e)
    hqv_t = _dot_nt(wqv_ref[...], xb)
    cos_t = cost_ref[...]
    sin_t = sint_ref[...]
    half = DIFF_QK_DIM // 2
    wcols = lambda lo, hi: w_ref[:, lo:hi].astype(BF16)
    hk = _dot(xb, wcols(OFF_DK, OFF_DV))
    for hd in range(DIFF_HEADS):
        qt = hqv_t[hd * LANES:(hd + 1) * LANES]
        swapped = jnp.concatenate(
            [qt[half:2 * half], qt[0:half], qt[3 * half:4 * half], qt[2 * half:3 * half]], axis=0)
        qt = ((qt * cos_t + swapped * sin_t) * qk_scale).astype(BF16)
        vt = hqv_t[DIFF_QK_COLS + hd * LANES:DIFF_QK_COLS + (hd + 1) * LANES].astype(BF16)
        for j in range(tm // ATTN_TQ):
            qt_ref[0, hd, j] = qt[:, j * ATTN_TQ:(j + 1) * ATTN_TQ]
        for j in range(tm // ATTN_KC):
            vt_ref[0, hd, j] = vt[:, j * ATTN_KC:(j + 1) * ATTN_KC]
        slab = slice(hd * LANES, (hd + 1) * LANES)
        k_ref[0, hd] = _rope_slab(hk[:, slab], cos, sin_signed, first_half).astype(BF16)
    fu_ref[...] = _dot(xb, wcols(OFF_FU, OFF_GQ)).astype(BF16)
    gqk_ref[...] = _dot(xb, wcols(OFF_GQ, OFF_GV))
    gv_ref[...] = _dot(xb, wcols(OFF_GV, OFF_GR))
    gr_ref[...] = _dot(xb, wcols(OFF_GR, OFF_GZ))
    gz_ref[...] = _dot(xb, wcols(OFF_GZ, IN_WIDTH))


def _layer_weight(w_all, layer):
    return pl.BlockSpec((None,) + w_all.shape[1:], lambda i: (layer,) + (0,) * (w_all.ndim - 1),
                        pipeline_mode=pl.Buffered(1))


def _in_proj(x2d, wqv_t, w_all, layer, rope, seq, tm=512):
    m = x2d.shape[0]
    batch = m // seq
    n_seq_tiles = seq // tm
    cos128, sin128, cos_t, sin_t = rope
    row = lambda i: (i, 0)
    const = lambda i: (0, 0)
    tab = lambda i: (i % n_seq_tiles, 0)
    tab_t = lambda i: (0, i % n_seq_tiles)
    tiled = lambda t: (jax.ShapeDtypeStruct((batch, DIFF_HEADS, seq // t, LANES, t), BF16),
                       pl.BlockSpec((1, DIFF_HEADS, tm // t, LANES, t),
                                    lambda i: (i // n_seq_tiles, 0, i % n_seq_tiles, 0, 0)))
    qt_shape, qt_spec = tiled(ATTN_TQ)
    vt_shape, vt_spec = tiled(ATTN_KC)
    k_shape = jax.ShapeDtypeStruct((batch, DIFF_HEADS, seq, LANES), BF16)
    k_spec = pl.BlockSpec((1, DIFF_HEADS, tm, LANES), lambda i: (i // n_seq_tiles, 0, i % n_seq_tiles, 0))
    flat_shapes = (
        jax.ShapeDtypeStruct((m, FOURIER_WIDTH), BF16),
        jax.ShapeDtypeStruct((m, 2 * GLA_K_COLS), F32),
        jax.ShapeDtypeStruct((m, GLA_WIDTH), F32),
        jax.ShapeDtypeStruct((m, GLA_WIDTH), F32),
        jax.ShapeDtypeStruct((m, 2 * GLA_GATE_RANK), F32),
    )
    return pl.pallas_call(
        _inproj_kernel,
        grid=(m // tm,),
        in_specs=[
            pl.BlockSpec((tm, D_MODEL), row),
            pl.BlockSpec((DIFF_QK_COLS + DIFF_WIDTH, D_MODEL), const),
            _layer_weight(w_all, layer),
            pl.BlockSpec((tm, LANES), tab),
            pl.BlockSpec((tm, LANES), tab),
            pl.BlockSpec((LANES, tm), tab_t),
            pl.BlockSpec((LANES, tm), tab_t),
        ],
        out_specs=(qt_spec, k_spec, vt_spec) + tuple(pl.BlockSpec((tm, s.shape[1]), row) for s in flat_shapes),
        out_shape=(qt_shape, k_shape, vt_shape) + flat_shapes,
        compiler_params=_params(1),
        name="in_proj",
    )(x2d, wqv_t, w_all, cos128, sin128, cos_t, sin_t)


ATTN_SUB = 128
ATTN_GROUP = 4


def _attn_kernel(lam_ref, g_ref, qt_ref, k_ref, vt_ref, o_ref,
                 sa_ref, sb_ref, pa_ref, pb_ref, mx_ref, coef_ref, acc_ref, *, lam_init):
    tq, kc, sub = ATTN_TQ, ATTN_KC, ATTN_SUB
    seq = k_ref.shape[2]
    tiles = seq // tq
    n_units = DIFF_HEADS * tiles
    n_chunks = seq // kc
    lp = lam_ref[...]
    lam = (jnp.exp(jnp.sum(lp[0:1] * lp[1:2], axis=1, keepdims=True))
           - jnp.exp(jnp.sum(lp[2:3] * lp[3:4], axis=1, keepdims=True)) + lam_init)
    gain = g_ref[...] * (1.0 - lam_init)
    feature = lax.broadcasted_iota(jnp.int32, (LANES, tq), 0)
    first_map = feature < DIFF_QK_DIM
    s_bufs = (sa_ref, sb_ref)
    p_bufs = (pa_ref, pb_ref)

    def step(k, par, do_a=True, do_b=True, do_c=True):
        s_w, s_r = s_bufs[par], s_bufs[1 - par]
        p_w, p_r = p_bufs[1 - par], p_bufs[par]
        if do_a:
            h_a = k // tiles
            qt = qt_ref[0, h_a, k % tiles]
            zero = jnp.zeros_like(qt)
            qw = (jnp.where(first_map, qt, zero), jnp.where(first_map, zero, qt))
        if do_b:
            m8 = (mx_ref[1 - par, 0], mx_ref[1 - par, 1])
        if do_c:
            h_c = (k - 2) // tiles
            c16 = (coef_ref[par, 0], coef_ref[par, 1])

        def chunk(c, carry):
            mx = [carry[0], carry[1]]
            ls = [carry[2], carry[3]]
            acc = carry[4]
            for part in range(kc // sub):
                rows = pl.ds(pl.multiple_of(c * kc + part * sub, sub), sub)
                if do_a:
                    keys = k_ref[0, h_a, rows, :]
                    for m in range(2):
                        st = _dot(keys, qw[m])
                        s_w[m, rows, :] = st
                        mx[m] = jnp.maximum(mx[m], jnp.max(st.reshape(sub // 8, 8, tq), axis=0))
                if do_b:
                    for m in range(2):
                        p = jnp.exp2(s_r[m, rows, :].reshape(sub // 8, 8, tq) - m8[m][None])
                        ls[m] = ls[m] + jnp.sum(p, axis=0)
                        p_w[m, rows, :] = p.reshape(sub, tq).astype(BF16)
            if do_c:
                rows = pl.ds(pl.multiple_of(c * kc, kc), kc)
                a = (p_r[0, rows, :].reshape(kc // 16, 16, tq) * c16[0][None]
                     - p_r[1, rows, :].reshape(kc // 16, 16, tq) * c16[1][None])
                part_o = _dot(vt_ref[0, h_c, c], a.reshape(kc, tq))
                acc = part_o if acc is None else acc + part_o
            return mx[0], mx[1], ls[0], ls[1], acc

        lowest = jnp.full((8, tq), -jnp.inf, F32)
        nothing = jnp.zeros((8, tq), F32)
        def group(gi, carry):
            stats = carry + (None,)
            for c in range(ATTN_GROUP):
                stats = chunk(gi * ATTN_GROUP + c, stats)
            if do_c:
                acc_ref[...] += stats[4]
            return stats[:4]

        if do_c:
            acc_ref[...] = jnp.zeros_like(acc_ref)
        mx0, mx1, ls0, ls1 = lax.fori_loop(0, n_chunks // ATTN_GROUP, group, (lowest, lowest, nothing, nothing))
        acc = acc_ref[...] if do_c else None
        if do_a:
            mx_ref[par, 0] = jnp.broadcast_to(jnp.max(mx0, axis=0, keepdims=True), (8, tq))
            mx_ref[par, 1] = jnp.broadcast_to(jnp.max(mx1, axis=0, keepdims=True), (8, tq))
        if do_b:
            l0 = jnp.sum(ls0, axis=0, keepdims=True)
            l1 = jnp.sum(ls1, axis=0, keepdims=True)
            coef_ref[1 - par, 0] = jnp.broadcast_to(1.0 / l0, (16, tq)).astype(BF16)
            coef_ref[1 - par, 1] = jnp.broadcast_to(lam / l1, (16, tq)).astype(BF16)
        if do_c:
            o = acc.T
            ms = jnp.mean(o * o, axis=1, keepdims=True)
            q_start = ((k - 2) % tiles) * tq
            q_rows = pl.ds(q_start if isinstance(q_start, int) else pl.multiple_of(q_start, tq), tq)
            o_ref[0, h_c, q_rows, :] = (o * lax.rsqrt(ms + LN_EPS) * gain).astype(BF16)

    def step_pair(j, carry):
        step(2 * j, 0)
        step(2 * j + 1, 1)
        return carry

    step(0, 0, do_b=False, do_c=False)
    step(1, 1, do_c=False)
    lax.fori_loop(1, n_units // 2, step_pair, 0)
    step(n_units, 0, do_a=False)
    step(n_units + 1, 1, do_a=False, do_b=False)


def _diff_attention(qt, k, vt, lam_params, g, lam_init):
    b, _, s, _ = k.shape
    whole = lambda a: pl.BlockSpec((1,) + a.shape[1:], lambda bi: (bi,) + (0,) * (a.ndim - 1))
    return pl.pallas_call(
        functools.partial(_attn_kernel, lam_init=lam_init),
        grid=(b,),
        in_specs=[
            pl.BlockSpec((4, DIFF_QK_DIM), lambda bi: (0, 0)),
            pl.BlockSpec((1, DIFF_V_DIM), lambda bi: (0, 0)),
            whole(qt), whole(k), whole(vt),
        ],
        out_specs=whole(k),
        out_shape=jax.ShapeDtypeStruct((b, DIFF_HEADS, s, LANES), BF16),
        scratch_shapes=[
            pltpu.VMEM((2, s, ATTN_TQ), F32), pltpu.VMEM((2, s, ATTN_TQ), F32),
            pltpu.VMEM((2, s, ATTN_TQ), BF16), pltpu.VMEM((2, s, ATTN_TQ), BF16),
            pltpu.VMEM((2, 2, 8, ATTN_TQ), F32),
            pltpu.VMEM((2, 2, 16, ATTN_TQ), BF16),
            pltpu.VMEM((DIFF_V_DIM, ATTN_TQ), F32),
        ],
        compiler_params=_params(1),
        name="diff_attn",
    )(lam_params, g, qt, k, vt)


FOURIER_PACK = 2


def _fourier_kernel(u_ref, cc_ref, sc_ref, fe_ref, fo_ref, w_ref, o_ref, ab_ref, *, scale):
    s = u_ref.shape[1]
    half = s // 2
    w = FOURIER_WIDTH
    for e in range(FOURIER_PACK):
        u = u_ref[e]
        a = _dot(u, cc_ref[...])
        b = _dot(u, sc_ref[...])
        lanes = slice(e * w, (e + 1) * w)
        ab_ref[0, 0:half, lanes] = (a[0:half] + a[half:s]).astype(BF16)
        ab_ref[0, half:s, lanes] = (b[0:half] + b[half:s]).astype(BF16)
        ab_ref[1, 0:half, lanes] = (a[0:half] - a[half:s]).astype(BF16)
        ab_ref[1, half:s, lanes] = (b[0:half] - b[half:s]).astype(BF16)
    for parity, f_ref in enumerate((fe_ref, fo_ref)):
        z = _dot(f_ref[...], ab_ref[parity]) * scale
        for e in range(FOURIER_PACK):
            y = _dot(z[:, e * w:(e + 1) * w].astype(BF16), w_ref[...])
            o_ref[e, :, parity * w:(parity + 1) * w] = y.astype(BF16)


def _fourier(u, cc_bd, sc_bd, f_even, f_odd, w_bd):
    b, s, _ = u.shape
    pack = FOURIER_PACK
    scale = 1.0 / math.sqrt(s * FOURIER_GROUP_DIM)
    const2 = lambda bi: (0, 0)
    out = pl.pallas_call(
        functools.partial(_fourier_kernel, scale=scale),
        grid=(b // pack,),
        in_specs=[
            pl.BlockSpec((pack, s, FOURIER_WIDTH), lambda bi: (bi, 0, 0)),
            pl.BlockSpec((FOURIER_WIDTH, FOURIER_WIDTH), const2),
            pl.BlockSpec((FOURIER_WIDTH, FOURIER_WIDTH), const2),
            pl.BlockSpec((s // 2, s), const2),
            pl.BlockSpec((s // 2, s), const2),
            pl.BlockSpec((FOURIER_WIDTH, FOURIER_WIDTH), const2),
        ],
        out_specs=pl.BlockSpec((pack, s // 2, 2 * FOURIER_WIDTH), lambda bi: (bi, 0, 0)),
        out_shape=jax.ShapeDtypeStruct((b, s // 2, 2 * FOURIER_WIDTH), BF16),
        scratch_shapes=[pltpu.VMEM((2, s, pack * FOURIER_WIDTH), BF16)],
        compiler_params=_params(1),
        name="fourier",
    )(u, cc_bd, sc_bd, f_even, f_odd, w_bd)
    return out.reshape(b, s, FOURIER_WIDTH)


def _split3(g):
    hi = g.astype(BF16)
    r1 = g - hi.astype(F32)
    mid = r1.astype(BF16)
    lo = (r1 - mid.astype(F32)).astype(BF16)
    return hi, mid, lo


def _gla_kernel(qk_ref, v_ref, r_ref, z_ref, w2_ref, b2_ref, g_ref, o_ref,
                la_ref, vt_ref, acc_ref, qt_ref, kt_ref, a_ref, kv_ref, dec_ref, st_ref):
    s = v_ref.shape[1]
    c = GLA_CHUNK
    pair = 2 * c
    n_pairs = s // pair
    n_chunks = s // c
    kc = GLA_K_COLS

    logit = _dot(z_ref[0].astype(BF16), w2_ref[...]) + b2_ref[...]
    la_ref[...] = (jnp.minimum(logit, 0.0) - jnp.log1p(jnp.exp(-jnp.abs(logit)))) * (1.0 / GLA_GATE_TAU)
    vt_ref[...] = v_ref[0].T.astype(BF16)

    row_i = lax.broadcasted_iota(jnp.int32, (pair, pair), 0)
    col_i = lax.broadcasted_iota(jnp.int32, (pair, pair), 1)
    same_chunk = (row_i // c) == (col_i // c)
    cum_mats = ((same_chunk & (col_i <= row_i)).astype(BF16), (same_chunk & (col_i >= row_i)).astype(BF16))
    r_hk = lax.broadcasted_iota(jnp.int32, (GLA_HEADS * c, kc), 0) // c
    l_hk = lax.broadcasted_iota(jnp.int32, (GLA_HEADS * c, kc), 1) // GLA_K_DIM
    blk_k = r_hk == l_hk
    r_hv = lax.broadcasted_iota(jnp.int32, (GLA_HEADS * c, GLA_WIDTH), 0) // c
    l_hv = lax.broadcasted_iota(jnp.int32, (GLA_HEADS * c, GLA_WIDTH), 1) // GLA_V_DIM
    blk_v = r_hv == l_hv
    qi = lax.broadcasted_iota(jnp.int32, (c, GLA_HEADS * c), 0)
    kj = lax.broadcasted_iota(jnp.int32, (c, GLA_HEADS * c), 1) % c
    q_scale = GLA_K_DIM ** -0.5

    causal = (kj <= qi, kj >= qi)

    def pair_rows(p):
        start = p * pair
        return pl.ds(start if isinstance(start, int) else pl.multiple_of(start, pair), pair)

    def decayed_qk(p):
        rows = pair_rows(p)
        q2 = qk_ref[0, rows, 0:kc] * q_scale
        k2 = qk_ref[0, rows, kc:2 * kc]
        for d in range(2):
            hi, mid, lo = _split3(la_ref[rows, d * kc:(d + 1) * kc])
            c3 = _dot(cum_mats[d], jnp.concatenate([hi, mid, lo], axis=1))
            bcum = c3[:, 0:kc] + c3[:, kc:2 * kc] + c3[:, 2 * kc:3 * kc]
            eb = jnp.exp(bcum)
            qt_ref[d, rows, :] = (q2 * eb).astype(BF16)
            kt_ref[d, rows, :] = k2 * jnp.exp(-bcum)
            for ci in range(2):
                edge = ci * c if d == 1 else (ci + 1) * c - 1
                dec_ref[d, 2 * p + ci] = jnp.broadcast_to(eb[edge:edge + 1], (8, kc))

    def intra_scores(p):
        for ci in range(2):
            start = p * pair + ci * c
            rows = pl.ds(start if isinstance(start, int) else pl.multiple_of(start, c), c)
            for d in range(2):
                kt_bd = jnp.where(blk_k, jnp.concatenate([kt_ref[d, rows, :]] * GLA_HEADS, axis=0), 0.0)
                a = _dot_nt(qt_ref[d, rows, :], kt_bd.astype(BF16))
                a_ref[2 * p + ci, d * c:(d + 1) * c, :] = jnp.where(causal[d], a, 0.0).astype(BF16)

    def intra_out_and_kv(p):
        rows2 = pair_rows(p)
        v2 = v_ref[0, rows2, :]
        vt2 = vt_ref[:, rows2]
        kt2 = jnp.concatenate([kt_ref[0, rows2, :], kt_ref[1, rows2, :]], axis=1)
        for ci in range(2):
            sl = slice(ci * c, (ci + 1) * c)
            start = p * pair + ci * c
            rows = pl.ds(start if isinstance(start, int) else pl.multiple_of(start, c), c)
            v_bd = jnp.where(blk_v, jnp.concatenate([v2[sl]] * GLA_HEADS, axis=0), 0.0).astype(BF16)
            o_both = _dot(a_ref[2 * p + ci], v_bd)
            acc_ref[rows, :] = o_both[0:c] + o_both[c:2 * c]
            gap = jnp.zeros((c, 2 * kc), F32)
            kt_pair = jnp.concatenate([kt2[sl], gap] if ci == 0 else [gap, kt2[sl]], axis=0).astype(BF16)
            kv = _dot(vt2, kt_pair)
            kv_ref[0, 2 * p + ci] = jnp.where(blk_k, kv[:, 0:kc], 0.0)
            kv_ref[1, 2 * p + ci] = jnp.where(blk_k, kv[:, kc:2 * kc], 0.0)

    decayed_qk(0)
    intra_scores(0)
    decayed_qk(1)

    def intra_step(p, carry):
        intra_out_and_kv(p)
        intra_scores(p + 1)
        decayed_qk(p + 2)
        return carry

    lax.fori_loop(0, n_pairs - 2, intra_step, 0)
    intra_out_and_kv(n_pairs - 2)
    intra_scores(n_pairs - 1)
    intra_out_and_kv(n_pairs - 1)

    st_ref[...] = jnp.zeros_like(st_ref)

    def scan_step(i, carry):
        for d in range(2):
            n = i if d == 0 else n_chunks - 1 - i
            rows = pl.ds(pl.multiple_of(n * c, c), c)
            st = st_ref[d]
            acc_ref[rows, :] += _dot_nt(qt_ref[d, rows, :], st.astype(BF16))
            st = (st + kv_ref[d, n]).reshape(GLA_WIDTH // 8, 8, kc) * dec_ref[d, n][None]
            st_ref[d] = st.reshape(GLA_WIDTH, kc)
        return carry

    lax.fori_loop(0, n_chunks, scan_step, 0, unroll=2)

    o = acc_ref[...]
    o2 = o * o
    o2_hi = o2.astype(BF16)
    o2_lo = (o2 - o2_hi.astype(F32)).astype(BF16)
    gi = lax.broadcasted_iota(jnp.int32, (GLA_WIDTH, GLA_WIDTH), 0) // GLA_V_DIM
    gj = lax.broadcasted_iota(jnp.int32, (GLA_WIDTH, GLA_WIDTH), 1) // GLA_V_DIM
    ones_bd = (gi == gj).astype(BF16)
    ms = (_dot(o2_hi, ones_bd) + _dot(o2_lo, ones_bd)) * (1.0 / GLA_V_DIM)
    r = r_ref[0]
    gate = r * (1.0 / (1.0 + jnp.exp(-r)))
    o_ref[0] = (o * lax.rsqrt(ms + LN_EPS) * g_ref[...] * gate).astype(BF16)


def _gla(gqk, gv, gr, gz, w2_bd, b2_cat, g_tiled):
    b, s, _ = gv.shape
    per_b = lambda bi: (bi, 0, 0)
    const = lambda bi: (0, 0)
    return pl.pallas_call(
        _gla_kernel,
        grid=(b,),
        in_specs=[
            pl.BlockSpec((1, s, 2 * GLA_K_COLS), per_b),
            pl.BlockSpec((1, s, GLA_WIDTH), per_b),
            pl.BlockSpec((1, s, GLA_WIDTH), per_b),
            pl.BlockSpec((1, s, 2 * GLA_GATE_RANK), per_b),
            pl.BlockSpec((2 * GLA_GATE_RANK, 2 * GLA_K_COLS), const),
            pl.BlockSpec((1, 2 * GLA_K_COLS), const),
            pl.BlockSpec((1, GLA_WIDTH), const),
        ],
        out_specs=pl.BlockSpec((1, s, GLA_WIDTH), per_b),
        out_shape=jax.ShapeDtypeStruct((b, s, GLA_WIDTH), BF16),
        scratch_shapes=[
            pltpu.VMEM((s, 2 * GLA_K_COLS), F32),
            pltpu.VMEM((GLA_WIDTH, s), BF16),
            pltpu.VMEM((s, GLA_WIDTH), F32),
            pltpu.VMEM((2, s, GLA_K_COLS), BF16),
            pltpu.VMEM((2, s, GLA_K_COLS), F32),
            pltpu.VMEM((s // GLA_CHUNK, 2 * GLA_CHUNK, GLA_WIDTH), BF16),
            pltpu.VMEM((2, s // GLA_CHUNK, GLA_WIDTH, GLA_K_COLS), F32),
            pltpu.VMEM((2, s // GLA_CHUNK, 8, GLA_K_COLS), F32),
            pltpu.VMEM((2, GLA_WIDTH, GLA_K_COLS), F32),
        ],
        compiler_params=_params(1),
        name="gla",
    )(gqk, gv, gr, gz, w2_bd, b2_cat, g_tiled)


def _layer_norm(y, g, b):
    mu = jnp.mean(y, axis=1, keepdims=True)
    d = y - mu
    var = jnp.mean(d * d, axis=1, keepdims=True)
    return d * lax.rsqrt(var + LN_EPS) * g + b


ROW_SUB = 512


def _outproj_kernel(od_ref, of_ref, og_ref, x_ref, w_ref, g_ref, b_ref, o_ref):
    w = w_ref[...].astype(BF16)
    for r in range(x_ref.shape[0] // ROW_SUB):
        rows = slice(r * ROW_SUB, (r + 1) * ROW_SUB)
        mixed = jnp.concatenate([od_ref[0, hd, rows, :] for hd in range(DIFF_HEADS)]
                                + [of_ref[rows, :], og_ref[rows, :]], axis=1)
        m = _dot(mixed, w)
        o_ref[rows, :] = _layer_norm(DEEPNORM_ALPHA * x_ref[rows, :] + m, g_ref[...], b_ref[...])


def _out_proj(o_diff, o_four, o_gla, x2d, w_all, layer, g, b, tm=1024):
    m = x2d.shape[0]
    seq = o_diff.shape[2]
    n_seq_tiles = seq // tm
    row = lambda i: (i, 0)
    const = lambda i: (0, 0)
    return pl.pallas_call(
        _outproj_kernel,
        grid=(m // tm,),
        in_specs=[
            pl.BlockSpec((1, DIFF_HEADS, tm, LANES), lambda i: (i // n_seq_tiles, 0, i % n_seq_tiles, 0)),
            pl.BlockSpec((tm, FOURIER_WIDTH), row),
            pl.BlockSpec((tm, GLA_WIDTH), row),
            pl.BlockSpec((tm, D_MODEL), row),
            _layer_weight(w_all, layer),
            pl.BlockSpec((1, D_MODEL), const),
            pl.BlockSpec((1, D_MODEL), const),
        ],
        out_specs=pl.BlockSpec((tm, D_MODEL), row),
        out_shape=jax.ShapeDtypeStruct((m, D_MODEL), F32),
        compiler_params=_params(1),
        name="out_proj",
    )(o_diff, o_four, o_gla, x2d, w_all, g, b)


FFN_CHUNK = 256


def _ffn_kernel(x_ref, wg_ref, wu_ref, wd_ref, g_ref, b_ref, o_ref, acc_ref):
    x = x_ref[...]
    xb = x.astype(BF16)
    for ci in range(FFN_HIDDEN // FFN_CHUNK):
        cols = slice(ci * FFN_CHUNK, (ci + 1) * FFN_CHUNK)
        hg = _dot(xb, wg_ref[:, cols].astype(BF16))
        hu = _dot(xb, wu_ref[:, cols].astype(BF16))
        act = (hg * (1.0 / (1.0 + jnp.exp(-hg))) * hu).astype(BF16)
        part = _dot(act, wd_ref[cols, :].astype(BF16))
        if ci == 0:
            acc_ref[...] = part
        else:
            acc_ref[...] += part
    o_ref[...] = _layer_norm(DEEPNORM_ALPHA * x + acc_ref[...], g_ref[...], b_ref[...])


def _ffn(x2d, wg_all, wu_all, wd_all, layer, g, b, tm=512):
    m = x2d.shape[0]
    row = lambda i: (i, 0)
    const = lambda i: (0, 0)
    return pl.pallas_call(
        _ffn_kernel,
        grid=(m // tm,),
        in_specs=[
            pl.BlockSpec((tm, D_MODEL), row),
            _layer_weight(wg_all, layer),
            _layer_weight(wu_all, layer),
            _layer_weight(wd_all, layer),
            pl.BlockSpec((1, D_MODEL), const),
            pl.BlockSpec((1, D_MODEL), const),
        ],
        out_specs=pl.BlockSpec((tm, D_MODEL), row),
        out_shape=jax.ShapeDtypeStruct((m, D_MODEL), F32),
        scratch_shapes=[pltpu.VMEM((tm, D_MODEL), F32)],
        compiler_params=_params(1),
        name="ffn",
    )(x2d, wg_all, wu_all, wd_all, g, b)


def _rope_tables(seq):
    half = DIFF_QK_DIM // 2
    pos = jnp.arange(seq, dtype=F32)
    inv_freq = ROPE_THETA ** (-jnp.arange(0, DIFF_QK_DIM, 2, dtype=F32) / DIFF_QK_DIM)
    ang = pos[:, None] * inv_freq[None, :]
    cos, sin = jnp.cos(ang), jnp.sin(ang)
    reps = LANES // DIFF_QK_DIM
    cos128 = jnp.tile(jnp.concatenate([cos, cos], axis=1), (1, reps))
    sin128 = jnp.tile(jnp.concatenate([-sin, sin], axis=1), (1, reps))
    assert cos128.shape == (seq, LANES) and half * 2 == DIFF_QK_DIM
    return cos128, sin128, cos128.T, sin128.T


def _dft_tables(seq):
    half = seq // 2
    j = np.arange(half, dtype=np.int64)[:, None]
    pos = np.arange(half, dtype=np.int64)[None, :]
    ang_even = 2.0 * np.pi * ((j * pos) % half).astype(np.float64) / half
    ang_odd = 2.0 * np.pi * (((2 * j + 1) * pos) % seq).astype(np.float64) / seq
    f_even = np.concatenate([np.cos(ang_even), -np.sin(ang_even)], axis=1)
    f_odd = np.concatenate([np.cos(ang_odd), -np.sin(ang_odd)], axis=1)
    c = np.arange(FOURIER_GROUP_DIM, dtype=np.int64)
    ang_c = 2.0 * np.pi * ((c[:, None] * c[None, :]) % FOURIER_GROUP_DIM).astype(np.float64) / FOURIER_GROUP_DIM
    eye = np.eye(FOURIER_GROUPS)
    cc_bd = np.kron(eye, np.cos(ang_c))
    sc_bd = np.kron(eye, np.sin(ang_c))
    return tuple(jnp.asarray(t, dtype=F32).astype(BF16) for t in (f_even, f_odd, cc_bd, sc_bd))


def _block_diag(blocks):
    n = len(blocks)
    rows = []
    for i, blk in enumerate(blocks):
        rows.append(jnp.concatenate(
            [blk if j == i else jnp.zeros((blk.shape[0], blocks[j].shape[1]), blk.dtype) for j in range(n)],
            axis=1))
    return jnp.concatenate(rows, axis=0)


def kernel(x, w_in, diff_lambda, diff_norm_g, fourier_w, gla_gate_w2, gla_gate_b2, gla_norm_g, w_out,
           ln1_g, ln1_b, ffn_w_gate, ffn_w_up, ffn_w_down, ln2_g, ln2_b):
    b, s, d = x.shape
    m = b * s
    rope = _rope_tables(s)
    f_even, f_odd, cc_bd, sc_bd = _dft_tables(s)
    x2d = x.reshape(m, d)
    for l in range(DEPTH):
        lam_init = 0.8 - 0.6 * math.exp(-0.3 * l)
        wqv_t = jnp.concatenate([w_in[l, :, OFF_DQ:OFF_DK], w_in[l, :, OFF_DV:OFF_FU]], axis=1).T.astype(BF16)
        qt, k, vt, fu, gqk, gv, gr, gz = _in_proj(x2d, wqv_t, w_in, l, rope, s)
        o_diff = _diff_attention(qt, k, vt, diff_lambda[l], diff_norm_g[l].reshape(1, -1), lam_init)
        w_four = _block_diag([fourier_w[l, g] for g in range(FOURIER_GROUPS)]).astype(BF16)
        o_four = _fourier(fu.reshape(b, s, -1), cc_bd, sc_bd, f_even, f_odd, w_four)
        w2_bd = _block_diag([gla_gate_w2[l, 0], gla_gate_w2[l, 1]]).astype(BF16)
        b2_cat = gla_gate_b2[l].reshape(1, -1)
        g_gla = jnp.tile(gla_norm_g[l], GLA_HEADS).reshape(1, -1)
        o_gla = _gla(gqk.reshape(b, s, -1), gv.reshape(b, s, -1), gr.reshape(b, s, -1),
                     gz.reshape(b, s, -1), w2_bd, b2_cat, g_gla)
        x2d = _out_proj(o_diff, o_four.reshape(m, -1), o_gla.reshape(m, -1), x2d,
                        w_out, l, ln1_g[l].reshape(1, -1), ln1_b[l].reshape(1, -1))
        x2d = _ffn(x2d, ffn_w_gate, ffn_w_up, ffn_w_down, l,
                   ln2_g[l].reshape(1, -1), ln2_b[l].reshape(1, -1))
    return x2d.reshape(b, s, d)
```

```python
import functools
import math

import jax
import jax.numpy as jnp
import numpy as np
from jax import lax
from jax.experimental import pallas as pl
from jax.experimental.pallas import tpu as pltpu

D_MODEL = 1024
DEPTH = 2
DIFF_HEADS = 4
DIFF_QK_DIM = 64
DIFF_V_DIM = 128
DIFF_WIDTH = 512
DIFF_QK_COLS = 512
ROPE_THETA = 10000.0
FOURIER_GROUPS = 4
FOURIER_GROUP_DIM = 64
FOURIER_WIDTH = 256
GLA_HEADS = 4
GLA_V_DIM = 64
GLA_K_DIM = 32
GLA_WIDTH = 256
GLA_K_COLS = 128
GLA_GATE_RANK = 16
GLA_GATE_TAU = 16.0
GLA_CHUNK = 64
IN_WIDTH = 2592
FFN_HIDDEN = 2816
DEEPNORM_ALPHA = (2 * DEPTH) ** 0.25
LN_EPS = 1e-5

OFF_DQ, OFF_DK, OFF_DV, OFF_FU = 0, 512, 1024, 1536
OFF_GQ, OFF_GV, OFF_GR, OFF_GZ = 1792, 2048, 2304, 2560

LANES = 128
VMEM_LIMIT = 56 * 1024 * 1024

BF16 = jnp.bfloat16
F32 = jnp.float32


def _dot(a, b):
    return jnp.dot(a, b, preferred_element_type=F32)


def _dot_nt(a, b):
    return lax.dot_general(a, b, (((1,), (1,)), ((), ())), preferred_element_type=F32)


def _params(n_grid_dims):
    return pltpu.CompilerParams(
        dimension_semantics=("arbitrary",) * n_grid_dims,
        vmem_limit_bytes=VMEM_LIMIT)


def _rope_slab(t, cos, sin_signed, first_half):
    swapped = jnp.where(first_half, pltpu.roll(t, 96, 1), pltpu.roll(t, 32, 1))
    return t * cos + swapped * sin_signed


ATTN_TQ = 256
ATTN_KC = 256


def _rest_col(col):
    assert col >= OFF_DK and not OFF_DV <= col < OFF_FU
    return col - OFF_DK if col < OFF_DV else col - OFF_DK - DIFF_WIDTH


def _inproj_kernel(x_ref, wqv_ref, w_ref, cos_ref, sin_ref, cost_ref, sint_ref,
                   qt_ref, k_ref, vt_ref, fu_ref, gqk_ref, gv_ref, gr_ref, gz_ref):
    xb = x_ref[...].astype(BF16)
    tm = xb.shape[0]
    cos = cos_ref[...]
    sin_signed = sin_ref[...]
    lane = lax.broadcasted_iota(jnp.int32, cos.shape, 1)
    first_half = (lane % DIFF_QK_DIM) < (DIFF_QK_DIM // 2)
    qk_scale = DIFF_QK_DIM ** -0.5 * math.log2(math.e)
    hqv_t = _dot_nt(wqv_ref[...], xb)
    cos_t = cost_ref[...]
    sin_t = sint_ref[...]
    half = DIFF_QK_DIM // 2
    wcols = lambda lo, hi: w_ref[:, _rest_col(lo):_rest_col(hi - 1) + 1]
    hk = _dot(xb, wcols(OFF_DK, OFF_DV))
    for hd in range(DIFF_HEADS):
        qt = hqv_t[hd * LANES:(hd + 1) * LANES]
        swapped = jnp.concatenate(
            [qt[half:2 * half], qt[0:half], qt[3 * half:4 * half], qt[2 * half:3 * half]], axis=0)
        qt = ((qt * cos_t + swapped * sin_t) * qk_scale).astype(BF16)
        vt = hqv_t[DIFF_QK_COLS + hd * LANES:DIFF_QK_COLS + (hd + 1) * LANES].astype(BF16)
        for j in range(tm // ATTN_TQ):
            qt_ref[0, hd, j] = qt[:, j * ATTN_TQ:(j + 1) * ATTN_TQ]
        for j in range(tm // ATTN_KC):
            vt_ref[0, hd, j] = vt[:, j * ATTN_KC:(j + 1) * ATTN_KC]
        slab = slice(hd * LANES, (hd + 1) * LANES)
        k_ref[0, hd] = _rope_slab(hk[:, slab], cos, sin_signed, first_half).astype(BF16)
    hfu = _dot(xb, wcols(OFF_FU, OFF_GQ))
    for lane_half in range(FOURIER_WIDTH // LANES):
        fu_ref[lane_half] = hfu[:, lane_half * LANES:(lane_half + 1) * LANES]
    gqk_ref[...] = _dot(xb, wcols(OFF_GQ, OFF_GV))
    gv_ref[...] = _dot(xb, wcols(OFF_GV, OFF_GR))
    gr_ref[...] = _dot(xb, wcols(OFF_GR, OFF_GZ))
    gz_ref[...] = _dot(xb, wcols(OFF_GZ, IN_WIDTH))


def _layer_weight(w_all, layer):
    return pl.BlockSpec((None,) + w_all.shape[1:], lambda i: (layer,) + (0,) * (w_all.ndim - 1),
                        pipeline_mode=pl.Buffered(1))


def _in_proj(x2d, wqv_t, w_rest, rope, seq, tm=512):
    m = x2d.shape[0]
    batch = m // seq
    n_seq_tiles = seq // tm
    cos128, sin128, cos_t, sin_t = rope
    row = lambda i: (i, 0)
    const = lambda i: (0, 0)
    tab = lambda i: (i % n_seq_tiles, 0)
    tab_t = lambda i: (0, i % n_seq_tiles)
    tiled = lambda t: (jax.ShapeDtypeStruct((batch, DIFF_HEADS, seq // t, LANES, t), BF16),
                       pl.BlockSpec((1, DIFF_HEADS, tm // t, LANES, t),
                                    lambda i: (i // n_seq_tiles, 0, i % n_seq_tiles, 0, 0)))
    qt_shape, qt_spec = tiled(ATTN_TQ)
    vt_shape, vt_spec = tiled(ATTN_KC)
    k_shape = jax.ShapeDtypeStruct((batch, DIFF_HEADS, seq, LANES), BF16)
    k_spec = pl.BlockSpec((1, DIFF_HEADS, tm, LANES), lambda i: (i // n_seq_tiles, 0, i % n_seq_tiles, 0))
    fu_shape = jax.ShapeDtypeStruct((FOURIER_WIDTH // LANES, m, LANES), F32)
    fu_spec = pl.BlockSpec((FOURIER_WIDTH // LANES, tm, LANES), lambda i: (0, i, 0))
    flat_shapes = (
        jax.ShapeDtypeStruct((m, 2 * GLA_K_COLS), F32),
        jax.ShapeDtypeStruct((m, GLA_WIDTH), F32),
        jax.ShapeDtypeStruct((m, GLA_WIDTH), F32),
        jax.ShapeDtypeStruct((m, 2 * GLA_GATE_RANK), F32),
    )
    return pl.pallas_call(
        _inproj_kernel,
        grid=(m // tm,),
        in_specs=[
            pl.BlockSpec((tm, D_MODEL), row),
            pl.BlockSpec((DIFF_QK_COLS + DIFF_WIDTH, D_MODEL), const),
            pl.BlockSpec(w_rest.shape, const),
            pl.BlockSpec((tm, LANES), tab),
            pl.BlockSpec((tm, LANES), tab),
            pl.BlockSpec((LANES, tm), tab_t),
            pl.BlockSpec((LANES, tm), tab_t),
        ],
        out_specs=(qt_spec, k_spec, vt_spec, fu_spec)
        + tuple(pl.BlockSpec((tm, s.shape[1]), row) for s in flat_shapes),
        out_shape=(qt_shape, k_shape, vt_shape, fu_shape) + flat_shapes,
        compiler_params=_params(1),
        name="in_proj",
    )(x2d, wqv_t, w_rest, cos128, sin128, cos_t, sin_t)


ATTN_SUB = 128
ATTN_GROUP = 4


def _attn_kernel(lam_ref, g_ref, qt_ref, k_ref, vt_ref, o_ref,
                 sa_ref, sb_ref, pa_ref, pb_ref, mx_ref, coef_ref, acc_ref, *, lam_init):
    tq, kc, sub = ATTN_TQ, ATTN_KC, ATTN_SUB
    seq = k_ref.shape[2]
    tiles = seq // tq
    n_units = DIFF_HEADS * tiles
    n_chunks = seq // kc
    lp = lam_ref[...]
    lam = (jnp.exp(jnp.sum(lp[0:1] * lp[1:2], axis=1, keepdims=True))
           - jnp.exp(jnp.sum(lp[2:3] * lp[3:4], axis=1, keepdims=True)) + lam_init)
    gain = g_ref[...] * (1.0 - lam_init)
    feature = lax.broadcasted_iota(jnp.int32, (LANES, tq), 0)
    first_map = feature < DIFF_QK_DIM
    s_bufs = (sa_ref, sb_ref)
    p_bufs = (pa_ref, pb_ref)

    def step(k, par, do_a=True, do_b=True, do_c=True):
        s_w, s_r = s_bufs[par], s_bufs[1 - par]
        p_w, p_r = p_bufs[1 - par], p_bufs[par]
        if do_a:
            h_a = k // tiles
            qt = qt_ref[0, h_a, k % tiles]
            zero = jnp.zeros_like(qt)
            qw = (jnp.where(first_map, qt, zero), jnp.where(first_map, zero, qt))
        if do_b:
            m8 = (mx_ref[1 - par, 0], mx_ref[1 - par, 1])
        if do_c:
            h_c = (k - 2) // tiles
            c16 = (coef_ref[par, 0], coef_ref[par, 1])

        def chunk(c, carry):
            mx = [carry[0], carry[1]]
            ls = [carry[2], carry[3]]
            acc = carry[4]
            for part in range(kc // sub):
                rows = pl.ds(pl.multiple_of(c * kc + part * sub, sub), sub)
                if do_a:
                    keys = k_ref[0, h_a, rows, :]
                    for m in range(2):
                        st = _dot(keys, qw[m])
                        s_w[m, rows, :] = st
                        mx[m] = jnp.maximum(mx[m], jnp.max(st.reshape(sub // 8, 8, tq), axis=0))
                if do_b:
                    for m in range(2):
                        p = jnp.exp2(s_r[m, rows, :].reshape(sub // 8, 8, tq) - m8[m][None])
                        ls[m] = ls[m] + jnp.sum(p, axis=0)
                        p_w[m, rows, :] = p.reshape(sub, tq).astype(BF16)
            if do_c:
                rows = pl.ds(pl.multiple_of(c * kc, kc), kc)
                a = (p_r[0, rows, :].reshape(kc // 16, 16, tq) * c16[0][None]
                     - p_r[1, rows, :].reshape(kc // 16, 16, tq) * c16[1][None])
                part_o = _dot(vt_ref[0, h_c, c], a.reshape(kc, tq))
                acc = part_o if acc is None else acc + part_o
            return mx[0], mx[1], ls[0], ls[1], acc

        lowest = jnp.full((8, tq), -jnp.inf, F32)
        nothing = jnp.zeros((8, tq), F32)
        def group(gi, carry):
            stats = carry + (None,)
            for c in range(ATTN_GROUP):
                stats = chunk(gi * ATTN_GROUP + c, stats)
            if do_c:
                acc_ref[...] += stats[4]
            return stats[:4]

        if do_c:
            acc_ref[...] = jnp.zeros_like(acc_ref)
        mx0, mx1, ls0, ls1 = lax.fori_loop(0, n_chunks // ATTN_GROUP, group, (lowest, lowest, nothing, nothing))
        acc = acc_ref[...] if do_c else None
        if do_a:
            mx_ref[par, 0] = jnp.broadcast_to(jnp.max(mx0, axis=0, keepdims=True), (8, tq))
            mx_ref[par, 1] = jnp.broadcast_to(jnp.max(mx1, axis=0, keepdims=True), (8, tq))
        if do_b:
            l0 = jnp.sum(ls0, axis=0, keepdims=True)
            l1 = jnp.sum(ls1, axis=0, keepdims=True)
            coef_ref[1 - par, 0] = jnp.broadcast_to(1.0 / l0, (16, tq)).astype(BF16)
            coef_ref[1 - par, 1] = jnp.broadcast_to(lam / l1, (16, tq)).astype(BF16)
        if do_c:
            o = acc.T
            ms = jnp.mean(o * o, axis=1, keepdims=True)
            q_start = ((k - 2) % tiles) * tq
            q_rows = pl.ds(q_start if isinstance(q_start, int) else pl.multiple_of(q_start, tq), tq)
            o_ref[0, h_c, q_rows, :] = (o * lax.rsqrt(ms + LN_EPS) * gain).astype(BF16)

    def step_pair(j, carry):
        step(2 * j, 0)
        step(2 * j + 1, 1)
        return carry

    step(0, 0, do_b=False, do_c=False)
    step(1, 1, do_c=False)
    lax.fori_loop(1, n_units // 2, step_pair, 0)
    step(n_units, 0, do_a=False)
    step(n_units + 1, 1, do_a=False, do_b=False)


def _diff_attention(qt, k, vt, lam_params, g, lam_init):
    b, _, s, _ = k.shape
    whole = lambda a: pl.BlockSpec((1,) + a.shape[1:], lambda bi: (bi,) + (0,) * (a.ndim - 1))
    return pl.pallas_call(
        functools.partial(_attn_kernel, lam_init=lam_init),
        grid=(b,),
        in_specs=[
            pl.BlockSpec((4, DIFF_QK_DIM), lambda bi: (0, 0)),
            pl.BlockSpec((1, DIFF_V_DIM), lambda bi: (0, 0)),
            whole(qt), whole(k), whole(vt),
        ],
        out_specs=whole(k),
        out_shape=jax.ShapeDtypeStruct((b, DIFF_HEADS, s, LANES), BF16),
        scratch_shapes=[
            pltpu.VMEM((2, s, ATTN_TQ), F32), pltpu.VMEM((2, s, ATTN_TQ), F32),
            pltpu.VMEM((2, s, ATTN_TQ), BF16), pltpu.VMEM((2, s, ATTN_TQ), BF16),
            pltpu.VMEM((2, 2, 8, ATTN_TQ), F32),
            pltpu.VMEM((2, 2, 16, ATTN_TQ), BF16),
            pltpu.VMEM((DIFF_V_DIM, ATTN_TQ), F32),
        ],
        compiler_params=_params(1),
        name="diff_attn",
    )(lam_params, g, qt, k, vt)


FOURIER_PACK = 2


def _fourier_kernel(u_ref, cc_ref, sc_ref, fe_ref, fo_ref, w_ref, o_ref, ab_ref, *, scale):
    s = u_ref.shape[2]
    half = s // 2
    w = FOURIER_WIDTH
    for e in range(FOURIER_PACK):
        lanes = slice(e * w, (e + 1) * w)
        for parity in range(2):
            u = jnp.concatenate([u_ref[lh, e, pl.ds(parity, half, stride=2), :] for lh in range(2)],
                                axis=1).astype(BF16)
            ab_ref[parity, 0:half, lanes] = _dot(u, cc_ref[...]).astype(BF16)
            ab_ref[parity, half:s, lanes] = _dot(u, sc_ref[...]).astype(BF16)
    even = _dot(fe_ref[...].astype(BF16), ab_ref[0])
    odd = _dot(fo_ref[...].astype(BF16), ab_ref[1])
    for rows, z in ((slice(0, half), (even + odd) * scale), (slice(half, s), (even - odd) * scale)):
        for e in range(FOURIER_PACK):
            y = _dot(z[:, e * w:(e + 1) * w].astype(BF16), w_ref[...])
            o_ref[e, rows, :] = y.astype(BF16)


def _fourier(u, cc_bd, sc_bd, f_even, f_odd, w_bd):
    _, b, s, _ = u.shape
    pack = FOURIER_PACK
    scale = 1.0 / math.sqrt(s * FOURIER_GROUP_DIM)
    const2 = lambda bi: (0, 0)
    table = pl.BlockSpec((s // 2, s), const2, pipeline_mode=pl.Buffered(1))
    return pl.pallas_call(
        functools.partial(_fourier_kernel, scale=scale),
        grid=(b // pack,),
        in_specs=[
            pl.BlockSpec((2, pack, s, LANES), lambda bi: (0, bi, 0, 0)),
            pl.BlockSpec((FOURIER_WIDTH, FOURIER_WIDTH), const2),
            pl.BlockSpec((FOURIER_WIDTH, FOURIER_WIDTH), const2),
            table,
            table,
            pl.BlockSpec((FOURIER_WIDTH, FOURIER_WIDTH), const2),
        ],
        out_specs=pl.BlockSpec((pack, s, FOURIER_WIDTH), lambda bi: (bi, 0, 0)),
        out_shape=jax.ShapeDtypeStruct((b, s, FOURIER_WIDTH), BF16),
        scratch_shapes=[pltpu.VMEM((2, s, pack * FOURIER_WIDTH), BF16)],
        compiler_params=_params(1),
        name="fourier",
    )(u, cc_bd, sc_bd, f_even, f_odd, w_bd)


def _split3(g):
    hi = g.astype(BF16)
    r1 = g - hi.astype(F32)
    mid = r1.astype(BF16)
    lo = (r1 - mid.astype(F32)).astype(BF16)
    return hi, mid, lo


def _gla_kernel(qk_ref, v_ref, r_ref, z_ref, w2_ref, b2_ref, g_ref, o_ref,
                la_ref, vt_ref, acc_ref, qt_ref, kt_ref, a_ref, kv_ref, dec_ref, st_ref):
    s = v_ref.shape[1]
    c = GLA_CHUNK
    pair = 2 * c
    n_pairs = s // pair
    n_chunks = s // c
    kc = GLA_K_COLS

    logit = _dot(z_ref[0].astype(BF16), w2_ref[...]) + b2_ref[...]
    la_ref[...] = (jnp.minimum(logit, 0.0) - jnp.log1p(jnp.exp(-jnp.abs(logit)))) * (1.0 / GLA_GATE_TAU)
    vt_ref[...] = v_ref[0].T.astype(BF16)

    row_i = lax.broadcasted_iota(jnp.int32, (pair, pair), 0)
    col_i = lax.broadcasted_iota(jnp.int32, (pair, pair), 1)
    same_chunk = (row_i // c) == (col_i // c)
    cum_mats = ((same_chunk & (col_i <= row_i)).astype(BF16), (same_chunk & (col_i >= row_i)).astype(BF16))
    r_hk = lax.broadcasted_iota(jnp.int32, (GLA_HEADS * c, kc), 0) // c
    l_hk = lax.broadcasted_iota(jnp.int32, (GLA_HEADS * c, kc), 1) // GLA_K_DIM
    blk_k = r_hk == l_hk
    r_hv = lax.broadcasted_iota(jnp.int32, (GLA_HEADS * c, GLA_WIDTH), 0) // c
    l_hv = lax.broadcasted_iota(jnp.int32, (GLA_HEADS * c, GLA_WIDTH), 1) // GLA_V_DIM
    blk_v = r_hv == l_hv
    qi = lax.broadcasted_iota(jnp.int32, (c, GLA_HEADS * c), 0)
    kj = lax.broadcasted_iota(jnp.int32, (c, GLA_HEADS * c), 1) % c
    q_scale = GLA_K_DIM ** -0.5

    causal = (kj <= qi, kj >= qi)

    def pair_rows(p):
        start = p * pair
        return pl.ds(start if isinstance(start, int) else pl.multiple_of(start, pair), pair)

    def decayed_qk(p):
        rows = pair_rows(p)
        q2 = qk_ref[0, rows, 0:kc] * q_scale
        k2 = qk_ref[0, rows, kc:2 * kc]
        for d in range(2):
            hi, mid, lo = _split3(la_ref[rows, d * kc:(d + 1) * kc])
            c3 = _dot(cum_mats[d], jnp.concatenate([hi, mid, lo], axis=1))
            bcum = c3[:, 0:kc] + c3[:, kc:2 * kc] + c3[:, 2 * kc:3 * kc]
            eb = jnp.exp(bcum)
            qt_ref[d, rows, :] = (q2 * eb).astype(BF16)
            kt_ref[d, rows, :] = k2 * jnp.exp(-bcum)
            for ci in range(2):
                edge = ci * c if d == 1 else (ci + 1) * c - 1
                dec_ref[d, 2 * p + ci] = jnp.broadcast_to(eb[edge:edge + 1], (8, kc))

    def intra_scores(p):
        for ci in range(2):
            start = p * pair + ci * c
            rows = pl.ds(start if isinstance(start, int) else pl.multiple_of(start, c), c)
            for d in range(2):
                kt_bd = jnp.where(blk_k, jnp.concatenate([kt_ref[d, rows, :]] * GLA_HEADS, axis=0), 0.0)
                a = _dot_nt(qt_ref[d, rows, :], kt_bd.astype(BF16))
                a_ref[2 * p + ci, d * c:(d + 1) * c, :] = jnp.where(causal[d], a, 0.0).astype(BF16)

    def intra_out_and_kv(p):
        rows2 = pair_rows(p)
        v2 = v_ref[0, rows2, :]
        vt2 = vt_ref[:, rows2]
        kt2 = jnp.concatenate([kt_ref[0, rows2, :], kt_ref[1, rows2, :]], axis=1)
        for ci in range(2):
            sl = slice(ci * c, (ci + 1) * c)
            start = p * pair + ci * c
            rows = pl.ds(start if isinstance(start, int) else pl.multiple_of(start, c), c)
            v_bd = jnp.where(blk_v, jnp.concatenate([v2[sl]] * GLA_HEADS, axis=0), 0.0).astype(BF16)
            o_both = _dot(a_ref[2 * p + ci], v_bd)
            acc_ref[rows, :] = o_both[0:c] + o_both[c:2 * c]
            gap = jnp.zeros((c, 2 * kc), F32)
            kt_pair = jnp.concatenate([kt2[sl], gap] if ci == 0 else [gap, kt2[sl]], axis=0).astype(BF16)
            kv = _dot(vt2, kt_pair)
            kv_ref[0, 2 * p + ci] = jnp.where(blk_k, kv[:, 0:kc], 0.0)
            kv_ref[1, 2 * p + ci] = jnp.where(blk_k, kv[:, kc:2 * kc], 0.0)

    decayed_qk(0)
    intra_scores(0)
    decayed_qk(1)

    def intra_step(p, carry):
        intra_out_and_kv(p)
        intra_scores(p + 1)
        decayed_qk(p + 2)
        return carry

    lax.fori_loop(0, n_pairs - 2, intra_step, 0)
    intra_out_and_kv(n_pairs - 2)
    intra_scores(n_pairs - 1)
    intra_out_and_kv(n_pairs - 1)

    st_ref[...] = jnp.zeros_like(st_ref)

    def scan_step(i, carry):
        for d in range(2):
            n = i if d == 0 else n_chunks - 1 - i
            rows = pl.ds(pl.multiple_of(n * c, c), c)
            st = st_ref[d]
            acc_ref[rows, :] += _dot_nt(qt_ref[d, rows, :], st.astype(BF16))
            st = (st + kv_ref[d, n]).reshape(GLA_WIDTH // 8, 8, kc) * dec_ref[d, n][None]
            st_ref[d] = st.reshape(GLA_WIDTH, kc)
        return carry

    lax.fori_loop(0, n_chunks, scan_step, 0, unroll=2)

    o = acc_ref[...]
    o2 = o * o
    o2_hi = o2.astype(BF16)
    o2_lo = (o2 - o2_hi.astype(F32)).astype(BF16)
    gi = lax.broadcasted_iota(jnp.int32, (GLA_WIDTH, GLA_WIDTH), 0) // GLA_V_DIM
    gj = lax.broadcasted_iota(jnp.int32, (GLA_WIDTH, GLA_WIDTH), 1) // GLA_V_DIM
    ones_bd = (gi == gj).astype(BF16)
    ms = (_dot(o2_hi, ones_bd) + _dot(o2_lo, ones_bd)) * (1.0 / GLA_V_DIM)
    r = r_ref[0]
    gate = r * (1.0 / (1.0 + jnp.exp(-r)))
    o_ref[0] = (o * lax.rsqrt(ms + LN_EPS) * g_ref[...] * gate).astype(BF16)


def _gla(gqk, gv, gr, gz, w2_bd, b2_cat, g_tiled):
    b, s, _ = gv.shape
    per_b = lambda bi: (bi, 0, 0)
    const = lambda bi: (0, 0)
    return pl.pallas_call(
        _gla_kernel,
        grid=(b,),
        in_specs=[
            pl.BlockSpec((1, s, 2 * GLA_K_COLS), per_b),
            pl.BlockSpec((1, s, GLA_WIDTH), per_b),
            pl.BlockSpec((1, s, GLA_WIDTH), per_b),
            pl.BlockSpec((1, s, 2 * GLA_GATE_RANK), per_b),
            pl.BlockSpec((2 * GLA_GATE_RANK, 2 * GLA_K_COLS), const),
            pl.BlockSpec((1, 2 * GLA_K_COLS), const),
            pl.BlockSpec((1, GLA_WIDTH), const),
        ],
        out_specs=pl.BlockSpec((1, s, GLA_WIDTH), per_b),
        out_shape=jax.ShapeDtypeStruct((b, s, GLA_WIDTH), BF16),
        scratch_shapes=[
            pltpu.VMEM((s, 2 * GLA_K_COLS), F32),
            pltpu.VMEM((GLA_WIDTH, s), BF16),
            pltpu.VMEM((s, GLA_WIDTH), F32),
            pltpu.VMEM((2, s, GLA_K_COLS), BF16),
            pltpu.VMEM((2, s, GLA_K_COLS), F32),
            pltpu.VMEM((s // GLA_CHUNK, 2 * GLA_CHUNK, GLA_WIDTH), BF16),
            pltpu.VMEM((2, s // GLA_CHUNK, GLA_WIDTH, GLA_K_COLS), F32),
            pltpu.VMEM((2, s // GLA_CHUNK, 8, GLA_K_COLS), F32),
            pltpu.VMEM((2, GLA_WIDTH, GLA_K_COLS), F32),
        ],
        compiler_params=_params(1),
        name="gla",
    )(gqk, gv, gr, gz, w2_bd, b2_cat, g_tiled)


def _layer_norm(y, g, b):
    mu = jnp.mean(y, axis=1, keepdims=True)
    d = y - mu
    var = jnp.mean(d * d, axis=1, keepdims=True)
    return d * lax.rsqrt(var + LN_EPS) * g + b


ROW_SUB = 512


def _outproj_kernel(od_ref, of_ref, og_ref, x_ref, w_ref, g_ref, b_ref, o_ref):
    w = w_ref[...].astype(BF16)
    for r in range(x_ref.shape[0] // ROW_SUB):
        rows = slice(r * ROW_SUB, (r + 1) * ROW_SUB)
        mixed = jnp.concatenate([od_ref[0, hd, rows, :] for hd in range(DIFF_HEADS)]
                                + [of_ref[rows, :], og_ref[rows, :]], axis=1)
        m = _dot(mixed, w)
        o_ref[rows, :] = _layer_norm(DEEPNORM_ALPHA * x_ref[rows, :] + m, g_ref[...], b_ref[...])


def _out_proj(o_diff, o_four, o_gla, x2d, w_all, layer, g, b, tm=1024):
    m = x2d.shape[0]
    seq = o_diff.shape[2]
    n_seq_tiles = seq // tm
    row = lambda i: (i, 0)
    const = lambda i: (0, 0)
    return pl.pallas_call(
        _outproj_kernel,
        grid=(m // tm,),
        in_specs=[
            pl.BlockSpec((1, DIFF_HEADS, tm, LANES), lambda i: (i // n_seq_tiles, 0, i % n_seq_tiles, 0)),
            pl.BlockSpec((tm, FOURIER_WIDTH), row),
            pl.BlockSpec((tm, GLA_WIDTH), row),
            pl.BlockSpec((tm, D_MODEL), row),
            _layer_weight(w_all, layer),
            pl.BlockSpec((1, D_MODEL), const),
            pl.BlockSpec((1, D_MODEL), const),
        ],
        out_specs=pl.BlockSpec((tm, D_MODEL), row),
        out_shape=jax.ShapeDtypeStruct((m, D_MODEL), F32),
        compiler_params=_params(1),
        name="out_proj",
    )(o_diff, o_four, o_gla, x2d, w_all, g, b)


FFN_CHUNK = 256


def _ffn_kernel(x_ref, wg_ref, wu_ref, wd_ref, g_ref, b_ref, o_ref, acc_ref):
    x = x_ref[...]
    xb = x.astype(BF16)
    for ci in range(FFN_HIDDEN // FFN_CHUNK):
        cols = slice(ci * FFN_CHUNK, (ci + 1) * FFN_CHUNK)
        hg = _dot(xb, wg_ref[:, cols].astype(BF16))
        hu = _dot(xb, wu_ref[:, cols].astype(BF16))
        act = (hg * (1.0 / (1.0 + jnp.exp(-hg))) * hu).astype(BF16)
        part = _dot(act, wd_ref[cols, :].astype(BF16))
        if ci == 0:
            acc_ref[...] = part
        else:
            acc_ref[...] += part
    o_ref[...] = _layer_norm(DEEPNORM_ALPHA * x + acc_ref[...], g_ref[...], b_ref[...])


def _ffn(x2d, wg_all, wu_all, wd_all, layer, g, b, tm=512):
    m = x2d.shape[0]
    row = lambda i: (i, 0)
    const = lambda i: (0, 0)
    return pl.pallas_call(
        _ffn_kernel,
        grid=(m // tm,),
        in_specs=[
            pl.BlockSpec((tm, D_MODEL), row),
            _layer_weight(wg_all, layer),
            _layer_weight(wu_all, layer),
            _layer_weight(wd_all, layer),
            pl.BlockSpec((1, D_MODEL), const),
            pl.BlockSpec((1, D_MODEL), const),
        ],
        out_specs=pl.BlockSpec((tm, D_MODEL), row),
        out_shape=jax.ShapeDtypeStruct((m, D_MODEL), F32),
        scratch_shapes=[pltpu.VMEM((tm, D_MODEL), F32)],
        compiler_params=_params(1),
        name="ffn",
    )(x2d, wg_all, wu_all, wd_all, g, b)


def _rope_tables(seq):
    half = DIFF_QK_DIM // 2
    pos = jnp.arange(seq, dtype=F32)
    inv_freq = ROPE_THETA ** (-jnp.arange(0, DIFF_QK_DIM, 2, dtype=F32) / DIFF_QK_DIM)
    ang = pos[:, None] * inv_freq[None, :]
    cos, sin = jnp.cos(ang), jnp.sin(ang)
    reps = LANES // DIFF_QK_DIM
    cos128 = jnp.tile(jnp.concatenate([cos, cos], axis=1), (1, reps))
    sin128 = jnp.tile(jnp.concatenate([-sin, sin], axis=1), (1, reps))
    assert cos128.shape == (seq, LANES) and half * 2 == DIFF_QK_DIM
    return cos128, sin128, cos128.T, sin128.T


def _dft_tables(seq):
    half = seq // 2
    j = np.arange(half, dtype=np.int64)[:, None]
    pos = np.arange(half, dtype=np.int64)[None, :]
    ang_even = 2.0 * np.pi * ((j * pos) % half).astype(np.float64) / half
    ang_odd = 2.0 * np.pi * ((j * (2 * pos + 1)) % seq).astype(np.float64) / seq
    f_even = np.concatenate([np.cos(ang_even), -np.sin(ang_even)], axis=1)
    f_odd = np.concatenate([np.cos(ang_odd), -np.sin(ang_odd)], axis=1)
    c = np.arange(FOURIER_GROUP_DIM, dtype=np.int64)
    ang_c = 2.0 * np.pi * ((c[:, None] * c[None, :]) % FOURIER_GROUP_DIM).astype(np.float64) / FOURIER_GROUP_DIM
    eye = np.eye(FOURIER_GROUPS)
    cc_bd = np.kron(eye, np.cos(ang_c))
    sc_bd = np.kron(eye, np.sin(ang_c))
    return (jnp.asarray(f_even, dtype=F32), jnp.asarray(f_odd, dtype=F32),
            jnp.asarray(cc_bd, dtype=F32).astype(BF16), jnp.asarray(sc_bd, dtype=F32).astype(BF16))


def _block_diag(blocks):
    n = len(blocks)
    rows = []
    for i, blk in enumerate(blocks):
        rows.append(jnp.concatenate(
            [blk if j == i else jnp.zeros((blk.shape[0], blocks[j].shape[1]), blk.dtype) for j in range(n)],
            axis=1))
    return jnp.concatenate(rows, axis=0)


def kernel(x, w_in, diff_lambda, diff_norm_g, fourier_w, gla_gate_w2, gla_gate_b2, gla_norm_g, w_out,
           ln1_g, ln1_b, ffn_w_gate, ffn_w_up, ffn_w_down, ln2_g, ln2_b):
    b, s, d = x.shape
    m = b * s
    rope = _rope_tables(s)
    f_even, f_odd, cc_bd, sc_bd = _dft_tables(s)
    x2d = x.reshape(m, d)
    for l in range(DEPTH):
        lam_init = 0.8 - 0.6 * math.exp(-0.3 * l)
        wqv_t = jnp.concatenate([w_in[l, :, OFF_DQ:OFF_DK], w_in[l, :, OFF_DV:OFF_FU]], axis=1).T.astype(BF16)
        w_rest = jnp.concatenate([w_in[l, :, OFF_DK:OFF_DV], w_in[l, :, OFF_FU:]], axis=1).astype(BF16)
        qt, k, vt, fu, gqk, gv, gr, gz = _in_proj(x2d, wqv_t, w_rest, rope, s)
        o_diff = _diff_attention(qt, k, vt, diff_lambda[l], diff_norm_g[l].reshape(1, -1), lam_init)
        w_four = _block_diag([fourier_w[l, g] for g in range(FOURIER_GROUPS)]).astype(BF16)
        o_four = _fourier(fu.reshape(-1, b, s, LANES), cc_bd, sc_bd, f_even, f_odd, w_four)
        w2_bd = _block_diag([gla_gate_w2[l, 0], gla_gate_w2[l, 1]]).astype(BF16)
        b2_cat = gla_gate_b2[l].reshape(1, -1)
        g_gla = jnp.tile(gla_norm_g[l], GLA_HEADS).reshape(1, -1)
        o_gla = _gla(gqk.reshape(b, s, -1), gv.reshape(b, s, -1), gr.reshape(b, s, -1),
                     gz.reshape(b, s, -1), w2_bd, b2_cat, g_gla)
        x2d = _out_proj(o_diff, o_four.reshape(m, -1), o_gla.reshape(m, -1), x2d,
                        w_out, l, ln1_g[l].reshape(1, -1), ln1_b[l].reshape(1, -1))
        x2d = _ffn(x2d, ffn_w_gate, ffn_w_up, ffn_w_down, l,
                   ln2_g[l].reshape(1, -1), ln2_b[l].reshape(1, -1))
    return x2d.reshape(b, s, d)
```

```python
import functools
import math

import jax
import jax.numpy as jnp
import numpy as np
from jax import lax
from jax.experimental import pallas as pl
from jax.experimental.pallas import tpu as pltpu

D_MODEL = 1024
DEPTH = 2
DIFF_HEADS = 4
DIFF_QK_DIM = 64
DIFF_V_DIM = 128
DIFF_WIDTH = 512
DIFF_QK_COLS = 512
ROPE_THETA = 10000.0
FOURIER_GROUPS = 4
FOURIER_GROUP_DIM = 64
FOURIER_WIDTH = 256
GLA_HEADS = 4
GLA_V_DIM = 64
GLA_K_DIM = 32
GLA_WIDTH = 256
GLA_K_COLS = 128
GLA_GATE_RANK = 16
GLA_GATE_TAU = 16.0
GLA_CHUNK = 64
IN_WIDTH = 2592
FFN_HIDDEN = 2816
DEEPNORM_ALPHA = (2 * DEPTH) ** 0.25
LN_EPS = 1e-5

OFF_DQ, OFF_DK, OFF_DV, OFF_FU = 0, 512, 1024, 1536
OFF_GQ, OFF_GV, OFF_GR, OFF_GZ = 1792, 2048, 2304, 2560

LANES = 128
VMEM_LIMIT = 56 * 1024 * 1024

BF16 = jnp.bfloat16
F32 = jnp.float32


def _dot(a, b):
    return jnp.dot(a, b, preferred_element_type=F32)


def _dot_nt(a, b):
    return lax.dot_general(a, b, (((1,), (1,)), ((), ())), preferred_element_type=F32)


def _params(n_grid_dims):
    return pltpu.CompilerParams(
        dimension_semantics=("arbitrary",) * n_grid_dims,
        vmem_limit_bytes=VMEM_LIMIT)


def _rope_slab(t, cos, sin_signed, first_half):
    swapped = jnp.where(first_half, pltpu.roll(t, 96, 1), pltpu.roll(t, 32, 1))
    return t * cos + swapped * sin_signed


ATTN_TQ = 256
ATTN_KC = 256


def _rest_col(col):
    assert col >= OFF_DK and not OFF_DV <= col < OFF_FU
    return col - OFF_DK if col < OFF_DV else col - OFF_DK - DIFF_WIDTH


def _inproj_kernel(x_ref, wqv_ref, w_ref, cos_ref, sin_ref, cost_ref, sint_ref,
                   qt_ref, k_ref, vt_ref, fu_ref, gqk_ref, gv_ref, gr_ref, gz_ref):
    xb = x_ref[...].astype(BF16)
    tm = xb.shape[0]
    cos = cos_ref[...]
    sin_signed = sin_ref[...]
    lane = lax.broadcasted_iota(jnp.int32, cos.shape, 1)
    first_half = (lane % DIFF_QK_DIM) < (DIFF_QK_DIM // 2)
    qk_scale = DIFF_QK_DIM ** -0.5 * math.log2(math.e)
    hqv_t = _dot_nt(wqv_ref[...], xb)
    cos_t = cost_ref[...]
    sin_t = sint_ref[...]
    half = DIFF_QK_DIM // 2
    wcols = lambda lo, hi: w_ref[:, _rest_col(lo):_rest_col(hi - 1) + 1]
    hk = _dot(xb, wcols(OFF_DK, OFF_DV))
    for hd in range(DIFF_HEADS):
        qt = hqv_t[hd * LANES:(hd + 1) * LANES]
        swapped = jnp.concatenate(
            [qt[half:2 * half], qt[0:half], qt[3 * half:4 * half], qt[2 * half:3 * half]], axis=0)
        qt = ((qt * cos_t + swapped * sin_t) * qk_scale).astype(BF16)
        vt = hqv_t[DIFF_QK_COLS + hd * LANES:DIFF_QK_COLS + (hd + 1) * LANES].astype(BF16)
        for j in range(tm // ATTN_TQ):
            qt_ref[0, hd, j] = qt[:, j * ATTN_TQ:(j + 1) * ATTN_TQ]
        for j in range(tm // ATTN_KC):
            vt_ref[0, hd, j] = vt[:, j * ATTN_KC:(j + 1) * ATTN_KC]
        slab = slice(hd * LANES, (hd + 1) * LANES)
        k_ref[0, hd] = _rope_slab(hk[:, slab], cos, sin_signed, first_half).astype(BF16)
    hfu = _dot(xb, wcols(OFF_FU, OFF_GQ))
    for lane_half in range(FOURIER_WIDTH // LANES):
        fu_ref[lane_half] = hfu[:, lane_half * LANES:(lane_half + 1) * LANES]
    gqk_ref[...] = _dot(xb, wcols(OFF_GQ, OFF_GV))
    gv_ref[...] = _dot(xb, wcols(OFF_GV, OFF_GR))
    gr_ref[...] = _dot(xb, wcols(OFF_GR, OFF_GZ))
    gz_ref[...] = _dot(xb, wcols(OFF_GZ, IN_WIDTH))


def _layer_weight(w_all, layer):
    return pl.BlockSpec((None,) + w_all.shape[1:], lambda i: (layer,) + (0,) * (w_all.ndim - 1),
                        pipeline_mode=pl.Buffered(1))


def _in_proj(x2d, wqv_t, w_rest, rope, seq, tm=512):
    m = x2d.shape[0]
    batch = m // seq
    n_seq_tiles = seq // tm
    cos128, sin128, cos_t, sin_t = rope
    row = lambda i: (i, 0)
    const = lambda i: (0, 0)
    tab = lambda i: (i % n_seq_tiles, 0)
    tab_t = lambda i: (0, i % n_seq_tiles)
    tiled = lambda t: (jax.ShapeDtypeStruct((batch, DIFF_HEADS, seq // t, LANES, t), BF16),
                       pl.BlockSpec((1, DIFF_HEADS, tm // t, LANES, t),
                                    lambda i: (i // n_seq_tiles, 0, i % n_seq_tiles, 0, 0)))
    qt_shape, qt_spec = tiled(ATTN_TQ)
    vt_shape, vt_spec = tiled(ATTN_KC)
    k_shape = jax.ShapeDtypeStruct((batch, DIFF_HEADS, seq, LANES), BF16)
    k_spec = pl.BlockSpec((1, DIFF_HEADS, tm, LANES), lambda i: (i // n_seq_tiles, 0, i % n_seq_tiles, 0))
    fu_shape = jax.ShapeDtypeStruct((FOURIER_WIDTH // LANES, m, LANES), F32)
    fu_spec = pl.BlockSpec((FOURIER_WIDTH // LANES, tm, LANES), lambda i: (0, i, 0))
    flat_shapes = (
        jax.ShapeDtypeStruct((m, 2 * GLA_K_COLS), F32),
        jax.ShapeDtypeStruct((m, GLA_WIDTH), F32),
        jax.ShapeDtypeStruct((m, GLA_WIDTH), F32),
        jax.ShapeDtypeStruct((m, 2 * GLA_GATE_RANK), F32),
    )
    return pl.pallas_call(
        _inproj_kernel,
        grid=(m // tm,),
        in_specs=[
            pl.BlockSpec((tm, D_MODEL), row),
            pl.BlockSpec((DIFF_QK_COLS + DIFF_WIDTH, D_MODEL), const),
            pl.BlockSpec(w_rest.shape, const),
            pl.BlockSpec((tm, LANES), tab),
            pl.BlockSpec((tm, LANES), tab),
            pl.BlockSpec((LANES, tm), tab_t),
            pl.BlockSpec((LANES, tm), tab_t),
        ],
        out_specs=(qt_spec, k_spec, vt_spec, fu_spec)
        + tuple(pl.BlockSpec((tm, s.shape[1]), row) for s in flat_shapes),
        out_shape=(qt_shape, k_shape, vt_shape, fu_shape) + flat_shapes,
        compiler_params=_params(1),
        name="in_proj",
    )(x2d, wqv_t, w_rest, cos128, sin128, cos_t, sin_t)


ATTN_SUB = 128
ATTN_GROUP = 4


def _attn_kernel(lam_ref, g_ref, qt_ref, k_ref, vt_ref, o_ref,
                 sa_ref, sb_ref, pa_ref, pb_ref, mx_ref, coef_ref, ot_ref, *, lam_init):
    tq, kc, sub = ATTN_TQ, ATTN_KC, ATTN_SUB
    seq = k_ref.shape[2]
    tiles = seq // tq
    n_units = DIFF_HEADS * tiles
    n_chunks = seq // kc
    lp = lam_ref[...]
    lam = (jnp.exp(jnp.sum(lp[0:1] * lp[1:2], axis=1, keepdims=True))
           - jnp.exp(jnp.sum(lp[2:3] * lp[3:4], axis=1, keepdims=True)) + lam_init)
    gain = g_ref[...] * (1.0 - lam_init)
    feature = lax.broadcasted_iota(jnp.int32, (LANES, tq), 0)
    first_map = feature < DIFF_QK_DIM
    s_bufs = (sa_ref, sb_ref)
    p_bufs = (pa_ref, pb_ref)

    n_groups = n_chunks // ATTN_GROUP
    q_sub = tq // n_groups
    assert q_sub == LANES

    def step(k, par, do_a=True, do_b=True, do_c=True, do_d=True):
        s_w, s_r = s_bufs[par], s_bufs[1 - par]
        p_w, p_r = p_bufs[1 - par], p_bufs[par]
        if do_d:
            h_d = (k - 3) // tiles
            q_base = ((k - 3) % tiles) * tq
        if do_a:
            h_a = k // tiles
            qt = qt_ref[0, h_a, k % tiles]
            zero = jnp.zeros_like(qt)
            qw = (jnp.where(first_map, qt, zero), jnp.where(first_map, zero, qt))
        if do_b:
            m8 = (mx_ref[1 - par, 0], mx_ref[1 - par, 1])
        if do_c:
            h_c = (k - 2) // tiles
            c16 = (coef_ref[par, 0], coef_ref[par, 1])

        def chunk(c, carry):
            mx = [carry[0], carry[1]]
            ls = [carry[2], carry[3]]
            acc = carry[4]
            for part in range(kc // sub):
                rows = pl.ds(pl.multiple_of(c * kc + part * sub, sub), sub)
                if do_a:
                    keys = k_ref[0, h_a, rows, :]
                    for m in range(2):
                        st = _dot(keys, qw[m])
                        s_w[m, rows, :] = st
                        mx[m] = jnp.maximum(mx[m], jnp.max(st.reshape(sub // 8, 8, tq), axis=0))
                if do_b:
                    for m in range(2):
                        p = jnp.exp2(s_r[m, rows, :].reshape(sub // 8, 8, tq) - m8[m][None])
                        ls[m] = ls[m] + jnp.sum(p, axis=0)
                        p_w[m, rows, :] = p.reshape(sub, tq).astype(BF16)
            if do_c:
                rows = pl.ds(pl.multiple_of(c * kc, kc), kc)
                a = (p_r[0, rows, :].reshape(kc // 16, 16, tq) * c16[0][None]
                     - p_r[1, rows, :].reshape(kc // 16, 16, tq) * c16[1][None])
                part_o = _dot(vt_ref[0, h_c, c], a.reshape(kc, tq))
                acc = part_o if acc is None else acc + part_o
            return mx[0], mx[1], ls[0], ls[1], acc

        lowest = jnp.full((8, tq), -jnp.inf, F32)
        nothing = jnp.zeros((8, tq), F32)
        def group(gi, carry):
            if do_d:
                o = ot_ref[1 - par, gi].T
                ms = jnp.mean(o * o, axis=1, keepdims=True)
                q_start = q_base + gi * q_sub
                q_rows = pl.ds(q_start if isinstance(q_start, int) else pl.multiple_of(q_start, q_sub), q_sub)
                o_ref[0, h_d, q_rows, :] = (o * lax.rsqrt(ms + LN_EPS) * gain).astype(BF16)
            stats = carry + (None,)
            for c in range(ATTN_GROUP):
                stats = chunk(gi * ATTN_GROUP + c, stats)
            if do_c:
                for half in range(n_groups):
                    ot_ref[par, half] += stats[4][:, half * q_sub:(half + 1) * q_sub]
            return stats[:4]

        if do_c:
            ot_ref[par] = jnp.zeros(ot_ref.shape[1:], F32)
        mx0, mx1, ls0, ls1 = lax.fori_loop(0, n_groups, group, (lowest, lowest, nothing, nothing))
        if do_a:
            mx_ref[par, 0] = jnp.broadcast_to(jnp.max(mx0, axis=0, keepdims=True), (8, tq))
            mx_ref[par, 1] = jnp.broadcast_to(jnp.max(mx1, axis=0, keepdims=True), (8, tq))
        if do_b:
            l0 = jnp.sum(ls0, axis=0, keepdims=True)
            l1 = jnp.sum(ls1, axis=0, keepdims=True)
            coef_ref[1 - par, 0] = jnp.broadcast_to(1.0 / l0, (16, tq)).astype(BF16)
            coef_ref[1 - par, 1] = jnp.broadcast_to(lam / l1, (16, tq)).astype(BF16)

    def step_pair(j, carry):
        step(2 * j, 0)
        step(2 * j + 1, 1)
        return carry

    step(0, 0, do_b=False, do_c=False, do_d=False)
    step(1, 1, do_c=False, do_d=False)
    step(2, 0, do_d=False)
    step(3, 1)
    lax.fori_loop(2, n_units // 2, step_pair, 0)
    step(n_units, 0, do_a=False)
    step(n_units + 1, 1, do_a=False, do_b=False)
    step(n_units + 2, 0, do_a=False, do_b=False, do_c=False)


def _diff_attention(qt, k, vt, lam_params, g, lam_init):
    b, _, s, _ = k.shape
    whole = lambda a: pl.BlockSpec((1,) + a.shape[1:], lambda bi: (bi,) + (0,) * (a.ndim - 1))
    return pl.pallas_call(
        functools.partial(_attn_kernel, lam_init=lam_init),
        grid=(b,),
        in_specs=[
            pl.BlockSpec((4, DIFF_QK_DIM), lambda bi: (0, 0)),
            pl.BlockSpec((1, DIFF_V_DIM), lambda bi: (0, 0)),
            whole(qt), whole(k), whole(vt),
        ],
        out_specs=whole(k),
        out_shape=jax.ShapeDtypeStruct((b, DIFF_HEADS, s, LANES), BF16),
        scratch_shapes=[
            pltpu.VMEM((2, s, ATTN_TQ), F32), pltpu.VMEM((2, s, ATTN_TQ), F32),
            pltpu.VMEM((2, s, ATTN_TQ), BF16), pltpu.VMEM((2, s, ATTN_TQ), BF16),
            pltpu.VMEM((2, 2, 8, ATTN_TQ), F32),
            pltpu.VMEM((2, 2, 16, ATTN_TQ), BF16),
            pltpu.VMEM((2, ATTN_TQ // LANES, DIFF_V_DIM, LANES), F32),
        ],
        compiler_params=_params(1),
        name="diff_attn",
    )(lam_params, g, qt, k, vt)


FOURIER_PACK = 2


def _fourier_kernel(u_ref, cc_ref, sc_ref, fe_ref, fo_ref, w_ref, o_ref, ab_ref, *, scale):
    s = u_ref.shape[2]
    half = s // 2
    w = FOURIER_WIDTH
    for e in range(FOURIER_PACK):
        lanes = slice(e * w, (e + 1) * w)
        for parity in range(2):
            u = jnp.concatenate([u_ref[lh, e, pl.ds(parity, half, stride=2), :] for lh in range(2)],
                                axis=1).astype(BF16)
            ab_ref[parity, 0:half, lanes] = _dot(u, cc_ref[...]).astype(BF16)
            ab_ref[parity, half:s, lanes] = _dot(u, sc_ref[...]).astype(BF16)
    even = _dot(fe_ref[...].astype(BF16), ab_ref[0])
    odd = _dot(fo_ref[...].astype(BF16), ab_ref[1])
    for rows, z in ((slice(0, half), (even + odd) * scale), (slice(half, s), (even - odd) * scale)):
        for e in range(FOURIER_PACK):
            y = _dot(z[:, e * w:(e + 1) * w].astype(BF16), w_ref[...])
            o_ref[e, rows, :] = y.astype(BF16)


def _fourier(u, cc_bd, sc_bd, f_even, f_odd, w_bd):
    _, b, s, _ = u.shape
    pack = FOURIER_PACK
    scale = 1.0 / math.sqrt(s * FOURIER_GROUP_DIM)
    const2 = lambda bi: (0, 0)
    table = pl.BlockSpec((s // 2, s), const2, pipeline_mode=pl.Buffered(1))
    return pl.pallas_call(
        functools.partial(_fourier_kernel, scale=scale),
        grid=(b // pack,),
        in_specs=[
            pl.BlockSpec((2, pack, s, LANES), lambda bi: (0, bi, 0, 0)),
            pl.BlockSpec((FOURIER_WIDTH, FOURIER_WIDTH), const2),
            pl.BlockSpec((FOURIER_WIDTH, FOURIER_WIDTH), const2),
            table,
            table,
            pl.BlockSpec((FOURIER_WIDTH, FOURIER_WIDTH), const2),
        ],
        out_specs=pl.BlockSpec((pack, s, FOURIER_WIDTH), lambda bi: (bi, 0, 0)),
        out_shape=jax.ShapeDtypeStruct((b, s, FOURIER_WIDTH), BF16),
        scratch_shapes=[pltpu.VMEM((2, s, pack * FOURIER_WIDTH), BF16)],
        compiler_params=_params(1),
        name="fourier",
    )(u, cc_bd, sc_bd, f_even, f_odd, w_bd)


def _split3(g):
    hi = g.astype(BF16)
    r1 = g - hi.astype(F32)
    mid = r1.astype(BF16)
    lo = (r1 - mid.astype(F32)).astype(BF16)
    return hi, mid, lo


def _gla_kernel(qk_ref, v_ref, r_ref, z_ref, w2_ref, b2_ref, g_ref, o_ref,
                la_ref, vt_ref, acc_ref, qt_ref, kt_ref, a_ref, kv_ref, dec_ref, st_ref):
    s = v_ref.shape[1]
    c = GLA_CHUNK
    pair = 2 * c
    n_pairs = s // pair
    n_chunks = s // c
    kc = GLA_K_COLS

    logit = _dot(z_ref[0].astype(BF16), w2_ref[...]) + b2_ref[...]
    la_ref[...] = (jnp.minimum(logit, 0.0) - jnp.log1p(jnp.exp(-jnp.abs(logit)))) * (1.0 / GLA_GATE_TAU)
    vt_ref[...] = v_ref[0].T.astype(BF16)

    row_i = lax.broadcasted_iota(jnp.int32, (pair, pair), 0)
    col_i = lax.broadcasted_iota(jnp.int32, (pair, pair), 1)
    same_chunk = (row_i // c) == (col_i // c)
    cum_mats = ((same_chunk & (col_i <= row_i)).astype(BF16), (same_chunk & (col_i >= row_i)).astype(BF16))
    r_hk = lax.broadcasted_iota(jnp.int32, (GLA_HEADS * c, kc), 0) // c
    l_hk = lax.broadcasted_iota(jnp.int32, (GLA_HEADS * c, kc), 1) // GLA_K_DIM
    blk_k = r_hk == l_hk
    r_hv = lax.broadcasted_iota(jnp.int32, (GLA_HEADS * c, GLA_WIDTH), 0) // c
    l_hv = lax.broadcasted_iota(jnp.int32, (GLA_HEADS * c, GLA_WIDTH), 1) // GLA_V_DIM
    blk_v = r_hv == l_hv
    qi = lax.broadcasted_iota(jnp.int32, (c, GLA_HEADS * c), 0)
    kj = lax.broadcasted_iota(jnp.int32, (c, GLA_HEADS * c), 1) % c
    q_scale = GLA_K_DIM ** -0.5

    causal = (kj <= qi, kj >= qi)

    def pair_rows(p):
        start = p * pair
        return pl.ds(start if isinstance(start, int) else pl.multiple_of(start, pair), pair)

    def decayed_qk(p):
        rows = pair_rows(p)
        q2 = qk_ref[0, rows, 0:kc] * q_scale
        k2 = qk_ref[0, rows, kc:2 * kc]
        for d in range(2):
            hi, mid, lo = _split3(la_ref[rows, d * kc:(d + 1) * kc])
            c3 = _dot(cum_mats[d], jnp.concatenate([hi, mid, lo], axis=1))
            bcum = c3[:, 0:kc] + c3[:, kc:2 * kc] + c3[:, 2 * kc:3 * kc]
            eb = jnp.exp(bcum)
            qt_ref[d, rows, :] = (q2 * eb).astype(BF16)
            kt_ref[d, rows, :] = k2 * jnp.exp(-bcum)
            for ci in range(2):
                edge = ci * c if d == 1 else (ci + 1) * c - 1
                dec_ref[d, 2 * p + ci] = jnp.broadcast_to(eb[edge:edge + 1], (8, kc))

    def intra_scores(p):
        for ci in range(2):
            start = p * pair + ci * c
            rows = pl.ds(start if isinstance(start, int) else pl.multiple_of(start, c), c)
            for d in range(2):
                kt_bd = jnp.where(blk_k, jnp.concatenate([kt_ref[d, rows, :]] * GLA_HEADS, axis=0), 0.0)
                a = _dot_nt(qt_ref[d, rows, :], kt_bd.astype(BF16))
                a_ref[2 * p + ci, d * c:(d + 1) * c, :] = jnp.where(causal[d], a, 0.0).astype(BF16)

    def intra_out_and_kv(p):
        rows2 = pair_rows(p)
        v2 = v_ref[0, rows2, :]
        vt2 = vt_ref[:, rows2]
        kt2 = jnp.concatenate([kt_ref[0, rows2, :], kt_ref[1, rows2, :]], axis=1)
        for ci in range(2):
            sl = slice(ci * c, (ci + 1) * c)
            start = p * pair + ci * c
            rows = pl.ds(start if isinstance(start, int) else pl.multiple_of(start, c), c)
            v_bd = jnp.where(blk_v, jnp.concatenate([v2[sl]] * GLA_HEADS, axis=0), 0.0).astype(BF16)
            o_both = _dot(a_ref[2 * p + ci], v_bd)
            acc_ref[rows, :] = o_both[0:c] + o_both[c:2 * c]
            gap = jnp.zeros((c, 2 * kc), F32)
            kt_pair = jnp.concatenate([kt2[sl], gap] if ci == 0 else [gap, kt2[sl]], axis=0).astype(BF16)
            kv = _dot(vt2, kt_pair)
            kv_ref[0, 2 * p + ci] = jnp.where(blk_k, kv[:, 0:kc], 0.0)
            kv_ref[1, 2 * p + ci] = jnp.where(blk_k, kv[:, kc:2 * kc], 0.0)

    decayed_qk(0)
    intra_scores(0)
    decayed_qk(1)

    def intra_step(p, carry):
        intra_out_and_kv(p)
        intra_scores(p + 1)
        decayed_qk(p + 2)
        return carry

    lax.fori_loop(0, n_pairs - 2, intra_step, 0)
    intra_out_and_kv(n_pairs - 2)
    intra_scores(n_pairs - 1)
    intra_out_and_kv(n_pairs - 1)

    st_ref[...] = jnp.zeros_like(st_ref)

    def scan_step(i, carry):
        for d in range(2):
            n = i if d == 0 else n_chunks - 1 - i
            rows = pl.ds(pl.multiple_of(n * c, c), c)
            st = st_ref[d]
            acc_ref[rows, :] += _dot_nt(qt_ref[d, rows, :], st.astype(BF16))
            st = (st + kv_ref[d, n]).reshape(GLA_WIDTH // 8, 8, kc) * dec_ref[d, n][None]
            st_ref[d] = st.reshape(GLA_WIDTH, kc)
        return carry

    lax.fori_loop(0, n_chunks, scan_step, 0, unroll=2)

    o = acc_ref[...]
    o2 = o * o
    o2_hi = o2.astype(BF16)
    o2_lo = (o2 - o2_hi.astype(F32)).astype(BF16)
    gi = lax.broadcasted_iota(jnp.int32, (GLA_WIDTH, GLA_WIDTH), 0) // GLA_V_DIM
    gj = lax.broadcasted_iota(jnp.int32, (GLA_WIDTH, GLA_WIDTH), 1) // GLA_V_DIM
    ones_bd = (gi == gj).astype(BF16)
    ms = (_dot(o2_hi, ones_bd) + _dot(o2_lo, ones_bd)) * (1.0 / GLA_V_DIM)
    r = r_ref[0]
    gate = r * (1.0 / (1.0 + jnp.exp(-r)))
    o_ref[0] = (o * lax.rsqrt(ms + LN_EPS) * g_ref[...] * gate).astype(BF16)


def _gla(gqk, gv, gr, gz, w2_bd, b2_cat, g_tiled):
    b, s, _ = gv.shape
    per_b = lambda bi: (bi, 0, 0)
    const = lambda bi: (0, 0)
    return pl.pallas_call(
        _gla_kernel,
        grid=(b,),
        in_specs=[
            pl.BlockSpec((1, s, 2 * GLA_K_COLS), per_b),
            pl.BlockSpec((1, s, GLA_WIDTH), per_b),
            pl.BlockSpec((1, s, GLA_WIDTH), per_b),
            pl.BlockSpec((1, s, 2 * GLA_GATE_RANK), per_b),
            pl.BlockSpec((2 * GLA_GATE_RANK, 2 * GLA_K_COLS), const),
            pl.BlockSpec((1, 2 * GLA_K_COLS), const),
            pl.BlockSpec((1, GLA_WIDTH), const),
        ],
        out_specs=pl.BlockSpec((1, s, GLA_WIDTH), per_b),
        out_shape=jax.ShapeDtypeStruct((b, s, GLA_WIDTH), BF16),
        scratch_shapes=[
            pltpu.VMEM((s, 2 * GLA_K_COLS), F32),
            pltpu.VMEM((GLA_WIDTH, s), BF16),
            pltpu.VMEM((s, GLA_WIDTH), F32),
            pltpu.VMEM((2, s, GLA_K_COLS), BF16),
            pltpu.VMEM((2, s, GLA_K_COLS), F32),
            pltpu.VMEM((s // GLA_CHUNK, 2 * GLA_CHUNK, GLA_WIDTH), BF16),
            pltpu.VMEM((2, s // GLA_CHUNK, GLA_WIDTH, GLA_K_COLS), F32),
            pltpu.VMEM((2, s // GLA_CHUNK, 8, GLA_K_COLS), F32),
            pltpu.VMEM((2, GLA_WIDTH, GLA_K_COLS), F32),
        ],
        compiler_params=_params(1),
        name="gla",
    )(gqk, gv, gr, gz, w2_bd, b2_cat, g_tiled)


def _layer_norm(y, g, b):
    mu = jnp.mean(y, axis=1, keepdims=True)
    d = y - mu
    var = jnp.mean(d * d, axis=1, keepdims=True)
    return d * lax.rsqrt(var + LN_EPS) * g + b


ROW_SUB = 512


def _outproj_kernel(od_ref, of_ref, og_ref, x_ref, w_ref, g_ref, b_ref, o_ref):
    w = w_ref[...].astype(BF16)
    for r in range(x_ref.shape[0] // ROW_SUB):
        rows = slice(r * ROW_SUB, (r + 1) * ROW_SUB)
        mixed = jnp.concatenate([od_ref[0, hd, rows, :] for hd in range(DIFF_HEADS)]
                                + [of_ref[rows, :], og_ref[rows, :]], axis=1)
        m = _dot(mixed, w)
        o_ref[rows, :] = _layer_norm(DEEPNORM_ALPHA * x_ref[rows, :] + m, g_ref[...], b_ref[...])


def _out_proj(o_diff, o_four, o_gla, x2d, w_all, layer, g, b, tm=1024):
    m = x2d.shape[0]
    seq = o_diff.shape[2]
    n_seq_tiles = seq // tm
    row = lambda i: (i, 0)
    const = lambda i: (0, 0)
    return pl.pallas_call(
        _outproj_kernel,
        grid=(m // tm,),
        in_specs=[
            pl.BlockSpec((1, DIFF_HEADS, tm, LANES), lambda i: (i // n_seq_tiles, 0, i % n_seq_tiles, 0)),
            pl.BlockSpec((tm, FOURIER_WIDTH), row),
            pl.BlockSpec((tm, GLA_WIDTH), row),
            pl.BlockSpec((tm, D_MODEL), row),
            _layer_weight(w_all, layer),
            pl.BlockSpec((1, D_MODEL), const),
            pl.BlockSpec((1, D_MODEL), const),
        ],
        out_specs=pl.BlockSpec((tm, D_MODEL), row),
        out_shape=jax.ShapeDtypeStruct((m, D_MODEL), F32),
        compiler_params=_params(1),
        name="out_proj",
    )(o_diff, o_four, o_gla, x2d, w_all, g, b)


FFN_CHUNK = 256


def _ffn_kernel(x_ref, wg_ref, wu_ref, wd_ref, g_ref, b_ref, o_ref, acc_ref):
    x = x_ref[...]
    xb = x.astype(BF16)
    for ci in range(FFN_HIDDEN // FFN_CHUNK):
        cols = slice(ci * FFN_CHUNK, (ci + 1) * FFN_CHUNK)
        hg = _dot(xb, wg_ref[:, cols].astype(BF16))
        hu = _dot(xb, wu_ref[:, cols].astype(BF16))
        act = (hg * (1.0 / (1.0 + jnp.exp(-hg))) * hu).astype(BF16)
        part = _dot(act, wd_ref[cols, :].astype(BF16))
        if ci == 0:
            acc_ref[...] = part
        else:
            acc_ref[...] += part
    o_ref[...] = _layer_norm(DEEPNORM_ALPHA * x + acc_ref[...], g_ref[...], b_ref[...])


def _ffn(x2d, wg_all, wu_all, wd_all, layer, g, b, tm=512):
    m = x2d.shape[0]
    row = lambda i: (i, 0)
    const = lambda i: (0, 0)
    return pl.pallas_call(
        _ffn_kernel,
        grid=(m // tm,),
        in_specs=[
            pl.BlockSpec((tm, D_MODEL), row),
            _layer_weight(wg_all, layer),
            _layer_weight(wu_all, layer),
            _layer_weight(wd_all, layer),
            pl.BlockSpec((1, D_MODEL), const),
            pl.BlockSpec((1, D_MODEL), const),
        ],
        out_specs=pl.BlockSpec((tm, D_MODEL), row),
        out_shape=jax.ShapeDtypeStruct((m, D_MODEL), F32),
        scratch_shapes=[pltpu.VMEM((tm, D_MODEL), F32)],
        compiler_params=_params(1),
        name="ffn",
    )(x2d, wg_all, wu_all, wd_all, g, b)


def _rope_tables(seq):
    half = DIFF_QK_DIM // 2
    pos = jnp.arange(seq, dtype=F32)
    inv_freq = ROPE_THETA ** (-jnp.arange(0, DIFF_QK_DIM, 2, dtype=F32) / DIFF_QK_DIM)
    ang = pos[:, None] * inv_freq[None, :]
    cos, sin = jnp.cos(ang), jnp.sin(ang)
    reps = LANES // DIFF_QK_DIM
    cos128 = jnp.tile(jnp.concatenate([cos, cos], axis=1), (1, reps))
    sin128 = jnp.tile(jnp.concatenate([-sin, sin], axis=1), (1, reps))
    assert cos128.shape == (seq, LANES) and half * 2 == DIFF_QK_DIM
    return cos128, sin128, cos128.T, sin128.T


def _dft_tables(seq):
    half = seq // 2
    j = np.arange(half, dtype=np.int64)[:, None]
    pos = np.arange(half, dtype=np.int64)[None, :]
    ang_even = 2.0 * np.pi * ((j * pos) % half).astype(np.float64) / half
    ang_odd = 2.0 * np.pi * ((j * (2 * pos + 1)) % seq).astype(np.float64) / seq
    f_even = np.concatenate([np.cos(ang_even), -np.sin(ang_even)], axis=1)
    f_odd = np.concatenate([np.cos(ang_odd), -np.sin(ang_odd)], axis=1)
    c = np.arange(FOURIER_GROUP_DIM, dtype=np.int64)
    ang_c = 2.0 * np.pi * ((c[:, None] * c[None, :]) % FOURIER_GROUP_DIM).astype(np.float64) / FOURIER_GROUP_DIM
    eye = np.eye(FOURIER_GROUPS)
    cc_bd = np.kron(eye, np.cos(ang_c))
    sc_bd = np.kron(eye, np.sin(ang_c))
    return (jnp.asarray(f_even, dtype=F32), jnp.asarray(f_odd, dtype=F32),
            jnp.asarray(cc_bd, dtype=F32).astype(BF16), jnp.asarray(sc_bd, dtype=F32).astype(BF16))


def _block_diag(blocks):
    n = len(blocks)
    rows = []
    for i, blk in enumerate(blocks):
        rows.append(jnp.concatenate(
            [blk if j == i else jnp.zeros((blk.shape[0], blocks[j].shape[1]), blk.dtype) for j in range(n)],
            axis=1))
    return jnp.concatenate(rows, axis=0)


def kernel(x, w_in, diff_lambda, diff_norm_g, fourier_w, gla_gate_w2, gla_gate_b2, gla_norm_g, w_out,
           ln1_g, ln1_b, ffn_w_gate, ffn_w_up, ffn_w_down, ln2_g, ln2_b):
    b, s, d = x.shape
    m = b * s
    rope = _rope_tables(s)
    f_even, f_odd, cc_bd, sc_bd = _dft_tables(s)
    x2d = x.reshape(m, d)
    for l in range(DEPTH):
        lam_init = 0.8 - 0.6 * math.exp(-0.3 * l)
        wqv_t = jnp.concatenate([w_in[l, :, OFF_DQ:OFF_DK], w_in[l, :, OFF_DV:OFF_FU]], axis=1).T.astype(BF16)
        w_rest = jnp.concatenate([w_in[l, :, OFF_DK:OFF_DV], w_in[l, :, OFF_FU:]], axis=1).astype(BF16)
        qt, k, vt, fu, gqk, gv, gr, gz = _in_proj(x2d, wqv_t, w_rest, rope, s)
        o_diff = _diff_attention(qt, k, vt, diff_lambda[l], diff_norm_g[l].reshape(1, -1), lam_init)
        w_four = _block_diag([fourier_w[l, g] for g in range(FOURIER_GROUPS)]).astype(BF16)
        o_four = _fourier(fu.reshape(-1, b, s, LANES), cc_bd, sc_bd, f_even, f_odd, w_four)
        w2_bd = _block_diag([gla_gate_w2[l, 0], gla_gate_w2[l, 1]]).astype(BF16)
        b2_cat = gla_gate_b2[l].reshape(1, -1)
        g_gla = jnp.tile(gla_norm_g[l], GLA_HEADS).reshape(1, -1)
        o_gla = _gla(gqk.reshape(b, s, -1), gv.reshape(b, s, -1), gr.reshape(b, s, -1),
                     gz.reshape(b, s, -1), w2_bd, b2_cat, g_gla)
        x2d = _out_proj(o_diff, o_four.reshape(m, -1), o_gla.reshape(m, -1), x2d,
                        w_out, l, ln1_g[l].reshape(1, -1), ln1_b[l].reshape(1, -1))
        x2d = _ffn(x2d, ffn_w_gate, ffn_w_up, ffn_w_down, l,
                   ln2_g[l].reshape(1, -1), ln2_b[l].reshape(1, -1))
    return x2d.reshape(b, s, d)
```

```python
import functools
import math

import jax
import jax.numpy as jnp
import numpy as np
from jax import lax
from jax.experimental import pallas as pl
from jax.experimental.pallas import tpu as pltpu

D_MODEL = 1024
DEPTH = 2
DIFF_HEADS = 4
DIFF_QK_DIM = 64
DIFF_V_DIM = 128
DIFF_WIDTH = 512
DIFF_QK_COLS = 512
ROPE_THETA = 10000.0
FOURIER_GROUPS = 4
FOURIER_GROUP_DIM = 64
FOURIER_WIDTH = 256
GLA_HEADS = 4
GLA_V_DIM = 64
GLA_K_DIM = 32
GLA_WIDTH = 256
GLA_K_COLS = 128
GLA_GATE_RANK = 16
GLA_GATE_TAU = 16.0
GLA_CHUNK = 64
IN_WIDTH = 2592
FFN_HIDDEN = 2816
DEEPNORM_ALPHA = (2 * DEPTH) ** 0.25
LN_EPS = 1e-5

OFF_DQ, OFF_DK, OFF_DV, OFF_FU = 0, 512, 1024, 1536
OFF_GQ, OFF_GV, OFF_GR, OFF_GZ = 1792, 2048, 2304, 2560

LANES = 128
VMEM_LIMIT = 56 * 1024 * 1024

BF16 = jnp.bfloat16
F32 = jnp.float32


def _dot(a, b):
    return jnp.dot(a, b, preferred_element_type=F32)


def _dot_nt(a, b):
    return lax.dot_general(a, b, (((1,), (1,)), ((), ())), preferred_element_type=F32)


def _params(n_grid_dims):
    return pltpu.CompilerParams(
        dimension_semantics=("arbitrary",) * n_grid_dims,
        vmem_limit_bytes=VMEM_LIMIT)


def _rope_slab(t, cos, sin_signed, first_half):
    swapped = jnp.where(first_half, pltpu.roll(t, 96, 1), pltpu.roll(t, 32, 1))
    return t * cos + swapped * sin_signed


ATTN_TQ = 256
ATTN_KC = 256


def _rest_col(col):
    assert col >= OFF_DK and not OFF_DV <= col < OFF_FU
    return col - OFF_DK if col < OFF_DV else col - OFF_DK - DIFF_WIDTH


def _inproj_kernel(x_ref, wqv_ref, w_ref, cos_ref, sin_ref, cost_ref, sint_ref,
                   qt_ref, k_ref, vt_ref, fu_ref, gqk_ref, gv_ref, gr_ref, gz_ref):
    xb = x_ref[...].astype(BF16)
    tm = xb.shape[0]
    cos = cos_ref[...]
    sin_signed = sin_ref[...]
    lane = lax.broadcasted_iota(jnp.int32, cos.shape, 1)
    first_half = (lane % DIFF_QK_DIM) < (DIFF_QK_DIM // 2)
    qk_scale = DIFF_QK_DIM ** -0.5 * math.log2(math.e)
    hqv_t = _dot_nt(wqv_ref[...], xb)
    cos_t = cost_ref[...]
    sin_t = sint_ref[...]
    half = DIFF_QK_DIM // 2
    wcols = lambda lo, hi: w_ref[:, _rest_col(lo):_rest_col(hi - 1) + 1]
    hk = _dot(xb, wcols(OFF_DK, OFF_DV))
    for hd in range(DIFF_HEADS):
        qt = hqv_t[hd * LANES:(hd + 1) * LANES]
        swapped = jnp.concatenate(
            [qt[half:2 * half], qt[0:half], qt[3 * half:4 * half], qt[2 * half:3 * half]], axis=0)
        qt = ((qt * cos_t + swapped * sin_t) * qk_scale).astype(BF16)
        vt = hqv_t[DIFF_QK_COLS + hd * LANES:DIFF_QK_COLS + (hd + 1) * LANES].astype(BF16)
        for j in range(tm // ATTN_TQ):
            qt_ref[0, hd, j] = qt[:, j * ATTN_TQ:(j + 1) * ATTN_TQ]
        for j in range(tm // ATTN_KC):
            vt_ref[0, hd, j] = vt[:, j * ATTN_KC:(j + 1) * ATTN_KC]
        slab = slice(hd * LANES, (hd + 1) * LANES)
        k_ref[0, hd] = _rope_slab(hk[:, slab], cos, sin_signed, first_half).astype(BF16)
    hfu = _dot(xb, wcols(OFF_FU, OFF_GQ))
    for lane_half in range(FOURIER_WIDTH // LANES):
        fu_ref[lane_half] = hfu[:, lane_half * LANES:(lane_half + 1) * LANES]
    gqk_ref[...] = _dot(xb, wcols(OFF_GQ, OFF_GV))
    gv_ref[...] = _dot(xb, wcols(OFF_GV, OFF_GR))
    gr_ref[...] = _dot(xb, wcols(OFF_GR, OFF_GZ))
    gz_ref[...] = _dot(xb, wcols(OFF_GZ, IN_WIDTH))


def _layer_weight(w_all, layer):
    return pl.BlockSpec((None,) + w_all.shape[1:], lambda i: (layer,) + (0,) * (w_all.ndim - 1),
                        pipeline_mode=pl.Buffered(1))


def _in_proj(x2d, wqv_t, w_rest, rope, seq, tm=512):
    m = x2d.shape[0]
    batch = m // seq
    n_seq_tiles = seq // tm
    cos128, sin128, cos_t, sin_t = rope
    row = lambda i: (i, 0)
    const = lambda i: (0, 0)
    tab = lambda i: (i % n_seq_tiles, 0)
    tab_t = lambda i: (0, i % n_seq_tiles)
    tiled = lambda t: (jax.ShapeDtypeStruct((batch, DIFF_HEADS, seq // t, LANES, t), BF16),
                       pl.BlockSpec((1, DIFF_HEADS, tm // t, LANES, t),
                                    lambda i: (i // n_seq_tiles, 0, i % n_seq_tiles, 0, 0)))
    qt_shape, qt_spec = tiled(ATTN_TQ)
    vt_shape, vt_spec = tiled(ATTN_KC)
    k_shape = jax.ShapeDtypeStruct((batch, DIFF_HEADS, seq, LANES), BF16)
    k_spec = pl.BlockSpec((1, DIFF_HEADS, tm, LANES), lambda i: (i // n_seq_tiles, 0, i % n_seq_tiles, 0))
    fu_shape = jax.ShapeDtypeStruct((FOURIER_WIDTH // LANES, m, LANES), F32)
    fu_spec = pl.BlockSpec((FOURIER_WIDTH // LANES, tm, LANES), lambda i: (0, i, 0))
    flat_shapes = (
        jax.ShapeDtypeStruct((m, 2 * GLA_K_COLS), F32),
        jax.ShapeDtypeStruct((m, GLA_WIDTH), F32),
        jax.ShapeDtypeStruct((m, GLA_WIDTH), F32),
        jax.ShapeDtypeStruct((m, 2 * GLA_GATE_RANK), F32),
    )
    return pl.pallas_call(
        _inproj_kernel,
        grid=(m // tm,),
        in_specs=[
            pl.BlockSpec((tm, D_MODEL), row),
            pl.BlockSpec((DIFF_QK_COLS + DIFF_WIDTH, D_MODEL), const),
            pl.BlockSpec(w_rest.shape, const),
            pl.BlockSpec((tm, LANES), tab),
            pl.BlockSpec((tm, LANES), tab),
            pl.BlockSpec((LANES, tm), tab_t),
            pl.BlockSpec((LANES, tm), tab_t),
        ],
        out_specs=(qt_spec, k_spec, vt_spec, fu_spec)
        + tuple(pl.BlockSpec((tm, s.shape[1]), row) for s in flat_shapes),
        out_shape=(qt_shape, k_shape, vt_shape, fu_shape) + flat_shapes,
        compiler_params=_params(1),
        name="in_proj",
    )(x2d, wqv_t, w_rest, cos128, sin128, cos_t, sin_t)


ATTN_SUB = 128
ATTN_GROUP = 4


def _attn_kernel(lam_ref, g_ref, qt_ref, k_ref, vt_ref, o_ref,
                 sa_ref, sb_ref, pa_ref, pb_ref, mx_ref, coef_ref, ot_ref, *, lam_init):
    tq, kc, sub = ATTN_TQ, ATTN_KC, ATTN_SUB
    seq = k_ref.shape[2]
    tiles = seq // tq
    n_units = DIFF_HEADS * tiles
    n_chunks = seq // kc
    lp = lam_ref[...]
    lam = (jnp.exp(jnp.sum(lp[0:1] * lp[1:2], axis=1, keepdims=True))
           - jnp.exp(jnp.sum(lp[2:3] * lp[3:4], axis=1, keepdims=True)) + lam_init)
    gain = g_ref[...] * (1.0 - lam_init)
    feature = lax.broadcasted_iota(jnp.int32, (LANES, tq), 0)
    first_map = feature < DIFF_QK_DIM
    s_bufs = (sa_ref, sb_ref)
    p_bufs = (pa_ref, pb_ref)

    n_groups = n_chunks // ATTN_GROUP
    slabs_per_trip = (tq // LANES) // n_groups
    assert slabs_per_trip * n_groups * LANES == tq

    def step(k, par, do_a=True, do_b=True, do_c=True, do_d=True):
        s_w, s_r = s_bufs[par], s_bufs[1 - par]
        p_w, p_r = p_bufs[1 - par], p_bufs[par]
        if do_d:
            h_d = (k - 3) // tiles
            q_base = ((k - 3) % tiles) * tq
        if do_a:
            h_a = k // tiles
            qt = qt_ref[0, h_a, k % tiles]
            zero = jnp.zeros_like(qt)
            qw = (jnp.where(first_map, qt, zero), jnp.where(first_map, zero, qt))
        if do_b:
            m8 = (mx_ref[1 - par, 0], mx_ref[1 - par, 1])
        if do_c:
            h_c = (k - 2) // tiles
            c16 = (coef_ref[par, 0], coef_ref[par, 1])

        def chunk(c, carry):
            mx = [carry[0], carry[1]]
            ls = [carry[2], carry[3]]
            acc = carry[4]
            for part in range(kc // sub):
                rows = pl.ds(pl.multiple_of(c * kc + part * sub, sub), sub)
                if do_a:
                    keys = k_ref[0, h_a, rows, :]
                    for m in range(2):
                        st = _dot(keys, qw[m])
                        s_w[m, rows, :] = st
                        mx[m] = jnp.maximum(mx[m], jnp.max(st.reshape(sub // 8, 8, tq), axis=0))
                if do_b:
                    for m in range(2):
                        p = jnp.exp2(s_r[m, rows, :].reshape(sub // 8, 8, tq) - m8[m][None])
                        ls[m] = ls[m] + jnp.sum(p, axis=0)
                        p_w[m, rows, :] = p.reshape(sub, tq).astype(BF16)
            if do_c:
                rows = pl.ds(pl.multiple_of(c * kc, kc), kc)
                a = (p_r[0, rows, :].reshape(kc // 16, 16, tq) * c16[0][None]
                     - p_r[1, rows, :].reshape(kc // 16, 16, tq) * c16[1][None])
                part_o = _dot(vt_ref[0, h_c, c], a.reshape(kc, tq))
                acc = part_o if acc is None else acc + part_o
            return mx[0], mx[1], ls[0], ls[1], acc

        lowest = jnp.full((8, tq), -jnp.inf, F32)
        nothing = jnp.zeros((8, tq), F32)
        def group(gi, carry):
            if do_d:
                for slab in range(slabs_per_trip):
                    si = gi * slabs_per_trip + slab
                    o = ot_ref[1 - par, si].T
                    ms = jnp.mean(o * o, axis=1, keepdims=True)
                    q_start = q_base + si * LANES
                    q_rows = pl.ds(q_start if isinstance(q_start, int) else pl.multiple_of(q_start, LANES),
                                   LANES)
                    o_ref[0, h_d, q_rows, :] = (o * lax.rsqrt(ms + LN_EPS) * gain).astype(BF16)
            stats = carry + (None,)
            for c in range(ATTN_GROUP):
                stats = chunk(gi * ATTN_GROUP + c, stats)
            if do_c:
                for si in range(tq // LANES):
                    ot_ref[par, si] += stats[4][:, si * LANES:(si + 1) * LANES]
            return stats[:4]

        if do_c:
            ot_ref[par] = jnp.zeros(ot_ref.shape[1:], F32)
        mx0, mx1, ls0, ls1 = lax.fori_loop(0, n_groups, group, (lowest, lowest, nothing, nothing))
        if do_a:
            mx_ref[par, 0] = jnp.broadcast_to(jnp.max(mx0, axis=0, keepdims=True), (8, tq))
            mx_ref[par, 1] = jnp.broadcast_to(jnp.max(mx1, axis=0, keepdims=True), (8, tq))
        if do_b:
            l0 = jnp.sum(ls0, axis=0, keepdims=True)
            l1 = jnp.sum(ls1, axis=0, keepdims=True)
            coef_ref[1 - par, 0] = jnp.broadcast_to(1.0 / l0, (16, tq)).astype(BF16)
            coef_ref[1 - par, 1] = jnp.broadcast_to(lam / l1, (16, tq)).astype(BF16)

    def step_pair(j, carry):
        step(2 * j, 0)
        step(2 * j + 1, 1)
        return carry

    step(0, 0, do_b=False, do_c=False, do_d=False)
    step(1, 1, do_c=False, do_d=False)
    step(2, 0, do_d=False)
    step(3, 1)
    lax.fori_loop(2, n_units // 2, step_pair, 0)
    step(n_units, 0, do_a=False)
    step(n_units + 1, 1, do_a=False, do_b=False)
    step(n_units + 2, 0, do_a=False, do_b=False, do_c=False)


def _diff_attention(qt, k, vt, lam_params, g, lam_init):
    b, _, s, _ = k.shape
    whole = lambda a: pl.BlockSpec((1,) + a.shape[1:], lambda bi: (bi,) + (0,) * (a.ndim - 1))
    return pl.pallas_call(
        functools.partial(_attn_kernel, lam_init=lam_init),
        grid=(b,),
        in_specs=[
            pl.BlockSpec((4, DIFF_QK_DIM), lambda bi: (0, 0)),
            pl.BlockSpec((1, DIFF_V_DIM), lambda bi: (0, 0)),
            whole(qt), whole(k), whole(vt),
        ],
        out_specs=whole(k),
        out_shape=jax.ShapeDtypeStruct((b, DIFF_HEADS, s, LANES), BF16),
        scratch_shapes=[
            pltpu.VMEM((2, s, ATTN_TQ), F32), pltpu.VMEM((2, s, ATTN_TQ), F32),
            pltpu.VMEM((2, s, ATTN_TQ), BF16), pltpu.VMEM((2, s, ATTN_TQ), BF16),
            pltpu.VMEM((2, 2, 8, ATTN_TQ), F32),
            pltpu.VMEM((2, 2, 16, ATTN_TQ), BF16),
            pltpu.VMEM((2, ATTN_TQ // LANES, DIFF_V_DIM, LANES), F32),
        ],
        compiler_params=_params(1),
        name="diff_attn",
    )(lam_params, g, qt, k, vt)


FOURIER_PACK = 2


def _fourier_kernel(u_ref, cc_ref, sc_ref, fe_ref, fo_ref, w_ref, o_ref, ab_ref, *, scale):
    s = u_ref.shape[2]
    half = s // 2
    w = FOURIER_WIDTH
    for e in range(FOURIER_PACK):
        lanes = slice(e * w, (e + 1) * w)
        for parity in range(2):
            u = jnp.concatenate([u_ref[lh, e, pl.ds(parity, half, stride=2), :] for lh in range(2)],
                                axis=1).astype(BF16)
            ab_ref[parity, 0:half, lanes] = _dot(u, cc_ref[...]).astype(BF16)
            ab_ref[parity, half:s, lanes] = _dot(u, sc_ref[...]).astype(BF16)
    even = _dot(fe_ref[...].astype(BF16), ab_ref[0])
    odd = _dot(fo_ref[...].astype(BF16), ab_ref[1])
    for rows, z in ((slice(0, half), (even + odd) * scale), (slice(half, s), (even - odd) * scale)):
        for e in range(FOURIER_PACK):
            y = _dot(z[:, e * w:(e + 1) * w].astype(BF16), w_ref[...])
            o_ref[e, rows, :] = y.astype(BF16)


def _fourier(u, cc_bd, sc_bd, f_even, f_odd, w_bd):
    _, b, s, _ = u.shape
    pack = FOURIER_PACK
    scale = 1.0 / math.sqrt(s * FOURIER_GROUP_DIM)
    const2 = lambda bi: (0, 0)
    table = pl.BlockSpec((s // 2, s), const2, pipeline_mode=pl.Buffered(1))
    return pl.pallas_call(
        functools.partial(_fourier_kernel, scale=scale),
        grid=(b // pack,),
        in_specs=[
            pl.BlockSpec((2, pack, s, LANES), lambda bi: (0, bi, 0, 0)),
            pl.BlockSpec((FOURIER_WIDTH, FOURIER_WIDTH), const2),
            pl.BlockSpec((FOURIER_WIDTH, FOURIER_WIDTH), const2),
            table,
            table,
            pl.BlockSpec((FOURIER_WIDTH, FOURIER_WIDTH), const2),
        ],
        out_specs=pl.BlockSpec((pack, s, FOURIER_WIDTH), lambda bi: (bi, 0, 0)),
        out_shape=jax.ShapeDtypeStruct((b, s, FOURIER_WIDTH), BF16),
        scratch_shapes=[pltpu.VMEM((2, s, pack * FOURIER_WIDTH), BF16)],
        compiler_params=_params(1),
        name="fourier",
    )(u, cc_bd, sc_bd, f_even, f_odd, w_bd)


def _split3(g):
    hi = g.astype(BF16)
    r1 = g - hi.astype(F32)
    mid = r1.astype(BF16)
    lo = (r1 - mid.astype(F32)).astype(BF16)
    return hi, mid, lo


def _gla_kernel(qk_ref, v_ref, r_ref, z_ref, w2_ref, b2_ref, g_ref, o_ref,
                la_ref, vt_ref, acc_ref, qt_ref, kt_ref, a_ref, kv_ref, dec_ref, st_ref):
    s = v_ref.shape[1]
    c = GLA_CHUNK
    pair = 2 * c
    n_pairs = s // pair
    n_chunks = s // c
    kc = GLA_K_COLS

    logit = _dot(z_ref[0].astype(BF16), w2_ref[...]) + b2_ref[...]
    la_ref[...] = (jnp.minimum(logit, 0.0) - jnp.log(1.0 + jnp.exp(-jnp.abs(logit)))) * (1.0 / GLA_GATE_TAU)
    vt_ref[...] = v_ref[0].T.astype(BF16)

    row_i = lax.broadcasted_iota(jnp.int32, (pair, pair), 0)
    col_i = lax.broadcasted_iota(jnp.int32, (pair, pair), 1)
    same_chunk = (row_i // c) == (col_i // c)
    cum_mats = ((same_chunk & (col_i <= row_i)).astype(BF16), (same_chunk & (col_i >= row_i)).astype(BF16))
    r_hk = lax.broadcasted_iota(jnp.int32, (GLA_HEADS * c, kc), 0) // c
    l_hk = lax.broadcasted_iota(jnp.int32, (GLA_HEADS * c, kc), 1) // GLA_K_DIM
    blk_k = r_hk == l_hk
    r_hv = lax.broadcasted_iota(jnp.int32, (GLA_HEADS * c, GLA_WIDTH), 0) // c
    l_hv = lax.broadcasted_iota(jnp.int32, (GLA_HEADS * c, GLA_WIDTH), 1) // GLA_V_DIM
    blk_v = r_hv == l_hv
    qi = lax.broadcasted_iota(jnp.int32, (c, GLA_HEADS * c), 0)
    kj = lax.broadcasted_iota(jnp.int32, (c, GLA_HEADS * c), 1) % c
    q_scale = GLA_K_DIM ** -0.5

    causal = (kj <= qi, kj >= qi)

    def pair_rows(p):
        start = p * pair
        return pl.ds(start if isinstance(start, int) else pl.multiple_of(start, pair), pair)

    def decayed_qk(p):
        rows = pair_rows(p)
        q2 = qk_ref[0, rows, 0:kc] * q_scale
        k2 = qk_ref[0, rows, kc:2 * kc]
        for d in range(2):
            hi, mid, lo = _split3(la_ref[rows, d * kc:(d + 1) * kc])
            c3 = _dot(cum_mats[d], jnp.concatenate([hi, mid, lo], axis=1))
            bcum = c3[:, 0:kc] + c3[:, kc:2 * kc] + c3[:, 2 * kc:3 * kc]
            eb = jnp.exp(bcum)
            qt_ref[d, rows, :] = (q2 * eb).astype(BF16)
            kt_ref[d, rows, :] = k2 * jnp.exp(-bcum)
            for ci in range(2):
                edge = ci * c if d == 1 else (ci + 1) * c - 1
                dec_ref[d, 2 * p + ci] = jnp.broadcast_to(eb[edge:edge + 1], (8, kc))

    def intra_scores(p):
        for ci in range(2):
            start = p * pair + ci * c
            rows = pl.ds(start if isinstance(start, int) else pl.multiple_of(start, c), c)
            for d in range(2):
                kt_bd = jnp.where(blk_k, jnp.concatenate([kt_ref[d, rows, :]] * GLA_HEADS, axis=0), 0.0)
                a = _dot_nt(qt_ref[d, rows, :], kt_bd.astype(BF16))
                a_ref[2 * p + ci, d * c:(d + 1) * c, :] = jnp.where(causal[d], a, 0.0).astype(BF16)

    def intra_out_and_kv(p):
        rows2 = pair_rows(p)
        v2 = v_ref[0, rows2, :]
        vt2 = vt_ref[:, rows2]
        kt2 = jnp.concatenate([kt_ref[0, rows2, :], kt_ref[1, rows2, :]], axis=1)
        for ci in range(2):
            sl = slice(ci * c, (ci + 1) * c)
            start = p * pair + ci * c
            rows = pl.ds(start if isinstance(start, int) else pl.multiple_of(start, c), c)
            v_bd = jnp.where(blk_v, jnp.concatenate([v2[sl]] * GLA_HEADS, axis=0), 0.0).astype(BF16)
            o_both = _dot(a_ref[2 * p + ci], v_bd)
            acc_ref[rows, :] = o_both[0:c] + o_both[c:2 * c]
            gap = jnp.zeros((c, 2 * kc), F32)
            kt_pair = jnp.concatenate([kt2[sl], gap] if ci == 0 else [gap, kt2[sl]], axis=0).astype(BF16)
            kv = _dot(vt2, kt_pair)
            kv_ref[0, 2 * p + ci] = jnp.where(blk_k, kv[:, 0:kc], 0.0)
            kv_ref[1, 2 * p + ci] = jnp.where(blk_k, kv[:, kc:2 * kc], 0.0)

    decayed_qk(0)
    intra_scores(0)
    decayed_qk(1)

    def intra_step(p, carry):
        intra_out_and_kv(p)
        intra_scores(p + 1)
        decayed_qk(p + 2)
        return carry

    lax.fori_loop(0, n_pairs - 2, intra_step, 0, unroll=2)
    intra_out_and_kv(n_pairs - 2)
    intra_scores(n_pairs - 1)
    intra_out_and_kv(n_pairs - 1)

    st_ref[...] = jnp.zeros_like(st_ref)

    def scan_step(i, carry):
        for d in range(2):
            n = i if d == 0 else n_chunks - 1 - i
            rows = pl.ds(pl.multiple_of(n * c, c), c)
            st = st_ref[d]
            acc_ref[rows, :] += _dot_nt(qt_ref[d, rows, :], st.astype(BF16))
            st = (st + kv_ref[d, n]).reshape(GLA_WIDTH // 8, 8, kc) * dec_ref[d, n][None]
            st_ref[d] = st.reshape(GLA_WIDTH, kc)
        return carry

    lax.fori_loop(0, n_chunks, scan_step, 0, unroll=8)

    o = acc_ref[...]
    o2 = o * o
    o2_hi = o2.astype(BF16)
    o2_lo = (o2 - o2_hi.astype(F32)).astype(BF16)
    gi = lax.broadcasted_iota(jnp.int32, (GLA_WIDTH, GLA_WIDTH), 0) // GLA_V_DIM
    gj = lax.broadcasted_iota(jnp.int32, (GLA_WIDTH, GLA_WIDTH), 1) // GLA_V_DIM
    ones_bd = (gi == gj).astype(BF16)
    ms = (_dot(o2_hi, ones_bd) + _dot(o2_lo, ones_bd)) * (1.0 / GLA_V_DIM)
    r = r_ref[0]
    gate = r * (1.0 / (1.0 + jnp.exp(-r)))
    o_ref[0] = (o * lax.rsqrt(ms + LN_EPS) * g_ref[...] * gate).astype(BF16)


def _gla(gqk, gv, gr, gz, w2_bd, b2_cat, g_tiled):
    b, s, _ = gv.shape
    per_b = lambda bi: (bi, 0, 0)
    const = lambda bi: (0, 0)
    return pl.pallas_call(
        _gla_kernel,
        grid=(b,),
        in_specs=[
            pl.BlockSpec((1, s, 2 * GLA_K_COLS), per_b),
            pl.BlockSpec((1, s, GLA_WIDTH), per_b),
            pl.BlockSpec((1, s, GLA_WIDTH), per_b),
            pl.BlockSpec((1, s, 2 * GLA_GATE_RANK), per_b),
            pl.BlockSpec((2 * GLA_GATE_RANK, 2 * GLA_K_COLS), const),
            pl.BlockSpec((1, 2 * GLA_K_COLS), const),
            pl.BlockSpec((1, GLA_WIDTH), const),
        ],
        out_specs=pl.BlockSpec((1, s, GLA_WIDTH), per_b),
        out_shape=jax.ShapeDtypeStruct((b, s, GLA_WIDTH), BF16),
        scratch_shapes=[
            pltpu.VMEM((s, 2 * GLA_K_COLS), F32),
            pltpu.VMEM((GLA_WIDTH, s), BF16),
            pltpu.VMEM((s, GLA_WIDTH), F32),
            pltpu.VMEM((2, s, GLA_K_COLS), BF16),
            pltpu.VMEM((2, s, GLA_K_COLS), F32),
            pltpu.VMEM((s // GLA_CHUNK, 2 * GLA_CHUNK, GLA_WIDTH), BF16),
            pltpu.VMEM((2, s // GLA_CHUNK, GLA_WIDTH, GLA_K_COLS), F32),
            pltpu.VMEM((2, s // GLA_CHUNK, 8, GLA_K_COLS), F32),
            pltpu.VMEM((2, GLA_WIDTH, GLA_K_COLS), F32),
        ],
        compiler_params=_params(1),
        name="gla",
    )(gqk, gv, gr, gz, w2_bd, b2_cat, g_tiled)


def _layer_norm(y, g, b):
    mu = jnp.mean(y, axis=1, keepdims=True)
    d = y - mu
    var = jnp.mean(d * d, axis=1, keepdims=True)
    return d * lax.rsqrt(var + LN_EPS) * g + b


ROW_SUB = 512


def _outproj_kernel(od_ref, of_ref, og_ref, x_ref, w_ref, g_ref, b_ref, o_ref):
    w = w_ref[...].astype(BF16)
    for r in range(x_ref.shape[0] // ROW_SUB):
        rows = slice(r * ROW_SUB, (r + 1) * ROW_SUB)
        mixed = jnp.concatenate([od_ref[0, hd, rows, :] for hd in range(DIFF_HEADS)]
                                + [of_ref[rows, :], og_ref[rows, :]], axis=1)
        m = _dot(mixed, w)
        o_ref[rows, :] = _layer_norm(DEEPNORM_ALPHA * x_ref[rows, :] + m, g_ref[...], b_ref[...])


def _out_proj(o_diff, o_four, o_gla, x2d, w_all, layer, g, b, tm=1024):
    m = x2d.shape[0]
    seq = o_diff.shape[2]
    n_seq_tiles = seq // tm
    row = lambda i: (i, 0)
    const = lambda i: (0, 0)
    return pl.pallas_call(
        _outproj_kernel,
        grid=(m // tm,),
        in_specs=[
            pl.BlockSpec((1, DIFF_HEADS, tm, LANES), lambda i: (i // n_seq_tiles, 0, i % n_seq_tiles, 0)),
            pl.BlockSpec((tm, FOURIER_WIDTH), row),
            pl.BlockSpec((tm, GLA_WIDTH), row),
            pl.BlockSpec((tm, D_MODEL), row),
            _layer_weight(w_all, layer),
            pl.BlockSpec((1, D_MODEL), const),
            pl.BlockSpec((1, D_MODEL), const),
        ],
        out_specs=pl.BlockSpec((tm, D_MODEL), row),
        out_shape=jax.ShapeDtypeStruct((m, D_MODEL), F32),
        compiler_params=_params(1),
        name="out_proj",
    )(o_diff, o_four, o_gla, x2d, w_all, g, b)


FFN_CHUNK = 256


def _ffn_kernel(x_ref, wg_ref, wu_ref, wd_ref, g_ref, b_ref, o_ref, acc_ref):
    x = x_ref[...]
    xb = x.astype(BF16)
    for ci in range(FFN_HIDDEN // FFN_CHUNK):
        cols = slice(ci * FFN_CHUNK, (ci + 1) * FFN_CHUNK)
        hg = _dot(xb, wg_ref[:, cols].astype(BF16))
        hu = _dot(xb, wu_ref[:, cols].astype(BF16))
        act = (hg * (1.0 / (1.0 + jnp.exp(-hg))) * hu).astype(BF16)
        part = _dot(act, wd_ref[cols, :].astype(BF16))
        if ci == 0:
            acc_ref[...] = part
        else:
            acc_ref[...] += part
    o_ref[...] = _layer_norm(DEEPNORM_ALPHA * x + acc_ref[...], g_ref[...], b_ref[...])


def _ffn(x2d, wg_all, wu_all, wd_all, layer, g, b, tm=512):
    m = x2d.shape[0]
    row = lambda i: (i, 0)
    const = lambda i: (0, 0)
    return pl.pallas_call(
        _ffn_kernel,
        grid=(m // tm,),
        in_specs=[
            pl.BlockSpec((tm, D_MODEL), row),
            _layer_weight(wg_all, layer),
            _layer_weight(wu_all, layer),
            _layer_weight(wd_all, layer),
            pl.BlockSpec((1, D_MODEL), const),
            pl.BlockSpec((1, D_MODEL), const),
        ],
        out_specs=pl.BlockSpec((tm, D_MODEL), row),
        out_shape=jax.ShapeDtypeStruct((m, D_MODEL), F32),
        scratch_shapes=[pltpu.VMEM((tm, D_MODEL), F32)],
        compiler_params=_params(1),
        name="ffn",
    )(x2d, wg_all, wu_all, wd_all, g, b)


def _rope_tables(seq):
    half = DIFF_QK_DIM // 2
    pos = jnp.arange(seq, dtype=F32)
    inv_freq = ROPE_THETA ** (-jnp.arange(0, DIFF_QK_DIM, 2, dtype=F32) / DIFF_QK_DIM)
    ang = pos[:, None] * inv_freq[None, :]
    cos, sin = jnp.cos(ang), jnp.sin(ang)
    reps = LANES // DIFF_QK_DIM
    cos128 = jnp.tile(jnp.concatenate([cos, cos], axis=1), (1, reps))
    sin128 = jnp.tile(jnp.concatenate([-sin, sin], axis=1), (1, reps))
    assert cos128.shape == (seq, LANES) and half * 2 == DIFF_QK_DIM
    return cos128, sin128, cos128.T, sin128.T


def _dft_tables(seq):
    half = seq // 2
    j = np.arange(half, dtype=np.int64)[:, None]
    pos = np.arange(half, dtype=np.int64)[None, :]
    ang_even = 2.0 * np.pi * ((j * pos) % half).astype(np.float64) / half
    ang_odd = 2.0 * np.pi * ((j * (2 * pos + 1)) % seq).astype(np.float64) / seq
    f_even = np.concatenate([np.cos(ang_even), -np.sin(ang_even)], axis=1)
    f_odd = np.concatenate([np.cos(ang_odd), -np.sin(ang_odd)], axis=1)
    c = np.arange(FOURIER_GROUP_DIM, dtype=np.int64)
    ang_c = 2.0 * np.pi * ((c[:, None] * c[None, :]) % FOURIER_GROUP_DIM).astype(np.float64) / FOURIER_GROUP_DIM
    eye = np.eye(FOURIER_GROUPS)
    cc_bd = np.kron(eye, np.cos(ang_c))
    sc_bd = np.kron(eye, np.sin(ang_c))
    return (jnp.asarray(f_even, dtype=F32), jnp.asarray(f_odd, dtype=F32),
            jnp.asarray(cc_bd, dtype=F32).astype(BF16), jnp.asarray(sc_bd, dtype=F32).astype(BF16))


def _block_diag(blocks):
    n = len(blocks)
    rows = []
    for i, blk in enumerate(blocks):
        rows.append(jnp.concatenate(
            [blk if j == i else jnp.zeros((blk.shape[0], blocks[j].shape[1]), blk.dtype) for j in range(n)],
            axis=1))
    return jnp.concatenate(rows, axis=0)


def kernel(x, w_in, diff_lambda, diff_norm_g, fourier_w, gla_gate_w2, gla_gate_b2, gla_norm_g, w_out,
           ln1_g, ln1_b, ffn_w_gate, ffn_w_up, ffn_w_down, ln2_g, ln2_b):
    b, s, d = x.shape
    m = b * s
    rope = _rope_tables(s)
    f_even, f_odd, cc_bd, sc_bd = _dft_tables(s)
    x2d = x.reshape(m, d)
    for l in range(DEPTH):
        lam_init = 0.8 - 0.6 * math.exp(-0.3 * l)
        wqv_t = jnp.concatenate([w_in[l, :, OFF_DQ:OFF_DK], w_in[l, :, OFF_DV:OFF_FU]], axis=1).T.astype(BF16)
        w_rest = jnp.concatenate([w_in[l, :, OFF_DK:OFF_DV], w_in[l, :, OFF_FU:]], axis=1).astype(BF16)
        qt, k, vt, fu, gqk, gv, gr, gz = _in_proj(x2d, wqv_t, w_rest, rope, s)
        o_diff = _diff_attention(qt, k, vt, diff_lambda[l], diff_norm_g[l].reshape(1, -1), lam_init)
        w_four = _block_diag([fourier_w[l, g] for g in range(FOURIER_GROUPS)]).astype(BF16)
        o_four = _fourier(fu.reshape(-1, b, s, LANES), cc_bd, sc_bd, f_even, f_odd, w_four)
        w2_bd = _block_diag([gla_gate_w2[l, 0], gla_gate_w2[l, 1]]).astype(BF16)
        b2_cat = gla_gate_b2[l].reshape(1, -1)
        g_gla = jnp.tile(gla_norm_g[l], GLA_HEADS).reshape(1, -1)
        o_gla = _gla(gqk.reshape(b, s, -1), gv.reshape(b, s, -1), gr.reshape(b, s, -1),
                     gz.reshape(b, s, -1), w2_bd, b2_cat, g_gla)
        x2d = _out_proj(o_diff, o_four.reshape(m, -1), o_gla.reshape(m, -1), x2d,
                        w_out, l, ln1_g[l].reshape(1, -1), ln1_b[l].reshape(1, -1))
        x2d = _ffn(x2d, ffn_w_gate, ffn_w_up, ffn_w_down, l,
                   ln2_g[l].reshape(1, -1), ln2_b[l].reshape(1, -1))
    return x2d.reshape(b, s, d)
```

```python
import functools
import math

import jax
import jax.numpy as jnp
import numpy as np
from jax import lax
from jax.experimental import pallas as pl
from jax.experimental.pallas import tpu as pltpu

D_MODEL = 1024
DEPTH = 2
DIFF_HEADS = 4
DIFF_QK_DIM = 64
DIFF_V_DIM = 128
DIFF_WIDTH = 512
DIFF_QK_COLS = 512
ROPE_THETA = 10000.0
FOURIER_GROUPS = 4
FOURIER_GROUP_DIM = 64
FOURIER_WIDTH = 256
GLA_HEADS = 4
GLA_V_DIM = 64
GLA_K_DIM = 32
GLA_WIDTH = 256
GLA_K_COLS = 128
GLA_GATE_RANK = 16
GLA_GATE_TAU = 16.0
GLA_CHUNK = 64
IN_WIDTH = 2592
FFN_HIDDEN = 2816
DEEPNORM_ALPHA = (2 * DEPTH) ** 0.25
LN_EPS = 1e-5

OFF_DQ, OFF_DK, OFF_DV, OFF_FU = 0, 512, 1024, 1536
OFF_GQ, OFF_GV, OFF_GR, OFF_GZ = 1792, 2048, 2304, 2560

LANES = 128
VMEM_LIMIT = 56 * 1024 * 1024

BF16 = jnp.bfloat16
F32 = jnp.float32


def _dot(a, b):
    return jnp.dot(a, b, preferred_element_type=F32)


def _dot_nt(a, b):
    return lax.dot_general(a, b, (((1,), (1,)), ((), ())), preferred_element_type=F32)


def _params(n_grid_dims):
    return pltpu.CompilerParams(
        dimension_semantics=("arbitrary",) * n_grid_dims,
        vmem_limit_bytes=VMEM_LIMIT)


def _rope_slab(t, cos, sin_signed, first_half):
    swapped = jnp.where(first_half, pltpu.roll(t, 96, 1), pltpu.roll(t, 32, 1))
    return t * cos + swapped * sin_signed


ATTN_TQ = 256
ATTN_KC = 256


def _rest_col(col):
    assert col >= OFF_DK and not OFF_DV <= col < OFF_FU
    return col - OFF_DK if col < OFF_DV else col - OFF_DK - DIFF_WIDTH


def _inproj_kernel(x_ref, wqv_ref, w_ref, cos_ref, sin_ref, cost_ref, sint_ref,
                   qt_ref, k_ref, vt_ref, fu_ref, gqk_ref, gv_ref, gr_ref, gz_ref):
    xb = x_ref[...].astype(BF16)
    tm = xb.shape[0]
    cos = cos_ref[...]
    sin_signed = sin_ref[...]
    lane = lax.broadcasted_iota(jnp.int32, cos.shape, 1)
    first_half = (lane % DIFF_QK_DIM) < (DIFF_QK_DIM // 2)
    qk_scale = DIFF_QK_DIM ** -0.5 * math.log2(math.e)
    hqv_t = _dot_nt(wqv_ref[...], xb)
    cos_t = cost_ref[...]
    sin_t = sint_ref[...]
    half = DIFF_QK_DIM // 2
    wcols = lambda lo, hi: w_ref[:, _rest_col(lo):_rest_col(hi - 1) + 1]
    hk = _dot(xb, wcols(OFF_DK, OFF_DV))
    for hd in range(DIFF_HEADS):
        qt = hqv_t[hd * LANES:(hd + 1) * LANES]
        swapped = jnp.concatenate(
            [qt[half:2 * half], qt[0:half], qt[3 * half:4 * half], qt[2 * half:3 * half]], axis=0)
        qt = ((qt * cos_t + swapped * sin_t) * qk_scale).astype(BF16)
        vt = hqv_t[DIFF_QK_COLS + hd * LANES:DIFF_QK_COLS + (hd + 1) * LANES].astype(BF16)
        for j in range(tm // ATTN_TQ):
            qt_ref[0, hd, j] = qt[:, j * ATTN_TQ:(j + 1) * ATTN_TQ]
        for j in range(tm // ATTN_KC):
            vt_ref[0, hd, j] = vt[:, j * ATTN_KC:(j + 1) * ATTN_KC]
        slab = slice(hd * LANES, (hd + 1) * LANES)
        k_ref[0, hd] = _rope_slab(hk[:, slab], cos, sin_signed, first_half).astype(BF16)
    hfu = _dot(xb, wcols(OFF_FU, OFF_GQ))
    for lane_half in range(FOURIER_WIDTH // LANES):
        fu_ref[lane_half] = hfu[:, lane_half * LANES:(lane_half + 1) * LANES]
    gqk_ref[...] = _dot(xb, wcols(OFF_GQ, OFF_GV))
    gv_ref[...] = _dot(xb, wcols(OFF_GV, OFF_GR))
    gr_ref[...] = _dot(xb, wcols(OFF_GR, OFF_GZ))
    gz_ref[...] = _dot(xb, wcols(OFF_GZ, IN_WIDTH))


def _layer_weight(w_all, layer):
    return pl.BlockSpec((None,) + w_all.shape[1:], lambda i: (layer,) + (0,) * (w_all.ndim - 1),
                        pipeline_mode=pl.Buffered(1))


def _in_proj(x2d, wqv_t, w_rest, rope, seq, tm=512):
    m = x2d.shape[0]
    batch = m // seq
    n_seq_tiles = seq // tm
    cos128, sin128, cos_t, sin_t = rope
    row = lambda i: (i, 0)
    const = lambda i: (0, 0)
    tab = lambda i: (i % n_seq_tiles, 0)
    tab_t = lambda i: (0, i % n_seq_tiles)
    tiled = lambda t: (jax.ShapeDtypeStruct((batch, DIFF_HEADS, seq // t, LANES, t), BF16),
                       pl.BlockSpec((1, DIFF_HEADS, tm // t, LANES, t),
                                    lambda i: (i // n_seq_tiles, 0, i % n_seq_tiles, 0, 0)))
    qt_shape, qt_spec = tiled(ATTN_TQ)
    vt_shape, vt_spec = tiled(ATTN_KC)
    k_shape = jax.ShapeDtypeStruct((batch, DIFF_HEADS, seq, LANES), BF16)
    k_spec = pl.BlockSpec((1, DIFF_HEADS, tm, LANES), lambda i: (i // n_seq_tiles, 0, i % n_seq_tiles, 0))
    fu_shape = jax.ShapeDtypeStruct((FOURIER_WIDTH // LANES, m, LANES), F32)
    fu_spec = pl.BlockSpec((FOURIER_WIDTH // LANES, tm, LANES), lambda i: (0, i, 0))
    flat_shapes = (
        jax.ShapeDtypeStruct((m, 2 * GLA_K_COLS), F32),
        jax.ShapeDtypeStruct((m, GLA_WIDTH), F32),
        jax.ShapeDtypeStruct((m, GLA_WIDTH), F32),
        jax.ShapeDtypeStruct((m, 2 * GLA_GATE_RANK), F32),
    )
    return pl.pallas_call(
        _inproj_kernel,
        grid=(m // tm,),
        in_specs=[
            pl.BlockSpec((tm, D_MODEL), row),
            pl.BlockSpec((DIFF_QK_COLS + DIFF_WIDTH, D_MODEL), const),
            pl.BlockSpec(w_rest.shape, const),
            pl.BlockSpec((tm, LANES), tab),
            pl.BlockSpec((tm, LANES), tab),
            pl.BlockSpec((LANES, tm), tab_t),
            pl.BlockSpec((LANES, tm), tab_t),
        ],
        out_specs=(qt_spec, k_spec, vt_spec, fu_spec)
        + tuple(pl.BlockSpec((tm, s.shape[1]), row) for s in flat_shapes),
        out_shape=(qt_shape, k_shape, vt_shape, fu_shape) + flat_shapes,
        compiler_params=_params(1),
        name="in_proj",
    )(x2d, wqv_t, w_rest, cos128, sin128, cos_t, sin_t)


ATTN_SUB = 128
ATTN_GROUP = 4


def _attn_kernel(lam_ref, g_ref, qt_ref, k_ref, vt_ref, o_ref,
                 sa_ref, sb_ref, pa_ref, pb_ref, mx_ref, coef_ref, ot_ref, *, lam_init):
    tq, kc, sub = ATTN_TQ, ATTN_KC, ATTN_SUB
    seq = k_ref.shape[2]
    tiles = seq // tq
    n_units = DIFF_HEADS * tiles
    n_chunks = seq // kc
    lp = lam_ref[...]
    lam = (jnp.exp(jnp.sum(lp[0:1] * lp[1:2], axis=1, keepdims=True))
           - jnp.exp(jnp.sum(lp[2:3] * lp[3:4], axis=1, keepdims=True)) + lam_init)
    gain = g_ref[...] * (1.0 - lam_init)
    feature = lax.broadcasted_iota(jnp.int32, (LANES, tq), 0)
    first_map = feature < DIFF_QK_DIM
    s_bufs = (sa_ref, sb_ref)
    p_bufs = (pa_ref, pb_ref)

    n_groups = n_chunks // ATTN_GROUP
    slabs_per_trip = (tq // LANES) // n_groups
    assert slabs_per_trip * n_groups * LANES == tq

    def step(k, par, do_a=True, do_b=True, do_c=True, do_d=True):
        s_w, s_r = s_bufs[par], s_bufs[1 - par]
        p_w, p_r = p_bufs[1 - par], p_bufs[par]
        if do_d:
            h_d = (k - 3) // tiles
            q_base = ((k - 3) % tiles) * tq
        if do_a:
            h_a = k // tiles
            qt = qt_ref[0, h_a, k % tiles]
            zero = jnp.zeros_like(qt)
            qw = (jnp.where(first_map, qt, zero), jnp.where(first_map, zero, qt))
        if do_b:
            m8 = (mx_ref[1 - par, 0], mx_ref[1 - par, 1])
        if do_c:
            h_c = (k - 2) // tiles
            c16 = (coef_ref[par, 0], coef_ref[par, 1])

        def chunk(c, carry):
            mx = [carry[0], carry[1]]
            ls = [carry[2], carry[3]]
            acc = carry[4]
            if do_c:
                rows = pl.ds(pl.multiple_of(c * kc, kc), kc)
                a = (p_r[0, rows, :].reshape(kc // 16, 16, tq) * c16[0][None]
                     - p_r[1, rows, :].reshape(kc // 16, 16, tq) * c16[1][None])
                part_o = _dot(vt_ref[0, h_c, c], a.reshape(kc, tq))
                acc = part_o if acc is None else acc + part_o
            for part in range(kc // sub):
                rows = pl.ds(pl.multiple_of(c * kc + part * sub, sub), sub)
                if do_b:
                    for m in range(2):
                        p = jnp.exp2(s_r[m, rows, :].reshape(sub // 8, 8, tq) - m8[m][None])
                        ls[m] = ls[m] + jnp.sum(p, axis=0)
                        p_w[m, rows, :] = p.reshape(sub, tq).astype(BF16)
                if do_a:
                    keys = k_ref[0, h_a, rows, :]
                    for m in range(2):
                        st = _dot(keys, qw[m])
                        s_w[m, rows, :] = st
                        mx[m] = jnp.maximum(mx[m], jnp.max(st.reshape(sub // 8, 8, tq), axis=0))
            return mx[0], mx[1], ls[0], ls[1], acc

        lowest = jnp.full((8, tq), -jnp.inf, F32)
        nothing = jnp.zeros((8, tq), F32)
        def group(gi, carry):
            if do_d:
                for slab in range(slabs_per_trip):
                    si = gi * slabs_per_trip + slab
                    o = ot_ref[1 - par, si].T
                    ms = jnp.mean(o * o, axis=1, keepdims=True)
                    q_start = q_base + si * LANES
                    q_rows = pl.ds(q_start if isinstance(q_start, int) else pl.multiple_of(q_start, LANES),
                                   LANES)
                    o_ref[0, h_d, q_rows, :] = (o * lax.rsqrt(ms + LN_EPS) * gain).astype(BF16)
            stats = carry + (None,)
            for c in range(ATTN_GROUP):
                stats = chunk(gi * ATTN_GROUP + c, stats)
            if do_c:
                for si in range(tq // LANES):
                    ot_ref[par, si] += stats[4][:, si * LANES:(si + 1) * LANES]
            return stats[:4]

        if do_c:
            ot_ref[par] = jnp.zeros(ot_ref.shape[1:], F32)
        mx0, mx1, ls0, ls1 = lax.fori_loop(0, n_groups, group, (lowest, lowest, nothing, nothing))
        if do_a:
            mx_ref[par, 0] = jnp.broadcast_to(jnp.max(mx0, axis=0, keepdims=True), (8, tq))
            mx_ref[par, 1] = jnp.broadcast_to(jnp.max(mx1, axis=0, keepdims=True), (8, tq))
        if do_b:
            l0 = jnp.sum(ls0, axis=0, keepdims=True)
            l1 = jnp.sum(ls1, axis=0, keepdims=True)
            coef_ref[1 - par, 0] = jnp.broadcast_to(1.0 / l0, (16, tq)).astype(BF16)
            coef_ref[1 - par, 1] = jnp.broadcast_to(lam / l1, (16, tq)).astype(BF16)

    def step_pair(j, carry):
        step(2 * j, 0)
        step(2 * j + 1, 1)
        return carry

    step(0, 0, do_b=False, do_c=False, do_d=False)
    step(1, 1, do_c=False, do_d=False)
    step(2, 0, do_d=False)
    step(3, 1)
    lax.fori_loop(2, n_units // 2, step_pair, 0)
    step(n_units, 0, do_a=False)
    step(n_units + 1, 1, do_a=False, do_b=False)
    step(n_units + 2, 0, do_a=False, do_b=False, do_c=False)


def _diff_attention(qt, k, vt, lam_params, g, lam_init):
    b, _, s, _ = k.shape
    whole = lambda a: pl.BlockSpec((1,) + a.shape[1:], lambda bi: (bi,) + (0,) * (a.ndim - 1))
    return pl.pallas_call(
        functools.partial(_attn_kernel, lam_init=lam_init),
        grid=(b,),
        in_specs=[
            pl.BlockSpec((4, DIFF_QK_DIM), lambda bi: (0, 0)),
            pl.BlockSpec((1, DIFF_V_DIM), lambda bi: (0, 0)),
            whole(qt), whole(k), whole(vt),
        ],
        out_specs=whole(k),
        out_shape=jax.ShapeDtypeStruct((b, DIFF_HEADS, s, LANES), BF16),
        scratch_shapes=[
            pltpu.VMEM((2, s, ATTN_TQ), F32), pltpu.VMEM((2, s, ATTN_TQ), F32),
            pltpu.VMEM((2, s, ATTN_TQ), BF16), pltpu.VMEM((2, s, ATTN_TQ), BF16),
            pltpu.VMEM((2, 2, 8, ATTN_TQ), F32),
            pltpu.VMEM((2, 2, 16, ATTN_TQ), BF16),
            pltpu.VMEM((2, ATTN_TQ // LANES, DIFF_V_DIM, LANES), F32),
        ],
        compiler_params=_params(1),
        name="diff_attn",
    )(lam_params, g, qt, k, vt)


FOURIER_PACK = 2


def _fourier_kernel(u_ref, cc_ref, sc_ref, fe_ref, fo_ref, w_ref, o_ref, ab_ref, *, scale):
    s = u_ref.shape[2]
    half = s // 2
    w = FOURIER_WIDTH
    for e in range(FOURIER_PACK):
        lanes = slice(e * w, (e + 1) * w)
        for parity in range(2):
            u = jnp.concatenate([u_ref[lh, e, pl.ds(parity, half, stride=2), :] for lh in range(2)],
                                axis=1).astype(BF16)
            ab_ref[parity, 0:half, lanes] = _dot(u, cc_ref[...]).astype(BF16)
            ab_ref[parity, half:s, lanes] = _dot(u, sc_ref[...]).astype(BF16)
    even = _dot(fe_ref[...].astype(BF16), ab_ref[0])
    odd = _dot(fo_ref[...].astype(BF16), ab_ref[1])
    for rows, z in ((slice(0, half), (even + odd) * scale), (slice(half, s), (even - odd) * scale)):
        for e in range(FOURIER_PACK):
            y = _dot(z[:, e * w:(e + 1) * w].astype(BF16), w_ref[...])
            o_ref[e, rows, :] = y.astype(BF16)


def _fourier(u, cc_bd, sc_bd, f_even, f_odd, w_bd):
    _, b, s, _ = u.shape
    pack = FOURIER_PACK
    scale = 1.0 / math.sqrt(s * FOURIER_GROUP_DIM)
    const2 = lambda bi: (0, 0)
    table = pl.BlockSpec((s // 2, s), const2, pipeline_mode=pl.Buffered(1))
    return pl.pallas_call(
        functools.partial(_fourier_kernel, scale=scale),
        grid=(b // pack,),
        in_specs=[
            pl.BlockSpec((2, pack, s, LANES), lambda bi: (0, bi, 0, 0)),
            pl.BlockSpec((FOURIER_WIDTH, FOURIER_WIDTH), const2),
            pl.BlockSpec((FOURIER_WIDTH, FOURIER_WIDTH), const2),
            table,
            table,
            pl.BlockSpec((FOURIER_WIDTH, FOURIER_WIDTH), const2),
        ],
        out_specs=pl.BlockSpec((pack, s, FOURIER_WIDTH), lambda bi: (bi, 0, 0)),
        out_shape=jax.ShapeDtypeStruct((b, s, FOURIER_WIDTH), BF16),
        scratch_shapes=[pltpu.VMEM((2, s, pack * FOURIER_WIDTH), BF16)],
        compiler_params=_params(1),
        name="fourier",
    )(u, cc_bd, sc_bd, f_even, f_odd, w_bd)


def _split3(g):
    hi = g.astype(BF16)
    r1 = g - hi.astype(F32)
    mid = r1.astype(BF16)
    lo = (r1 - mid.astype(F32)).astype(BF16)
    return hi, mid, lo


def _gla_kernel(qk_ref, v_ref, r_ref, z_ref, w2_ref, b2_ref, g_ref, o_ref,
                la_ref, vt_ref, acc_ref, qt_ref, kt_ref, a_ref, kv_ref, dec_ref, st_ref):
    s = v_ref.shape[1]
    c = GLA_CHUNK
    pair = 2 * c
    n_pairs = s // pair
    n_chunks = s // c
    kc = GLA_K_COLS

    logit = _dot(z_ref[0].astype(BF16), w2_ref[...]) + b2_ref[...]
    la_ref[...] = (jnp.minimum(logit, 0.0) - jnp.log(1.0 + jnp.exp(-jnp.abs(logit)))) * (1.0 / GLA_GATE_TAU)
    vt_ref[...] = v_ref[0].T.astype(BF16)

    row_i = lax.broadcasted_iota(jnp.int32, (pair, pair), 0)
    col_i = lax.broadcasted_iota(jnp.int32, (pair, pair), 1)
    same_chunk = (row_i // c) == (col_i // c)
    cum_mats = ((same_chunk & (col_i <= row_i)).astype(BF16), (same_chunk & (col_i >= row_i)).astype(BF16))
    r_hk = lax.broadcasted_iota(jnp.int32, (GLA_HEADS * c, kc), 0) // c
    l_hk = lax.broadcasted_iota(jnp.int32, (GLA_HEADS * c, kc), 1) // GLA_K_DIM
    blk_k = r_hk == l_hk
    r_hv = lax.broadcasted_iota(jnp.int32, (GLA_HEADS * c, GLA_WIDTH), 0) // c
    l_hv = lax.broadcasted_iota(jnp.int32, (GLA_HEADS * c, GLA_WIDTH), 1) // GLA_V_DIM
    blk_v = r_hv == l_hv
    qi = lax.broadcasted_iota(jnp.int32, (c, GLA_HEADS * c), 0)
    kj = lax.broadcasted_iota(jnp.int32, (c, GLA_HEADS * c), 1) % c
    q_scale = GLA_K_DIM ** -0.5

    causal = (kj <= qi, kj >= qi)

    def pair_rows(p):
        start = p * pair
        return pl.ds(start if isinstance(start, int) else pl.multiple_of(start, pair), pair)

    def decayed_qk(p):
        rows = pair_rows(p)
        q2 = qk_ref[0, rows, 0:kc] * q_scale
        k2 = qk_ref[0, rows, kc:2 * kc]
        for d in range(2):
            hi, mid, lo = _split3(la_ref[rows, d * kc:(d + 1) * kc])
            c3 = _dot(cum_mats[d], jnp.concatenate([hi, mid, lo], axis=1))
            bcum = c3[:, 0:kc] + c3[:, kc:2 * kc] + c3[:, 2 * kc:3 * kc]
            eb = jnp.exp(bcum)
            qt_ref[d, rows, :] = (q2 * eb).astype(BF16)
            kt_ref[d, rows, :] = k2 * jnp.exp(-bcum)
            for ci in range(2):
                edge = ci * c if d == 1 else (ci + 1) * c - 1
                dec_ref[d, 2 * p + ci] = jnp.broadcast_to(eb[edge:edge + 1], (8, kc))

    def intra_scores(p):
        for ci in range(2):
            start = p * pair + ci * c
            rows = pl.ds(start if isinstance(start, int) else pl.multiple_of(start, c), c)
            for d in range(2):
                kt_bd = jnp.where(blk_k, jnp.concatenate([kt_ref[d, rows, :]] * GLA_HEADS, axis=0), 0.0)
                a = _dot_nt(qt_ref[d, rows, :], kt_bd.astype(BF16))
                a_ref[2 * p + ci, d * c:(d + 1) * c, :] = jnp.where(causal[d], a, 0.0).astype(BF16)

    def intra_out_and_kv(p):
        rows2 = pair_rows(p)
        v2 = v_ref[0, rows2, :]
        vt2 = vt_ref[:, rows2]
        kt2 = jnp.concatenate([kt_ref[0, rows2, :], kt_ref[1, rows2, :]], axis=1)
        for ci in range(2):
            sl = slice(ci * c, (ci + 1) * c)
            start = p * pair + ci * c
            rows = pl.ds(start if isinstance(start, int) else pl.multiple_of(start, c), c)
            v_bd = jnp.where(blk_v, jnp.concatenate([v2[sl]] * GLA_HEADS, axis=0), 0.0).astype(BF16)
            o_both = _dot(a_ref[2 * p + ci], v_bd)
            acc_ref[rows, :] = o_both[0:c] + o_both[c:2 * c]
            gap = jnp.zeros((c, 2 * kc), F32)
            kt_pair = jnp.concatenate([kt2[sl], gap] if ci == 0 else [gap, kt2[sl]], axis=0).astype(BF16)
            kv = _dot(vt2, kt_pair)
            kv_ref[0, 2 * p + ci] = jnp.where(blk_k, kv[:, 0:kc], 0.0)
            kv_ref[1, 2 * p + ci] = jnp.where(blk_k, kv[:, kc:2 * kc], 0.0)

    decayed_qk(0)
    intra_scores(0)
    decayed_qk(1)

    def intra_step(p, carry):
        intra_out_and_kv(p)
        intra_scores(p + 1)
        decayed_qk(p + 2)
        return carry

    lax.fori_loop(0, n_pairs - 2, intra_step, 0, unroll=2)
    intra_out_and_kv(n_pairs - 2)
    intra_scores(n_pairs - 1)
    intra_out_and_kv(n_pairs - 1)

    st_ref[...] = jnp.zeros_like(st_ref)

    def scan_step(i, carry):
        for d in range(2):
            n = i if d == 0 else n_chunks - 1 - i
            rows = pl.ds(pl.multiple_of(n * c, c), c)
            st = st_ref[d]
            acc_ref[rows, :] += _dot_nt(qt_ref[d, rows, :], st.astype(BF16))
            st = (st + kv_ref[d, n]).reshape(GLA_WIDTH // 8, 8, kc) * dec_ref[d, n][None]
            st_ref[d] = st.reshape(GLA_WIDTH, kc)
        return carry

    lax.fori_loop(0, n_chunks, scan_step, 0, unroll=8)

    o = acc_ref[...]
    o2 = o * o
    o2_hi = o2.astype(BF16)
    o2_lo = (o2 - o2_hi.astype(F32)).astype(BF16)
    gi = lax.broadcasted_iota(jnp.int32, (GLA_WIDTH, GLA_WIDTH), 0) // GLA_V_DIM
    gj = lax.broadcasted_iota(jnp.int32, (GLA_WIDTH, GLA_WIDTH), 1) // GLA_V_DIM
    ones_bd = (gi == gj).astype(BF16)
    ms = (_dot(o2_hi, ones_bd) + _dot(o2_lo, ones_bd)) * (1.0 / GLA_V_DIM)
    r = r_ref[0]
    gate = r * (1.0 / (1.0 + jnp.exp(-r)))
    o_ref[0] = (o * lax.rsqrt(ms + LN_EPS) * g_ref[...] * gate).astype(BF16)


def _gla(gqk, gv, gr, gz, w2_bd, b2_cat, g_tiled):
    b, s, _ = gv.shape
    per_b = lambda bi: (bi, 0, 0)
    const = lambda bi: (0, 0)
    return pl.pallas_call(
        _gla_kernel,
        grid=(b,),
        in_specs=[
            pl.BlockSpec((1, s, 2 * GLA_K_COLS), per_b),
            pl.BlockSpec((1, s, GLA_WIDTH), per_b),
            pl.BlockSpec((1, s, GLA_WIDTH), per_b),
            pl.BlockSpec((1, s, 2 * GLA_GATE_RANK), per_b),
            pl.BlockSpec((2 * GLA_GATE_RANK, 2 * GLA_K_COLS), const),
            pl.BlockSpec((1, 2 * GLA_K_COLS), const),
            pl.BlockSpec((1, GLA_WIDTH), const),
        ],
        out_specs=pl.BlockSpec((1, s, GLA_WIDTH), per_b),
        out_shape=jax.ShapeDtypeStruct((b, s, GLA_WIDTH), BF16),
        scratch_shapes=[
            pltpu.VMEM((s, 2 * GLA_K_COLS), F32),
            pltpu.VMEM((GLA_WIDTH, s), BF16),
            pltpu.VMEM((s, GLA_WIDTH), F32),
            pltpu.VMEM((2, s, GLA_K_COLS), BF16),
            pltpu.VMEM((2, s, GLA_K_COLS), F32),
            pltpu.VMEM((s // GLA_CHUNK, 2 * GLA_CHUNK, GLA_WIDTH), BF16),
            pltpu.VMEM((2, s // GLA_CHUNK, GLA_WIDTH, GLA_K_COLS), F32),
            pltpu.VMEM((2, s // GLA_CHUNK, 8, GLA_K_COLS), F32),
            pltpu.VMEM((2, GLA_WIDTH, GLA_K_COLS), F32),
        ],
        compiler_params=_params(1),
        name="gla",
    )(gqk, gv, gr, gz, w2_bd, b2_cat, g_tiled)


def _layer_norm(y, g, b):
    mu = jnp.mean(y, axis=1, keepdims=True)
    d = y - mu
    var = jnp.mean(d * d, axis=1, keepdims=True)
    return d * lax.rsqrt(var + LN_EPS) * g + b


ROW_SUB = 256


def _outproj_kernel(od_ref, of_ref, og_ref, x_ref, w_ref, g_ref, b_ref, o_ref):
    w = w_ref[...].astype(BF16)
    for r in range(x_ref.shape[0] // ROW_SUB):
        rows = slice(r * ROW_SUB, (r + 1) * ROW_SUB)
        mixed = jnp.concatenate([od_ref[0, hd, rows, :] for hd in range(DIFF_HEADS)]
                                + [of_ref[rows, :], og_ref[rows, :]], axis=1)
        m = _dot(mixed, w)
        o_ref[rows, :] = _layer_norm(DEEPNORM_ALPHA * x_ref[rows, :] + m, g_ref[...], b_ref[...])


def _out_proj(o_diff, o_four, o_gla, x2d, w_all, layer, g, b, tm=1024):
    m = x2d.shape[0]
    seq = o_diff.shape[2]
    n_seq_tiles = seq // tm
    row = lambda i: (i, 0)
    const = lambda i: (0, 0)
    return pl.pallas_call(
        _outproj_kernel,
        grid=(m // tm,),
        in_specs=[
            pl.BlockSpec((1, DIFF_HEADS, tm, LANES), lambda i: (i // n_seq_tiles, 0, i % n_seq_tiles, 0)),
            pl.BlockSpec((tm, FOURIER_WIDTH), row),
            pl.BlockSpec((tm, GLA_WIDTH), row),
            pl.BlockSpec((tm, D_MODEL), row),
            _layer_weight(w_all, layer),
            pl.BlockSpec((1, D_MODEL), const),
            pl.BlockSpec((1, D_MODEL), const),
        ],
        out_specs=pl.BlockSpec((tm, D_MODEL), row),
        out_shape=jax.ShapeDtypeStruct((m, D_MODEL), F32),
        compiler_params=_params(1),
        name="out_proj",
    )(o_diff, o_four, o_gla, x2d, w_all, g, b)


FFN_CHUNK = 256


def _ffn_kernel(x_ref, wg_ref, wu_ref, wd_ref, g_ref, b_ref, o_ref, acc_ref):
    x = x_ref[...]
    xb = x.astype(BF16)
    for ci in range(FFN_HIDDEN // FFN_CHUNK):
        cols = slice(ci * FFN_CHUNK, (ci + 1) * FFN_CHUNK)
        hg = _dot(xb, wg_ref[:, cols].astype(BF16))
        hu = _dot(xb, wu_ref[:, cols].astype(BF16))
        act = (hg * (1.0 / (1.0 + jnp.exp(-hg))) * hu).astype(BF16)
        part = _dot(act, wd_ref[cols, :].astype(BF16))
        if ci == 0:
            acc_ref[...] = part
        else:
            acc_ref[...] += part
    o_ref[...] = _layer_norm(DEEPNORM_ALPHA * x + acc_ref[...], g_ref[...], b_ref[...])


def _ffn(x2d, wg_all, wu_all, wd_all, layer, g, b, tm=512):
    m = x2d.shape[0]
    row = lambda i: (i, 0)
    const = lambda i: (0, 0)
    return pl.pallas_call(
        _ffn_kernel,
        grid=(m // tm,),
        in_specs=[
            pl.BlockSpec((tm, D_MODEL), row),
            _layer_weight(wg_all, layer),
            _layer_weight(wu_all, layer),
            _layer_weight(wd_all, layer),
            pl.BlockSpec((1, D_MODEL), const),
            pl.BlockSpec((1, D_MODEL), const),
        ],
        out_specs=pl.BlockSpec((tm, D_MODEL), row),
        out_shape=jax.ShapeDtypeStruct((m, D_MODEL), F32),
        scratch_shapes=[pltpu.VMEM((tm, D_MODEL), F32)],
        compiler_params=_params(1),
        name="ffn",
    )(x2d, wg_all, wu_all, wd_all, g, b)


def _rope_tables(seq):
    half = DIFF_QK_DIM // 2
    pos = jnp.arange(seq, dtype=F32)
    inv_freq = ROPE_THETA ** (-jnp.arange(0, DIFF_QK_DIM, 2, dtype=F32) / DIFF_QK_DIM)
    ang = pos[:, None] * inv_freq[None, :]
    cos, sin = jnp.cos(ang), jnp.sin(ang)
    reps = LANES // DIFF_QK_DIM
    cos128 = jnp.tile(jnp.concatenate([cos, cos], axis=1), (1, reps))
    sin128 = jnp.tile(jnp.concatenate([-sin, sin], axis=1), (1, reps))
    assert cos128.shape == (seq, LANES) and half * 2 == DIFF_QK_DIM
    return cos128, sin128, cos128.T, sin128.T


def _dft_tables(seq):
    half = seq // 2
    j = np.arange(half, dtype=np.int64)[:, None]
    pos = np.arange(half, dtype=np.int64)[None, :]
    ang_even = 2.0 * np.pi * ((j * pos) % half).astype(np.float64) / half
    ang_odd = 2.0 * np.pi * ((j * (2 * pos + 1)) % seq).astype(np.float64) / seq
    f_even = np.concatenate([np.cos(ang_even), -np.sin(ang_even)], axis=1)
    f_odd = np.concatenate([np.cos(ang_odd), -np.sin(ang_odd)], axis=1)
    c = np.arange(FOURIER_GROUP_DIM, dtype=np.int64)
    ang_c = 2.0 * np.pi * ((c[:, None] * c[None, :]) % FOURIER_GROUP_DIM).astype(np.float64) / FOURIER_GROUP_DIM
    eye = np.eye(FOURIER_GROUPS)
    cc_bd = np.kron(eye, np.cos(ang_c))
    sc_bd = np.kron(eye, np.sin(ang_c))
    return (jnp.asarray(f_even, dtype=F32), jnp.asarray(f_odd, dtype=F32),
            jnp.asarray(cc_bd, dtype=F32).astype(BF16), jnp.asarray(sc_bd, dtype=F32).astype(BF16))


def _block_diag(blocks):
    n = len(blocks)
    rows = []
    for i, blk in enumerate(blocks):
        rows.append(jnp.concatenate(
            [blk if j == i else jnp.zeros((blk.shape[0], blocks[j].shape[1]), blk.dtype) for j in range(n)],
            axis=1))
    return jnp.concatenate(rows, axis=0)


def kernel(x, w_in, diff_lambda, diff_norm_g, fourier_w, gla_gate_w2, gla_gate_b2, gla_norm_g, w_out,
           ln1_g, ln1_b, ffn_w_gate, ffn_w_up, ffn_w_down, ln2_g, ln2_b):
    b, s, d = x.shape
    m = b * s
    rope = _rope_tables(s)
    f_even, f_odd, cc_bd, sc_bd = _dft_tables(s)
    x2d = x.reshape(m, d)
    for l in range(DEPTH):
        lam_init = 0.8 - 0.6 * math.exp(-0.3 * l)
        wqv_t = jnp.concatenate([w_in[l, :, OFF_DQ:OFF_DK], w_in[l, :, OFF_DV:OFF_FU]], axis=1).T.astype(BF16)
        w_rest = jnp.concatenate([w_in[l, :, OFF_DK:OFF_DV], w_in[l, :, OFF_FU:]], axis=1).astype(BF16)
        qt, k, vt, fu, gqk, gv, gr, gz = _in_proj(x2d, wqv_t, w_rest, rope, s)
        o_diff = _diff_attention(qt, k, vt, diff_lambda[l], diff_norm_g[l].reshape(1, -1), lam_init)
        w_four = _block_diag([fourier_w[l, g] for g in range(FOURIER_GROUPS)]).astype(BF16)
        o_four = _fourier(fu.reshape(-1, b, s, LANES), cc_bd, sc_bd, f_even, f_odd, w_four)
        w2_bd = _block_diag([gla_gate_w2[l, 0], gla_gate_w2[l, 1]]).astype(BF16)
        b2_cat = gla_gate_b2[l].reshape(1, -1)
        g_gla = jnp.tile(gla_norm_g[l], GLA_HEADS).reshape(1, -1)
        o_gla = _gla(gqk.reshape(b, s, -1), gv.reshape(b, s, -1), gr.reshape(b, s, -1),
                     gz.reshape(b, s, -1), w2_bd, b2_cat, g_gla)
        x2d = _out_proj(o_diff, o_four.reshape(m, -1), o_gla.reshape(m, -1), x2d,
                        w_out, l, ln1_g[l].reshape(1, -1), ln1_b[l].reshape(1, -1))
        x2d = _ffn(x2d, ffn_w_gate, ffn_w_up, ffn_w_down, l,
                   ln2_g[l].reshape(1, -1), ln2_b[l].reshape(1, -1))
    return x2d.reshape(b, s, d)
```

```python
import functools
import math

import jax
import jax.numpy as jnp
import numpy as np
from jax import lax
from jax.experimental import pallas as pl
from jax.experimental.pallas import tpu as pltpu

D_MODEL = 1024
DEPTH = 2
DIFF_HEADS = 4
DIFF_QK_DIM = 64
DIFF_V_DIM = 128
DIFF_WIDTH = 512
DIFF_QK_COLS = 512
ROPE_THETA = 10000.0
FOURIER_GROUPS = 4
FOURIER_GROUP_DIM = 64
FOURIER_WIDTH = 256
GLA_HEADS = 4
GLA_V_DIM = 64
GLA_K_DIM = 32
GLA_WIDTH = 256
GLA_K_COLS = 128
GLA_GATE_RANK = 16
GLA_GATE_TAU = 16.0
GLA_CHUNK = 64
IN_WIDTH = 2592
FFN_HIDDEN = 2816
DEEPNORM_ALPHA = (2 * DEPTH) ** 0.25
LN_EPS = 1e-5

OFF_DQ, OFF_DK, OFF_DV, OFF_FU = 0, 512, 1024, 1536
OFF_GQ, OFF_GV, OFF_GR, OFF_GZ = 1792, 2048, 2304, 2560

LANES = 128
VMEM_LIMIT = 56 * 1024 * 1024

BF16 = jnp.bfloat16
F32 = jnp.float32


def _dot(a, b):
    return jnp.dot(a, b, preferred_element_type=F32)


def _dot_nt(a, b):
    return lax.dot_general(a, b, (((1,), (1,)), ((), ())), preferred_element_type=F32)


def _params(n_grid_dims):
    return pltpu.CompilerParams(
        dimension_semantics=("arbitrary",) * n_grid_dims,
        vmem_limit_bytes=VMEM_LIMIT)


def _rope_slab(t, cos, sin_signed, first_half):
    swapped = jnp.where(first_half, pltpu.roll(t, 96, 1), pltpu.roll(t, 32, 1))
    return t * cos + swapped * sin_signed


ATTN_TQ = 256
ATTN_KC = 256


def _rest_col(col):
    assert col >= OFF_DK and not OFF_DV <= col < OFF_FU
    return col - OFF_DK if col < OFF_DV else col - OFF_DK - DIFF_WIDTH


def _inproj_kernel(x_ref, wqv_ref, w_ref, cos_ref, sin_ref, cost_ref, sint_ref,
                   qt_ref, k_ref, vt_ref, fu_ref, gqk_ref, gv_ref, gr_ref, gz_ref):
    xb = x_ref[...].astype(BF16)
    tm = xb.shape[0]
    cos = cos_ref[...]
    sin_signed = sin_ref[...]
    lane = lax.broadcasted_iota(jnp.int32, cos.shape, 1)
    first_half = (lane % DIFF_QK_DIM) < (DIFF_QK_DIM // 2)
    qk_scale = DIFF_QK_DIM ** -0.5 * math.log2(math.e)
    hqv_t = _dot_nt(wqv_ref[...], xb)
    cos_t = cost_ref[...]
    sin_t = sint_ref[...]
    half = DIFF_QK_DIM // 2
    wcols = lambda lo, hi: w_ref[:, _rest_col(lo):_rest_col(hi - 1) + 1]
    hk = _dot(xb, wcols(OFF_DK, OFF_DV))
    for hd in range(DIFF_HEADS):
        qt = hqv_t[hd * LANES:(hd + 1) * LANES]
        swapped = jnp.concatenate(
            [qt[half:2 * half], qt[0:half], qt[3 * half:4 * half], qt[2 * half:3 * half]], axis=0)
        qt = ((qt * cos_t + swapped * sin_t) * qk_scale).astype(BF16)
        vt = hqv_t[DIFF_QK_COLS + hd * LANES:DIFF_QK_COLS + (hd + 1) * LANES].astype(BF16)
        for j in range(tm // ATTN_TQ):
            qt_ref[0, hd, j] = qt[:, j * ATTN_TQ:(j + 1) * ATTN_TQ]
        for j in range(tm // ATTN_KC):
            vt_ref[0, hd, j] = vt[:, j * ATTN_KC:(j + 1) * ATTN_KC]
        slab = slice(hd * LANES, (hd + 1) * LANES)
        k_ref[0, hd] = _rope_slab(hk[:, slab], cos, sin_signed, first_half).astype(BF16)
    hfu = _dot(xb, wcols(OFF_FU, OFF_GQ))
    for lane_half in range(FOURIER_WIDTH // LANES):
        fu_ref[lane_half] = hfu[:, lane_half * LANES:(lane_half + 1) * LANES]
    gqk_ref[...] = _dot(xb, wcols(OFF_GQ, OFF_GV))
    gv_ref[...] = _dot(xb, wcols(OFF_GV, OFF_GR))
    gr_ref[...] = _dot(xb, wcols(OFF_GR, OFF_GZ))
    gz_ref[...] = _dot(xb, wcols(OFF_GZ, IN_WIDTH))


def _layer_weight(w_all, layer):
    return pl.BlockSpec((None,) + w_all.shape[1:], lambda i: (layer,) + (0,) * (w_all.ndim - 1),
                        pipeline_mode=pl.Buffered(1))


def _in_proj(x2d, wqv_t, w_rest, rope, seq, tm=512):
    m = x2d.shape[0]
    batch = m // seq
    n_seq_tiles = seq // tm
    cos128, sin128, cos_t, sin_t = rope
    row = lambda i: (i, 0)
    const = lambda i: (0, 0)
    tab = lambda i: (i % n_seq_tiles, 0)
    tab_t = lambda i: (0, i % n_seq_tiles)
    tiled = lambda t: (jax.ShapeDtypeStruct((batch, DIFF_HEADS, seq // t, LANES, t), BF16),
                       pl.BlockSpec((1, DIFF_HEADS, tm // t, LANES, t),
                                    lambda i: (i // n_seq_tiles, 0, i % n_seq_tiles, 0, 0)))
    qt_shape, qt_spec = tiled(ATTN_TQ)
    vt_shape, vt_spec = tiled(ATTN_KC)
    k_shape = jax.ShapeDtypeStruct((batch, DIFF_HEADS, seq, LANES), BF16)
    k_spec = pl.BlockSpec((1, DIFF_HEADS, tm, LANES), lambda i: (i // n_seq_tiles, 0, i % n_seq_tiles, 0))
    fu_shape = jax.ShapeDtypeStruct((FOURIER_WIDTH // LANES, m, LANES), F32)
    fu_spec = pl.BlockSpec((FOURIER_WIDTH // LANES, tm, LANES), lambda i: (0, i, 0))
    flat_shapes = (
        jax.ShapeDtypeStruct((m, 2 * GLA_K_COLS), F32),
        jax.ShapeDtypeStruct((m, GLA_WIDTH), F32),
        jax.ShapeDtypeStruct((m, GLA_WIDTH), F32),
        jax.ShapeDtypeStruct((m, 2 * GLA_GATE_RANK), F32),
    )
    return pl.pallas_call(
        _inproj_kernel,
        grid=(m // tm,),
        in_specs=[
            pl.BlockSpec((tm, D_MODEL), row),
            pl.BlockSpec((DIFF_QK_COLS + DIFF_WIDTH, D_MODEL), const),
            pl.BlockSpec(w_rest.shape, const),
            pl.BlockSpec((tm, LANES), tab),
            pl.BlockSpec((tm, LANES), tab),
            pl.BlockSpec((LANES, tm), tab_t),
            pl.BlockSpec((LANES, tm), tab_t),
        ],
        out_specs=(qt_spec, k_spec, vt_spec, fu_spec)
        + tuple(pl.BlockSpec((tm, s.shape[1]), row) for s in flat_shapes),
        out_shape=(qt_shape, k_shape, vt_shape, fu_shape) + flat_shapes,
        compiler_params=_params(1),
        name="in_proj",
    )(x2d, wqv_t, w_rest, cos128, sin128, cos_t, sin_t)


ATTN_SUB = 128
ATTN_GROUP = 4


def _attn_kernel(lam_ref, g_ref, qt_ref, k_ref, vt_ref, o_ref,
                 sa_ref, sb_ref, pa_ref, pb_ref, mx_ref, coef_ref, ot_ref, *, lam_init):
    tq, kc, sub = ATTN_TQ, ATTN_KC, ATTN_SUB
    seq = k_ref.shape[2]
    tiles = seq // tq
    n_units = DIFF_HEADS * tiles
    n_chunks = seq // kc
    lp = lam_ref[...]
    lam = (jnp.exp(jnp.sum(lp[0:1] * lp[1:2], axis=1, keepdims=True))
           - jnp.exp(jnp.sum(lp[2:3] * lp[3:4], axis=1, keepdims=True)) + lam_init)
    gain = g_ref[...] * (1.0 - lam_init)
    feature = lax.broadcasted_iota(jnp.int32, (LANES, tq), 0)
    first_map = feature < DIFF_QK_DIM
    s_bufs = (sa_ref, sb_ref)
    p_bufs = (pa_ref, pb_ref)

    n_groups = n_chunks // ATTN_GROUP
    slabs_per_trip = (tq // LANES) // n_groups
    assert slabs_per_trip * n_groups * LANES == tq

    def step(k, par, do_a=True, do_b=True, do_c=True, do_d=True):
        s_w, s_r = s_bufs[par], s_bufs[1 - par]
        p_w, p_r = p_bufs[1 - par], p_bufs[par]
        if do_d:
            h_d = (k - 3) // tiles
            q_base = ((k - 3) % tiles) * tq
        if do_a:
            h_a = k // tiles
            qt = qt_ref[0, h_a, k % tiles]
            zero = jnp.zeros_like(qt)
            qw = (jnp.where(first_map, qt, zero), jnp.where(first_map, zero, qt))
        if do_b:
            m8 = (mx_ref[1 - par, 0], mx_ref[1 - par, 1])
        if do_c:
            h_c = (k - 2) // tiles
            c16 = (coef_ref[par, 0], coef_ref[par, 1])

        def chunk(c, carry):
            mx = [carry[0], carry[1]]
            ls = [carry[2], carry[3]]
            acc = carry[4]
            for part in range(kc // sub):
                rows = pl.ds(pl.multiple_of(c * kc + part * sub, sub), sub)
                if do_a:
                    keys = k_ref[0, h_a, rows, :]
                    for m in range(2):
                        st = _dot(keys, qw[m])
                        s_w[m, rows, :] = st
                        mx[m] = jnp.maximum(mx[m], jnp.max(st.reshape(sub // 8, 8, tq), axis=0))
                if do_b:
                    for m in range(2):
                        p = jnp.exp2(s_r[m, rows, :].reshape(sub // 8, 8, tq) - m8[m][None])
                        ls[m] = ls[m] + jnp.sum(p, axis=0)
                        p_w[m, rows, :] = p.reshape(sub, tq).astype(BF16)
            if do_c:
                rows = pl.ds(pl.multiple_of(c * kc, kc), kc)
                a = (p_r[0, rows, :].reshape(kc // 16, 16, tq) * c16[0][None]
                     - p_r[1, rows, :].reshape(kc // 16, 16, tq) * c16[1][None])
                part_o = _dot(vt_ref[0, h_c, c], a.reshape(kc, tq))
                acc = part_o if acc is None else acc + part_o
            return mx[0], mx[1], ls[0], ls[1], acc

        lowest = jnp.full((8, tq), -jnp.inf, F32)
        nothing = jnp.zeros((8, tq), F32)
        def group(gi, carry):
            if do_d:
                for slab in range(slabs_per_trip):
                    si = gi * slabs_per_trip + slab
                    o = ot_ref[1 - par, si].T
                    ms = jnp.mean(o * o, axis=1, keepdims=True)
                    q_start = q_base + si * LANES
                    q_rows = pl.ds(q_start if isinstance(q_start, int) else pl.multiple_of(q_start, LANES),
                                   LANES)
                    o_ref[0, h_d, q_rows, :] = (o * lax.rsqrt(ms + LN_EPS) * gain).astype(BF16)
            stats = carry + (None,)
            for c in range(ATTN_GROUP):
                stats = chunk(gi * ATTN_GROUP + c, stats)
            if do_c:
                for si in range(tq // LANES):
                    ot_ref[par, si] += stats[4][:, si * LANES:(si + 1) * LANES]
            return stats[:4]

        if do_c:
            ot_ref[par] = jnp.zeros(ot_ref.shape[1:], F32)
        mx0, mx1, ls0, ls1 = lax.fori_loop(0, n_groups, group, (lowest, lowest, nothing, nothing))
        if do_a:
            mx_ref[par, 0] = jnp.broadcast_to(jnp.max(mx0, axis=0, keepdims=True), (8, tq))
            mx_ref[par, 1] = jnp.broadcast_to(jnp.max(mx1, axis=0, keepdims=True), (8, tq))
        if do_b:
            l0 = jnp.sum(ls0, axis=0, keepdims=True)
            l1 = jnp.sum(ls1, axis=0, keepdims=True)
            coef_ref[1 - par, 0] = jnp.broadcast_to(1.0 / l0, (16, tq)).astype(BF16)
            coef_ref[1 - par, 1] = jnp.broadcast_to(lam / l1, (16, tq)).astype(BF16)

    def step_pair(j, carry):
        step(2 * j, 0)
        step(2 * j + 1, 1)
        return carry

    step(0, 0, do_b=False, do_c=False, do_d=False)
    step(1, 1, do_c=False, do_d=False)
    step(2, 0, do_d=False)
    step(3, 1)
    lax.fori_loop(2, n_units // 2, step_pair, 0)
    step(n_units, 0, do_a=False)
    step(n_units + 1, 1, do_a=False, do_b=False)
    step(n_units + 2, 0, do_a=False, do_b=False, do_c=False)


def _diff_attention(qt, k, vt, lam_params, g, lam_init):
    b, _, s, _ = k.shape
    whole = lambda a: pl.BlockSpec((1,) + a.shape[1:], lambda bi: (bi,) + (0,) * (a.ndim - 1))
    return pl.pallas_call(
        functools.partial(_attn_kernel, lam_init=lam_init),
        grid=(b,),
        in_specs=[
            pl.BlockSpec((4, DIFF_QK_DIM), lambda bi: (0, 0)),
            pl.BlockSpec((1, DIFF_V_DIM), lambda bi: (0, 0)),
            whole(qt), whole(k), whole(vt),
        ],
        out_specs=whole(k),
        out_shape=jax.ShapeDtypeStruct((b, DIFF_HEADS, s, LANES), BF16),
        scratch_shapes=[
            pltpu.VMEM((2, s, ATTN_TQ), F32), pltpu.VMEM((2, s, ATTN_TQ), F32),
            pltpu.VMEM((2, s, ATTN_TQ), BF16), pltpu.VMEM((2, s, ATTN_TQ), BF16),
            pltpu.VMEM((2, 2, 8, ATTN_TQ), F32),
            pltpu.VMEM((2, 2, 16, ATTN_TQ), BF16),
            pltpu.VMEM((2, ATTN_TQ // LANES, DIFF_V_DIM, LANES), F32),
        ],
        compiler_params=_params(1),
        name="diff_attn",
    )(lam_params, g, qt, k, vt)


FOURIER_PACK = 2


def _fourier_kernel(u_ref, cc_ref, sc_ref, fe_ref, fo_ref, w_ref, o_ref, ab_ref, *, scale):
    s = u_ref.shape[2]
    half = s // 2
    w = FOURIER_WIDTH
    for e in range(FOURIER_PACK):
        lanes = slice(e * w, (e + 1) * w)
        for parity in range(2):
            u = jnp.concatenate([u_ref[lh, e, pl.ds(parity, half, stride=2), :] for lh in range(2)],
                                axis=1).astype(BF16)
            ab_ref[parity, 0:half, lanes] = _dot(u, cc_ref[...]).astype(BF16)
            ab_ref[parity, half:s, lanes] = _dot(u, sc_ref[...]).astype(BF16)
    even = _dot(fe_ref[...].astype(BF16), ab_ref[0])
    odd = _dot(fo_ref[...].astype(BF16), ab_ref[1])
    for rows, z in ((slice(0, half), (even + odd) * scale), (slice(half, s), (even - odd) * scale)):
        for e in range(FOURIER_PACK):
            y = _dot(z[:, e * w:(e + 1) * w].astype(BF16), w_ref[...])
            o_ref[e, rows, :] = y.astype(BF16)


def _fourier(u, cc_bd, sc_bd, f_even, f_odd, w_bd):
    _, b, s, _ = u.shape
    pack = FOURIER_PACK
    scale = 1.0 / math.sqrt(s * FOURIER_GROUP_DIM)
    const2 = lambda bi: (0, 0)
    table = pl.BlockSpec((s // 2, s), const2, pipeline_mode=pl.Buffered(1))
    return pl.pallas_call(
        functools.partial(_fourier_kernel, scale=scale),
        grid=(b // pack,),
        in_specs=[
            pl.BlockSpec((2, pack, s, LANES), lambda bi: (0, bi, 0, 0)),
            pl.BlockSpec((FOURIER_WIDTH, FOURIER_WIDTH), const2),
            pl.BlockSpec((FOURIER_WIDTH, FOURIER_WIDTH), const2),
            table,
            table,
            pl.BlockSpec((FOURIER_WIDTH, FOURIER_WIDTH), const2),
        ],
        out_specs=pl.BlockSpec((pack, s, FOURIER_WIDTH), lambda bi: (bi, 0, 0)),
        out_shape=jax.ShapeDtypeStruct((b, s, FOURIER_WIDTH), BF16),
        scratch_shapes=[pltpu.VMEM((2, s, pack * FOURIER_WIDTH), BF16)],
        compiler_params=_params(1),
        name="fourier",
    )(u, cc_bd, sc_bd, f_even, f_odd, w_bd)


def _split3(g):
    hi = g.astype(BF16)
    r1 = g - hi.astype(F32)
    mid = r1.astype(BF16)
    lo = (r1 - mid.astype(F32)).astype(BF16)
    return hi, mid, lo


def _gla_kernel(qk_ref, v_ref, r_ref, z_ref, w2_ref, b2_ref, g_ref, o_ref,
                la_ref, vt_ref, acc_ref, qt_ref, kt_ref, a_ref, kv_ref, dec_ref, st_ref):
    s = v_ref.shape[1]
    c = GLA_CHUNK
    pair = 2 * c
    n_pairs = s // pair
    n_chunks = s // c
    kc = GLA_K_COLS

    logit = _dot(z_ref[0].astype(BF16), w2_ref[...]) + b2_ref[...]
    la_ref[...] = (jnp.minimum(logit, 0.0) - jnp.log(1.0 + jnp.exp(-jnp.abs(logit)))) * (1.0 / GLA_GATE_TAU)
    vt_ref[...] = v_ref[0].T.astype(BF16)

    row_i = lax.broadcasted_iota(jnp.int32, (pair, pair), 0)
    col_i = lax.broadcasted_iota(jnp.int32, (pair, pair), 1)
    same_chunk = (row_i // c) == (col_i // c)
    cum_mats = ((same_chunk & (col_i <= row_i)).astype(BF16), (same_chunk & (col_i >= row_i)).astype(BF16))
    r_hk = lax.broadcasted_iota(jnp.int32, (GLA_HEADS * c, kc), 0) // c
    l_hk = lax.broadcasted_iota(jnp.int32, (GLA_HEADS * c, kc), 1) // GLA_K_DIM
    blk_k = r_hk == l_hk
    r_hv = lax.broadcasted_iota(jnp.int32, (GLA_HEADS * c, GLA_WIDTH), 0) // c
    l_hv = lax.broadcasted_iota(jnp.int32, (GLA_HEADS * c, GLA_WIDTH), 1) // GLA_V_DIM
    blk_v = r_hv == l_hv
    qi = lax.broadcasted_iota(jnp.int32, (c, GLA_HEADS * c), 0)
    kj = lax.broadcasted_iota(jnp.int32, (c, GLA_HEADS * c), 1) % c
    q_scale = GLA_K_DIM ** -0.5

    causal = (kj <= qi, kj >= qi)

    def pair_rows(p):
        start = p * pair
        return pl.ds(start if isinstance(start, int) else pl.multiple_of(start, pair), pair)

    def decayed_qk(p):
        rows = pair_rows(p)
        q2 = qk_ref[0, rows, 0:kc] * q_scale
        k2 = qk_ref[0, rows, kc:2 * kc]
        for d in range(2):
            hi, mid, lo = _split3(la_ref[rows, d * kc:(d + 1) * kc])
            c3 = _dot(cum_mats[d], jnp.concatenate([hi, mid, lo], axis=1))
            bcum = c3[:, 0:kc] + c3[:, kc:2 * kc] + c3[:, 2 * kc:3 * kc]
            eb = jnp.exp(bcum)
            qt_ref[d, rows, :] = (q2 * eb).astype(BF16)
            kt_ref[d, rows, :] = k2 * jnp.exp(-bcum)
            for ci in range(2):
                edge = ci * c if d == 1 else (ci + 1) * c - 1
                dec_ref[d, 2 * p + ci] = jnp.broadcast_to(eb[edge:edge + 1], (8, kc))

    def intra_scores(p):
        for ci in range(2):
            start = p * pair + ci * c
            rows = pl.ds(start if isinstance(start, int) else pl.multiple_of(start, c), c)
            for d in range(2):
                kt_bd = jnp.where(blk_k, jnp.concatenate([kt_ref[d, rows, :]] * GLA_HEADS, axis=0), 0.0)
                a = _dot_nt(qt_ref[d, rows, :], kt_bd.astype(BF16))
                a_ref[2 * p + ci, d * c:(d + 1) * c, :] = jnp.where(causal[d], a, 0.0).astype(BF16)

    def intra_out_and_kv(p):
        rows2 = pair_rows(p)
        v2 = v_ref[0, rows2, :]
        vt2 = vt_ref[:, rows2]
        kt2 = jnp.concatenate([kt_ref[0, rows2, :], kt_ref[1, rows2, :]], axis=1)
        for ci in range(2):
            sl = slice(ci * c, (ci + 1) * c)
            start = p * pair + ci * c
            rows = pl.ds(start if isinstance(start, int) else pl.multiple_of(start, c), c)
            v_bd = jnp.where(blk_v, jnp.concatenate([v2[sl]] * GLA_HEADS, axis=0), 0.0).astype(BF16)
            o_both = _dot(a_ref[2 * p + ci], v_bd)
            acc_ref[rows, :] = o_both[0:c] + o_both[c:2 * c]
            gap = jnp.zeros((c, 2 * kc), F32)
            kt_pair = jnp.concatenate([kt2[sl], gap] if ci == 0 else [gap, kt2[sl]], axis=0).astype(BF16)
            kv = _dot(vt2, kt_pair)
            kv_ref[0, 2 * p + ci] = jnp.where(blk_k, kv[:, 0:kc], 0.0)
            kv_ref[1, 2 * p + ci] = jnp.where(blk_k, kv[:, kc:2 * kc], 0.0)

    decayed_qk(0)
    intra_scores(0)
    decayed_qk(1)

    def intra_step(p, carry):
        intra_out_and_kv(p)
        intra_scores(p + 1)
        decayed_qk(p + 2)
        return carry

    lax.fori_loop(0, n_pairs - 2, intra_step, 0, unroll=2)
    intra_out_and_kv(n_pairs - 2)
    intra_scores(n_pairs - 1)
    intra_out_and_kv(n_pairs - 1)

    st_ref[...] = jnp.zeros_like(st_ref)

    def scan_step(i, carry):
        for d in range(2):
            n = i if d == 0 else n_chunks - 1 - i
            rows = pl.ds(pl.multiple_of(n * c, c), c)
            st = st_ref[d]
            acc_ref[rows, :] += _dot_nt(qt_ref[d, rows, :], st.astype(BF16))
            st = (st + kv_ref[d, n]).reshape(GLA_WIDTH // 8, 8, kc) * dec_ref[d, n][None]
            st_ref[d] = st.reshape(GLA_WIDTH, kc)
        return carry

    lax.fori_loop(0, n_chunks, scan_step, 0, unroll=8)

    o = acc_ref[...]
    o2 = o * o
    o2_hi = o2.astype(BF16)
    o2_lo = (o2 - o2_hi.astype(F32)).astype(BF16)
    gi = lax.broadcasted_iota(jnp.int32, (GLA_WIDTH, GLA_WIDTH), 0) // GLA_V_DIM
    gj = lax.broadcasted_iota(jnp.int32, (GLA_WIDTH, GLA_WIDTH), 1) // GLA_V_DIM
    ones_bd = (gi == gj).astype(BF16)
    ms = (_dot(o2_hi, ones_bd) + _dot(o2_lo, ones_bd)) * (1.0 / GLA_V_DIM)
    r = r_ref[0]
    gate = r * (1.0 / (1.0 + jnp.exp(-r)))
    o_ref[0] = (o * lax.rsqrt(ms + LN_EPS) * g_ref[...] * gate).astype(BF16)


def _gla(gqk, gv, gr, gz, w2_bd, b2_cat, g_tiled):
    b, s, _ = gv.shape
    per_b = lambda bi: (bi, 0, 0)
    const = lambda bi: (0, 0)
    return pl.pallas_call(
        _gla_kernel,
        grid=(b,),
        in_specs=[
            pl.BlockSpec((1, s, 2 * GLA_K_COLS), per_b),
            pl.BlockSpec((1, s, GLA_WIDTH), per_b),
            pl.BlockSpec((1, s, GLA_WIDTH), per_b),
            pl.BlockSpec((1, s, 2 * GLA_GATE_RANK), per_b),
            pl.BlockSpec((2 * GLA_GATE_RANK, 2 * GLA_K_COLS), const),
            pl.BlockSpec((1, 2 * GLA_K_COLS), const),
            pl.BlockSpec((1, GLA_WIDTH), const),
        ],
        out_specs=pl.BlockSpec((1, s, GLA_WIDTH), per_b),
        out_shape=jax.ShapeDtypeStruct((b, s, GLA_WIDTH), BF16),
        scratch_shapes=[
            pltpu.VMEM((s, 2 * GLA_K_COLS), F32),
            pltpu.VMEM((GLA_WIDTH, s), BF16),
            pltpu.VMEM((s, GLA_WIDTH), F32),
            pltpu.VMEM((2, s, GLA_K_COLS), BF16),
            pltpu.VMEM((2, s, GLA_K_COLS), F32),
            pltpu.VMEM((s // GLA_CHUNK, 2 * GLA_CHUNK, GLA_WIDTH), BF16),
            pltpu.VMEM((2, s // GLA_CHUNK, GLA_WIDTH, GLA_K_COLS), F32),
            pltpu.VMEM((2, s // GLA_CHUNK, 8, GLA_K_COLS), F32),
            pltpu.VMEM((2, GLA_WIDTH, GLA_K_COLS), F32),
        ],
        compiler_params=_params(1),
        name="gla",
    )(gqk, gv, gr, gz, w2_bd, b2_cat, g_tiled)


def _layer_norm(y, g, b):
    mu = jnp.mean(y, axis=1, keepdims=True)
    d = y - mu
    var = jnp.mean(d * d, axis=1, keepdims=True)
    return d * lax.rsqrt(var + LN_EPS) * g + b


ROW_SUB = 256


def _outproj_kernel(od_ref, of_ref, og_ref, x_ref, w_ref, g_ref, b_ref, o_ref):
    w = w_ref[...].astype(BF16)
    for r in range(x_ref.shape[0] // ROW_SUB):
        rows = slice(r * ROW_SUB, (r + 1) * ROW_SUB)
        mixed = jnp.concatenate([od_ref[0, hd, rows, :] for hd in range(DIFF_HEADS)]
                                + [of_ref[rows, :], og_ref[rows, :]], axis=1)
        m = _dot(mixed, w)
        o_ref[rows, :] = _layer_norm(DEEPNORM_ALPHA * x_ref[rows, :] + m, g_ref[...], b_ref[...])


def _out_proj(o_diff, o_four, o_gla, x2d, w_all, layer, g, b, tm=1024):
    m = x2d.shape[0]
    seq = o_diff.shape[2]
    n_seq_tiles = seq // tm
    row = lambda i: (i, 0)
    const = lambda i: (0, 0)
    return pl.pallas_call(
        _outproj_kernel,
        grid=(m // tm,),
        in_specs=[
            pl.BlockSpec((1, DIFF_HEADS, tm, LANES), lambda i: (i // n_seq_tiles, 0, i % n_seq_tiles, 0)),
            pl.BlockSpec((tm, FOURIER_WIDTH), row),
            pl.BlockSpec((tm, GLA_WIDTH), row),
            pl.BlockSpec((tm, D_MODEL), row),
            _layer_weight(w_all, layer),
            pl.BlockSpec((1, D_MODEL), const),
            pl.BlockSpec((1, D_MODEL), const),
        ],
        out_specs=pl.BlockSpec((tm, D_MODEL), row),
        out_shape=jax.ShapeDtypeStruct((m, D_MODEL), F32),
        compiler_params=_params(1),
        name="out_proj",
    )(o_diff, o_four, o_gla, x2d, w_all, g, b)


FFN_CHUNK = 256


def _ffn_kernel(x_ref, wg_ref, wu_ref, wd_ref, g_ref, b_ref, o_ref, acc_ref):
    x = x_ref[...]
    xb = x.astype(BF16)
    for ci in range(FFN_HIDDEN // FFN_CHUNK):
        cols = slice(ci * FFN_CHUNK, (ci + 1) * FFN_CHUNK)
        hg = _dot(xb, wg_ref[:, cols].astype(BF16))
        hu = _dot(xb, wu_ref[:, cols].astype(BF16))
        act = (hg * (1.0 / (1.0 + jnp.exp(-hg))) * hu).astype(BF16)
        part = _dot(act, wd_ref[cols, :].astype(BF16))
        if ci == 0:
            acc_ref[...] = part
        else:
            acc_ref[...] += part
    o_ref[...] = _layer_norm(DEEPNORM_ALPHA * x + acc_ref[...], g_ref[...], b_ref[...])


def _ffn(x2d, wg_all, wu_all, wd_all, layer, g, b, tm=512):
    m = x2d.shape[0]
    row = lambda i: (i, 0)
    const = lambda i: (0, 0)
    return pl.pallas_call(
        _ffn_kernel,
        grid=(m // tm,),
        in_specs=[
            pl.BlockSpec((tm, D_MODEL), row),
            _layer_weight(wg_all, layer),
            _layer_weight(wu_all, layer),
            _layer_weight(wd_all, layer),
            pl.BlockSpec((1, D_MODEL), const),
            pl.BlockSpec((1, D_MODEL), const),
        ],
        out_specs=pl.BlockSpec((tm, D_MODEL), row),
        out_shape=jax.ShapeDtypeStruct((m, D_MODEL), F32),
        scratch_shapes=[pltpu.VMEM((tm, D_MODEL), F32)],
        compiler_params=_params(1),
        name="ffn",
    )(x2d, wg_all, wu_all, wd_all, g, b)


def _rope_tables(seq):
    half = DIFF_QK_DIM // 2
    pos = jnp.arange(seq, dtype=F32)
    inv_freq = ROPE_THETA ** (-jnp.arange(0, DIFF_QK_DIM, 2, dtype=F32) / DIFF_QK_DIM)
    ang = pos[:, None] * inv_freq[None, :]
    cos, sin = jnp.cos(ang), jnp.sin(ang)
    reps = LANES // DIFF_QK_DIM
    cos128 = jnp.tile(jnp.concatenate([cos, cos], axis=1), (1, reps))
    sin128 = jnp.tile(jnp.concatenate([-sin, sin], axis=1), (1, reps))
    assert cos128.shape == (seq, LANES) and half * 2 == DIFF_QK_DIM
    return cos128, sin128, cos128.T, sin128.T


def _dft_tables(seq):
    half = seq // 2
    j = np.arange(half, dtype=np.int64)[:, None]
    pos = np.arange(half, dtype=np.int64)[None, :]
    ang_even = 2.0 * np.pi * ((j * pos) % half).astype(np.float64) / half
    ang_odd = 2.0 * np.pi * ((j * (2 * pos + 1)) % seq).astype(np.float64) / seq
    f_even = np.concatenate([np.cos(ang_even), -np.sin(ang_even)], axis=1)
    f_odd = np.concatenate([np.cos(ang_odd), -np.sin(ang_odd)], axis=1)
    c = np.arange(FOURIER_GROUP_DIM, dtype=np.int64)
    ang_c = 2.0 * np.pi * ((c[:, None] * c[None, :]) % FOURIER_GROUP_DIM).astype(np.float64) / FOURIER_GROUP_DIM
    eye = np.eye(FOURIER_GROUPS)
    cc_bd = np.kron(eye, np.cos(ang_c))
    sc_bd = np.kron(eye, np.sin(ang_c))
    return (jnp.asarray(f_even, dtype=F32), jnp.asarray(f_odd, dtype=F32),
            jnp.asarray(cc_bd, dtype=F32).astype(BF16), jnp.asarray(sc_bd, dtype=F32).astype(BF16))


def _block_diag(blocks):
    n = len(blocks)
    rows = []
    for i, blk in enumerate(blocks):
        rows.append(jnp.concatenate(
            [blk if j == i else jnp.zeros((blk.shape[0], blocks[j].shape[1]), blk.dtype) for j in range(n)],
            axis=1))
    return jnp.concatenate(rows, axis=0)


def kernel(x, w_in, diff_lambda, diff_norm_g, fourier_w, gla_gate_w2, gla_gate_b2, gla_norm_g, w_out,
           ln1_g, ln1_b, ffn_w_gate, ffn_w_up, ffn_w_down, ln2_g, ln2_b):
    b, s, d = x.shape
    m = b * s
    rope = _rope_tables(s)
    f_even, f_odd, cc_bd, sc_bd = _dft_tables(s)
    x2d = x.reshape(m, d)
    for l in range(DEPTH):
        lam_init = 0.8 - 0.6 * math.exp(-0.3 * l)
        wqv_t = jnp.concatenate([w_in[l, :, OFF_DQ:OFF_DK], w_in[l, :, OFF_DV:OFF_FU]], axis=1).T.astype(BF16)
        w_rest = jnp.concatenate([w_in[l, :, OFF_DK:OFF_DV], w_in[l, :, OFF_FU:]], axis=1).astype(BF16)
        qt, k, vt, fu, gqk, gv, gr, gz = _in_proj(x2d, wqv_t, w_rest, rope, s)
        o_diff = _diff_attention(qt, k, vt, diff_lambda[l], diff_norm_g[l].reshape(1, -1), lam_init)
        w_four = _block_diag([fourier_w[l, g] for g in range(FOURIER_GROUPS)]).astype(BF16)
        o_four = _fourier(fu.reshape(-1, b, s, LANES), cc_bd, sc_bd, f_even, f_odd, w_four)
        w2_bd = _block_diag([gla_gate_w2[l, 0], gla_gate_w2[l, 1]]).astype(BF16)
        b2_cat = gla_gate_b2[l].reshape(1, -1)
        g_gla = jnp.tile(gla_norm_g[l], GLA_HEADS).reshape(1, -1)
        o_gla = _gla(gqk.reshape(b, s, -1), gv.reshape(b, s, -1), gr.reshape(b, s, -1),
                     gz.reshape(b, s, -1), w2_bd, b2_cat, g_gla)
        x2d = _out_proj(o_diff, o_four.reshape(m, -1), o_gla.reshape(m, -1), x2d,
                        w_out, l, ln1_g[l].reshape(1, -1), ln1_b[l].reshape(1, -1))
        x2d = _ffn(x2d, ffn_w_gate, ffn_w_up, ffn_w_down, l,
                   ln2_g[l].reshape(1, -1), ln2_b[l].reshape(1, -1))
    return x2d.reshape(b, s, d)
```

```python
import functools
import math

import jax
import jax.numpy as jnp
import numpy as np
from jax import lax
from jax.experimental import pallas as pl
from jax.experimental.pallas import tpu as pltpu

D_MODEL = 1024
DEPTH = 2
DIFF_HEADS = 4
DIFF_QK_DIM = 64
DIFF_V_DIM = 128
DIFF_WIDTH = 512
DIFF_QK_COLS = 512
ROPE_THETA = 10000.0
FOURIER_GROUPS = 4
FOURIER_GROUP_DIM = 64
FOURIER_WIDTH = 256
GLA_HEADS = 4
GLA_V_DIM = 64
GLA_K_DIM = 32
GLA_WIDTH = 256
GLA_K_COLS = 128
GLA_GATE_RANK = 16
GLA_GATE_TAU = 16.0
GLA_CHUNK = 64
FFN_HIDDEN = 2816
DEEPNORM_ALPHA = (2 * DEPTH) ** 0.25
LN_EPS = 1e-5

OFF_DQ = 0
OFF_DK = OFF_DQ + DIFF_QK_COLS
OFF_DV = OFF_DK + DIFF_QK_COLS
OFF_FU = OFF_DV + DIFF_WIDTH
OFF_GQ = OFF_FU + FOURIER_WIDTH
OFF_GV = OFF_GQ + 2 * GLA_K_COLS
OFF_GR = OFF_GV + GLA_WIDTH
OFF_GZ = OFF_GR + GLA_WIDTH
IN_WIDTH = OFF_GZ + 2 * GLA_GATE_RANK

LANES = 128
SUBLANES = 8
PACKED_SUBLANES = 16
VMEM_LIMIT = 56 * 1024 * 1024

BF16 = jnp.bfloat16
F32 = jnp.float32


def _dot(a, b):
    return jnp.dot(a, b, preferred_element_type=F32)


def _dot_nt(a, b):
    return lax.dot_general(a, b, (((1,), (1,)), ((), ())), preferred_element_type=F32)


def _params(n_grid_dims):
    return pltpu.CompilerParams(
        dimension_semantics=("arbitrary",) * n_grid_dims,
        vmem_limit_bytes=VMEM_LIMIT)


def _rope_slab(t, cos, sin_signed, first_half):
    half = DIFF_QK_DIM // 2
    swapped = jnp.where(first_half, pltpu.roll(t, LANES - half, 1), pltpu.roll(t, half, 1))
    return t * cos + swapped * sin_signed


ATTN_TQ = 256
ATTN_KC = 256


def _rest_col(col):
    assert col >= OFF_DK and not OFF_DV <= col < OFF_FU
    return col - OFF_DK if col < OFF_DV else col - OFF_DK - DIFF_WIDTH


def _inproj_kernel(x_ref, wqv_ref, w_ref, cos_ref, sin_ref, cost_ref, sint_ref,
                   qt_ref, k_ref, vt_ref, fu_ref, gqk_ref, gv_ref, gr_ref, gz_ref):
    xb = x_ref[...].astype(BF16)
    tm = xb.shape[0]
    cos = cos_ref[...]
    sin_signed = sin_ref[...]
    lane = lax.broadcasted_iota(jnp.int32, cos.shape, 1)
    first_half = (lane % DIFF_QK_DIM) < (DIFF_QK_DIM // 2)
    qk_scale = DIFF_QK_DIM ** -0.5 * math.log2(math.e)
    hqv_t = _dot_nt(wqv_ref[...], xb)
    cos_t = cost_ref[...]
    sin_t = sint_ref[...]
    half = DIFF_QK_DIM // 2
    wcols = lambda lo, hi: w_ref[:, _rest_col(lo):_rest_col(hi - 1) + 1]
    hk = _dot(xb, wcols(OFF_DK, OFF_DV))
    for hd in range(DIFF_HEADS):
        qt = hqv_t[hd * LANES:(hd + 1) * LANES]
        swapped = jnp.concatenate(
            [qt[half:2 * half], qt[0:half], qt[3 * half:4 * half], qt[2 * half:3 * half]], axis=0)
        qt = ((qt * cos_t + swapped * sin_t) * qk_scale).astype(BF16)
        vt = hqv_t[DIFF_QK_COLS + hd * LANES:DIFF_QK_COLS + (hd + 1) * LANES].astype(BF16)
        for j in range(tm // ATTN_TQ):
            qt_ref[0, hd, j] = qt[:, j * ATTN_TQ:(j + 1) * ATTN_TQ]
        for j in range(tm // ATTN_KC):
            vt_ref[0, hd, j] = vt[:, j * ATTN_KC:(j + 1) * ATTN_KC]
        slab = slice(hd * LANES, (hd + 1) * LANES)
        k_ref[0, hd] = _rope_slab(hk[:, slab], cos, sin_signed, first_half).astype(BF16)
    hfu = _dot(xb, wcols(OFF_FU, OFF_GQ))
    for lane_half in range(FOURIER_WIDTH // LANES):
        fu_ref[lane_half] = hfu[:, lane_half * LANES:(lane_half + 1) * LANES]
    gqk_ref[...] = _dot(xb, wcols(OFF_GQ, OFF_GV))
    gv_ref[...] = _dot(xb, wcols(OFF_GV, OFF_GR))
    gr_ref[...] = _dot(xb, wcols(OFF_GR, OFF_GZ))
    gz_ref[...] = _dot(xb, wcols(OFF_GZ, IN_WIDTH))


def _layer_weight(w_all, layer):
    return pl.BlockSpec((None,) + w_all.shape[1:], lambda i: (layer,) + (0,) * (w_all.ndim - 1),
                        pipeline_mode=pl.Buffered(1))


def _in_proj(x2d, wqv_t, w_rest, rope, seq, tm=512):
    m = x2d.shape[0]
    batch = m // seq
    n_seq_tiles = seq // tm
    cos128, sin128, cos_t, sin_t = rope
    row = lambda i: (i, 0)
    const = lambda i: (0, 0)
    tab = lambda i: (i % n_seq_tiles, 0)
    tab_t = lambda i: (0, i % n_seq_tiles)
    tiled = lambda t: (jax.ShapeDtypeStruct((batch, DIFF_HEADS, seq // t, LANES, t), BF16),
                       pl.BlockSpec((1, DIFF_HEADS, tm // t, LANES, t),
                                    lambda i: (i // n_seq_tiles, 0, i % n_seq_tiles, 0, 0)))
    qt_shape, qt_spec = tiled(ATTN_TQ)
    vt_shape, vt_spec = tiled(ATTN_KC)
    k_shape = jax.ShapeDtypeStruct((batch, DIFF_HEADS, seq, LANES), BF16)
    k_spec = pl.BlockSpec((1, DIFF_HEADS, tm, LANES), lambda i: (i // n_seq_tiles, 0, i % n_seq_tiles, 0))
    fu_shape = jax.ShapeDtypeStruct((FOURIER_WIDTH // LANES, m, LANES), F32)
    fu_spec = pl.BlockSpec((FOURIER_WIDTH // LANES, tm, LANES), lambda i: (0, i, 0))
    flat_shapes = (
        jax.ShapeDtypeStruct((m, 2 * GLA_K_COLS), F32),
        jax.ShapeDtypeStruct((m, GLA_WIDTH), F32),
        jax.ShapeDtypeStruct((m, GLA_WIDTH), F32),
        jax.ShapeDtypeStruct((m, 2 * GLA_GATE_RANK), F32),
    )
    return pl.pallas_call(
        _inproj_kernel,
        grid=(m // tm,),
        in_specs=[
            pl.BlockSpec((tm, D_MODEL), row),
            pl.BlockSpec((DIFF_QK_COLS + DIFF_WIDTH, D_MODEL), const),
            pl.BlockSpec(w_rest.shape, const),
            pl.BlockSpec((tm, LANES), tab),
            pl.BlockSpec((tm, LANES), tab),
            pl.BlockSpec((LANES, tm), tab_t),
            pl.BlockSpec((LANES, tm), tab_t),
        ],
        out_specs=(qt_spec, k_spec, vt_spec, fu_spec)
        + tuple(pl.BlockSpec((tm, s.shape[1]), row) for s in flat_shapes),
        out_shape=(qt_shape, k_shape, vt_shape, fu_shape) + flat_shapes,
        compiler_params=_params(1),
        name="in_proj",
    )(x2d, wqv_t, w_rest, cos128, sin128, cos_t, sin_t)


ATTN_SUB = 128
ATTN_GROUP = 4


def _attn_kernel(lam_ref, g_ref, qt_ref, k_ref, vt_ref, o_ref,
                 sa_ref, sb_ref, pa_ref, pb_ref, mx_ref, coef_ref, ot_ref, *, lam_init):
    tq, kc, sub = ATTN_TQ, ATTN_KC, ATTN_SUB
    seq = k_ref.shape[2]
    tiles = seq // tq
    n_units = DIFF_HEADS * tiles
    n_chunks = seq // kc
    lp = lam_ref[...]
    lam = (jnp.exp(jnp.sum(lp[0:1] * lp[1:2], axis=1, keepdims=True))
           - jnp.exp(jnp.sum(lp[2:3] * lp[3:4], axis=1, keepdims=True)) + lam_init)
    gain = g_ref[...] * (1.0 - lam_init)
    feature = lax.broadcasted_iota(jnp.int32, (LANES, tq), 0)
    first_map = feature < DIFF_QK_DIM
    s_bufs = (sa_ref, sb_ref)
    p_bufs = (pa_ref, pb_ref)

    n_groups = n_chunks // ATTN_GROUP
    slabs_per_trip = (tq // LANES) // n_groups
    assert slabs_per_trip * n_groups * LANES == tq

    def step(k, par, do_a=True, do_b=True, do_c=True, do_d=True):
        s_w, s_r = s_bufs[par], s_bufs[1 - par]
        p_w, p_r = p_bufs[1 - par], p_bufs[par]
        if do_d:
            h_d = (k - 3) // tiles
            q_base = ((k - 3) % tiles) * tq
        if do_a:
            h_a = k // tiles
            qt = qt_ref[0, h_a, k % tiles]
            zero = jnp.zeros_like(qt)
            qw = (jnp.where(first_map, qt, zero), jnp.where(first_map, zero, qt))
        if do_b:
            m8 = (mx_ref[1 - par, 0], mx_ref[1 - par, 1])
        if do_c:
            h_c = (k - 2) // tiles
            c16 = (coef_ref[par, 0], coef_ref[par, 1])

        def chunk(c, carry):
            mx = [carry[0], carry[1]]
            ls = [carry[2], carry[3]]
            acc = carry[4]
            for part in range(kc // sub):
                rows = pl.ds(pl.multiple_of(c * kc + part * sub, sub), sub)
                if do_a:
                    keys = k_ref[0, h_a, rows, :]
                    for m in range(2):
                        st = _dot(keys, qw[m])
                        s_w[m, rows, :] = st
                        mx[m] = jnp.maximum(mx[m], jnp.max(st.reshape(sub // SUBLANES, SUBLANES, tq), axis=0))
                if do_b:
                    for m in range(2):
                        p = jnp.exp2(s_r[m, rows, :].reshape(sub // SUBLANES, SUBLANES, tq) - m8[m][None])
                        ls[m] = ls[m] + jnp.sum(p, axis=0)
                        p_w[m, rows, :] = p.reshape(sub, tq).astype(BF16)
            if do_c:
                rows = pl.ds(pl.multiple_of(c * kc, kc), kc)
                packed = (kc // PACKED_SUBLANES, PACKED_SUBLANES, tq)
                a = p_r[0, rows, :].reshape(packed) * c16[0][None] - p_r[1, rows, :].reshape(packed) * c16[1][None]
                part_o = _dot(vt_ref[0, h_c, c], a.reshape(kc, tq))
                acc = part_o if acc is None else acc + part_o
            return mx[0], mx[1], ls[0], ls[1], acc

        lowest = jnp.full((SUBLANES, tq), -jnp.inf, F32)
        nothing = jnp.zeros((SUBLANES, tq), F32)
        def group(gi, carry):
            if do_d:
                for slab in range(slabs_per_trip):
                    si = gi * slabs_per_trip + slab
                    o = ot_ref[1 - par, si].T
                    ms = jnp.mean(o * o, axis=1, keepdims=True)
                    q_start = q_base + si * LANES
                    q_rows = pl.ds(q_start if isinstance(q_start, int) else pl.multiple_of(q_start, LANES),
                                   LANES)
                    o_ref[0, h_d, q_rows, :] = (o * lax.rsqrt(ms + LN_EPS) * gain).astype(BF16)
            stats = carry + (None,)
            for c in range(ATTN_GROUP):
                stats = chunk(gi * ATTN_GROUP + c, stats)
            if do_c:
                for si in range(tq // LANES):
                    ot_ref[par, si] += stats[4][:, si * LANES:(si + 1) * LANES]
            return stats[:4]

        if do_c:
            ot_ref[par] = jnp.zeros(ot_ref.shape[1:], F32)
        mx0, mx1, ls0, ls1 = lax.fori_loop(0, n_groups, group, (lowest, lowest, nothing, nothing))
        if do_a:
            mx_ref[par, 0] = jnp.broadcast_to(jnp.max(mx0, axis=0, keepdims=True), (SUBLANES, tq))
            mx_ref[par, 1] = jnp.broadcast_to(jnp.max(mx1, axis=0, keepdims=True), (SUBLANES, tq))
        if do_b:
            l0 = jnp.sum(ls0, axis=0, keepdims=True)
            l1 = jnp.sum(ls1, axis=0, keepdims=True)
            coef_ref[1 - par, 0] = jnp.broadcast_to(1.0 / l0, (PACKED_SUBLANES, tq)).astype(BF16)
            coef_ref[1 - par, 1] = jnp.broadcast_to(lam / l1, (PACKED_SUBLANES, tq)).astype(BF16)

    def step_pair(j, carry):
        step(2 * j, 0)
        step(2 * j + 1, 1)
        return carry

    step(0, 0, do_b=False, do_c=False, do_d=False)
    step(1, 1, do_c=False, do_d=False)
    step(2, 0, do_d=False)
    step(3, 1)
    lax.fori_loop(2, n_units // 2, step_pair, 0)
    step(n_units, 0, do_a=False)
    step(n_units + 1, 1, do_a=False, do_b=False)
    step(n_units + 2, 0, do_a=False, do_b=False, do_c=False)


def _diff_attention(qt, k, vt, lam_params, g, lam_init):
    b, _, s, _ = k.shape
    whole = lambda a: pl.BlockSpec((1,) + a.shape[1:], lambda bi: (bi,) + (0,) * (a.ndim - 1))
    return pl.pallas_call(
        functools.partial(_attn_kernel, lam_init=lam_init),
        grid=(b,),
        in_specs=[
            pl.BlockSpec((4, DIFF_QK_DIM), lambda bi: (0, 0)),
            pl.BlockSpec((1, DIFF_V_DIM), lambda bi: (0, 0)),
            whole(qt), whole(k), whole(vt),
        ],
        out_specs=whole(k),
        out_shape=jax.ShapeDtypeStruct((b, DIFF_HEADS, s, LANES), BF16),
        scratch_shapes=[
            pltpu.VMEM((2, s, ATTN_TQ), F32), pltpu.VMEM((2, s, ATTN_TQ), F32),
            pltpu.VMEM((2, s, ATTN_TQ), BF16), pltpu.VMEM((2, s, ATTN_TQ), BF16),
            pltpu.VMEM((2, 2, SUBLANES, ATTN_TQ), F32),
            pltpu.VMEM((2, 2, PACKED_SUBLANES, ATTN_TQ), BF16),
            pltpu.VMEM((2, ATTN_TQ // LANES, DIFF_V_DIM, LANES), F32),
        ],
        compiler_params=_params(1),
        name="diff_attn",
    )(lam_params, g, qt, k, vt)


FOURIER_PACK = 2


def _fourier_kernel(u_ref, cc_ref, sc_ref, fe_ref, fo_ref, w_ref, o_ref, ab_ref, *, scale):
    s = u_ref.shape[2]
    half = s // 2
    w = FOURIER_WIDTH
    for e in range(FOURIER_PACK):
        lanes = slice(e * w, (e + 1) * w)
        for parity in range(2):
            u = jnp.concatenate([u_ref[lh, e, pl.ds(parity, half, stride=2), :] for lh in range(2)],
                                axis=1).astype(BF16)
            ab_ref[parity, 0:half, lanes] = _dot(u, cc_ref[...]).astype(BF16)
            ab_ref[parity, half:s, lanes] = _dot(u, sc_ref[...]).astype(BF16)
    even = _dot(fe_ref[...].astype(BF16), ab_ref[0])
    odd = _dot(fo_ref[...].astype(BF16), ab_ref[1])
    for rows, z in ((slice(0, half), (even + odd) * scale), (slice(half, s), (even - odd) * scale)):
        for e in range(FOURIER_PACK):
            y = _dot(z[:, e * w:(e + 1) * w].astype(BF16), w_ref[...])
            o_ref[e, rows, :] = y.astype(BF16)


def _fourier(u, cc_bd, sc_bd, f_even, f_odd, w_bd):
    _, b, s, _ = u.shape
    pack = FOURIER_PACK
    scale = 1.0 / math.sqrt(s * FOURIER_GROUP_DIM)
    const2 = lambda bi: (0, 0)
    table = pl.BlockSpec((s // 2, s), const2, pipeline_mode=pl.Buffered(1))
    return pl.pallas_call(
        functools.partial(_fourier_kernel, scale=scale),
        grid=(b // pack,),
        in_specs=[
            pl.BlockSpec((2, pack, s, LANES), lambda bi: (0, bi, 0, 0)),
            pl.BlockSpec((FOURIER_WIDTH, FOURIER_WIDTH), const2),
            pl.BlockSpec((FOURIER_WIDTH, FOURIER_WIDTH), const2),
            table,
            table,
            pl.BlockSpec((FOURIER_WIDTH, FOURIER_WIDTH), const2),
        ],
        out_specs=pl.BlockSpec((pack, s, FOURIER_WIDTH), lambda bi: (bi, 0, 0)),
        out_shape=jax.ShapeDtypeStruct((b, s, FOURIER_WIDTH), BF16),
        scratch_shapes=[pltpu.VMEM((2, s, pack * FOURIER_WIDTH), BF16)],
        compiler_params=_params(1),
        name="fourier",
    )(u, cc_bd, sc_bd, f_even, f_odd, w_bd)


def _split3(g):
    hi = g.astype(BF16)
    r1 = g - hi.astype(F32)
    mid = r1.astype(BF16)
    lo = (r1 - mid.astype(F32)).astype(BF16)
    return hi, mid, lo


def _gla_kernel(qk_ref, v_ref, r_ref, z_ref, w2_ref, b2_ref, g_ref, o_ref,
                la_ref, vt_ref, acc_ref, qt_ref, kt_ref, a_ref, kv_ref, dec_ref, st_ref):
    s = v_ref.shape[1]
    c = GLA_CHUNK
    pair = 2 * c
    n_pairs = s // pair
    n_chunks = s // c
    kc = GLA_K_COLS

    logit = _dot(z_ref[0].astype(BF16), w2_ref[...]) + b2_ref[...]
    la_ref[...] = (jnp.minimum(logit, 0.0) - jnp.log(1.0 + jnp.exp(-jnp.abs(logit)))) * (1.0 / GLA_GATE_TAU)
    vt_ref[...] = v_ref[0].T.astype(BF16)

    row_i = lax.broadcasted_iota(jnp.int32, (pair, pair), 0)
    col_i = lax.broadcasted_iota(jnp.int32, (pair, pair), 1)
    same_chunk = (row_i // c) == (col_i // c)
    cum_mats = ((same_chunk & (col_i <= row_i)).astype(BF16), (same_chunk & (col_i >= row_i)).astype(BF16))
    r_hk = lax.broadcasted_iota(jnp.int32, (GLA_HEADS * c, kc), 0) // c
    l_hk = lax.broadcasted_iota(jnp.int32, (GLA_HEADS * c, kc), 1) // GLA_K_DIM
    blk_k = r_hk == l_hk
    r_hv = lax.broadcasted_iota(jnp.int32, (GLA_HEADS * c, GLA_WIDTH), 0) // c
    l_hv = lax.broadcasted_iota(jnp.int32, (GLA_HEADS * c, GLA_WIDTH), 1) // GLA_V_DIM
    blk_v = r_hv == l_hv
    qi = lax.broadcasted_iota(jnp.int32, (c, GLA_HEADS * c), 0)
    kj = lax.broadcasted_iota(jnp.int32, (c, GLA_HEADS * c), 1) % c
    q_scale = GLA_K_DIM ** -0.5

    causal = (kj <= qi, kj >= qi)

    def pair_rows(p):
        start = p * pair
        return pl.ds(start if isinstance(start, int) else pl.multiple_of(start, pair), pair)

    def decayed_qk(p):
        rows = pair_rows(p)
        q2 = qk_ref[0, rows, 0:kc] * q_scale
        k2 = qk_ref[0, rows, kc:2 * kc]
        for d in range(2):
            hi, mid, lo = _split3(la_ref[rows, d * kc:(d + 1) * kc])
            c3 = _dot(cum_mats[d], jnp.concatenate([hi, mid, lo], axis=1))
            bcum = c3[:, 0:kc] + c3[:, kc:2 * kc] + c3[:, 2 * kc:3 * kc]
            eb = jnp.exp(bcum)
            qt_ref[d, rows, :] = (q2 * eb).astype(BF16)
            kt_ref[d, rows, :] = k2 * jnp.exp(-bcum)
            for ci in range(2):
                edge = ci * c if d == 1 else (ci + 1) * c - 1
                dec_ref[d, 2 * p + ci] = jnp.broadcast_to(eb[edge:edge + 1], (SUBLANES, kc))

    def intra_scores(p):
        for ci in range(2):
            start = p * pair + ci * c
            rows = pl.ds(start if isinstance(start, int) else pl.multiple_of(start, c), c)
            for d in range(2):
                kt_bd = jnp.where(blk_k, jnp.concatenate([kt_ref[d, rows, :]] * GLA_HEADS, axis=0), 0.0)
                a = _dot_nt(qt_ref[d, rows, :], kt_bd.astype(BF16))
                a_ref[2 * p + ci, d * c:(d + 1) * c, :] = jnp.where(causal[d], a, 0.0).astype(BF16)

    def intra_out_and_kv(p):
        rows2 = pair_rows(p)
        v2 = v_ref[0, rows2, :]
        vt2 = vt_ref[:, rows2]
        kt2 = jnp.concatenate([kt_ref[0, rows2, :], kt_ref[1, rows2, :]], axis=1)
        for ci in range(2):
            sl = slice(ci * c, (ci + 1) * c)
            start = p * pair + ci * c
            rows = pl.ds(start if isinstance(start, int) else pl.multiple_of(start, c), c)
            v_bd = jnp.where(blk_v, jnp.concatenate([v2[sl]] * GLA_HEADS, axis=0), 0.0).astype(BF16)
            o_both = _dot(a_ref[2 * p + ci], v_bd)
            acc_ref[rows, :] = o_both[0:c] + o_both[c:2 * c]
            gap = jnp.zeros((c, 2 * kc), F32)
            kt_pair = jnp.concatenate([kt2[sl], gap] if ci == 0 else [gap, kt2[sl]], axis=0).astype(BF16)
            kv = _dot(vt2, kt_pair)
            kv_ref[0, 2 * p + ci] = jnp.where(blk_k, kv[:, 0:kc], 0.0)
            kv_ref[1, 2 * p + ci] = jnp.where(blk_k, kv[:, kc:2 * kc], 0.0)

    decayed_qk(0)
    intra_scores(0)
    decayed_qk(1)

    def intra_step(p, carry):
        intra_out_and_kv(p)
        intra_scores(p + 1)
        decayed_qk(p + 2)
        return carry

    lax.fori_loop(0, n_pairs - 2, intra_step, 0, unroll=2)
    intra_out_and_kv(n_pairs - 2)
    intra_scores(n_pairs - 1)
    intra_out_and_kv(n_pairs - 1)

    st_ref[...] = jnp.zeros_like(st_ref)

    def scan_step(i, carry):
        for d in range(2):
            n = i if d == 0 else n_chunks - 1 - i
            rows = pl.ds(pl.multiple_of(n * c, c), c)
            st = st_ref[d]
            acc_ref[rows, :] += _dot_nt(qt_ref[d, rows, :], st.astype(BF16))
            st = (st + kv_ref[d, n]).reshape(GLA_WIDTH // SUBLANES, SUBLANES, kc) * dec_ref[d, n][None]
            st_ref[d] = st.reshape(GLA_WIDTH, kc)
        return carry

    lax.fori_loop(0, n_chunks, scan_step, 0, unroll=8)

    o = acc_ref[...]
    o2 = o * o
    o2_hi = o2.astype(BF16)
    o2_lo = (o2 - o2_hi.astype(F32)).astype(BF16)
    gi = lax.broadcasted_iota(jnp.int32, (GLA_WIDTH, GLA_WIDTH), 0) // GLA_V_DIM
    gj = lax.broadcasted_iota(jnp.int32, (GLA_WIDTH, GLA_WIDTH), 1) // GLA_V_DIM
    ones_bd = (gi == gj).astype(BF16)
    ms = (_dot(o2_hi, ones_bd) + _dot(o2_lo, ones_bd)) * (1.0 / GLA_V_DIM)
    r = r_ref[0]
    gate = r * (1.0 / (1.0 + jnp.exp(-r)))
    o_ref[0] = (o * lax.rsqrt(ms + LN_EPS) * g_ref[...] * gate).astype(BF16)


def _gla(gqk, gv, gr, gz, w2_bd, b2_cat, g_tiled):
    b, s, _ = gv.shape
    per_b = lambda bi: (bi, 0, 0)
    const = lambda bi: (0, 0)
    return pl.pallas_call(
        _gla_kernel,
        grid=(b,),
        in_specs=[
            pl.BlockSpec((1, s, 2 * GLA_K_COLS), per_b),
            pl.BlockSpec((1, s, GLA_WIDTH), per_b),
            pl.BlockSpec((1, s, GLA_WIDTH), per_b),
            pl.BlockSpec((1, s, 2 * GLA_GATE_RANK), per_b),
            pl.BlockSpec((2 * GLA_GATE_RANK, 2 * GLA_K_COLS), const),
            pl.BlockSpec((1, 2 * GLA_K_COLS), const),
            pl.BlockSpec((1, GLA_WIDTH), const),
        ],
        out_specs=pl.BlockSpec((1, s, GLA_WIDTH), per_b),
        out_shape=jax.ShapeDtypeStruct((b, s, GLA_WIDTH), BF16),
        scratch_shapes=[
            pltpu.VMEM((s, 2 * GLA_K_COLS), F32),
            pltpu.VMEM((GLA_WIDTH, s), BF16),
            pltpu.VMEM((s, GLA_WIDTH), F32),
            pltpu.VMEM((2, s, GLA_K_COLS), BF16),
            pltpu.VMEM((2, s, GLA_K_COLS), F32),
            pltpu.VMEM((s // GLA_CHUNK, 2 * GLA_CHUNK, GLA_WIDTH), BF16),
            pltpu.VMEM((2, s // GLA_CHUNK, GLA_WIDTH, GLA_K_COLS), F32),
            pltpu.VMEM((2, s // GLA_CHUNK, SUBLANES, GLA_K_COLS), F32),
            pltpu.VMEM((2, GLA_WIDTH, GLA_K_COLS), F32),
        ],
        compiler_params=_params(1),
        name="gla",
    )(gqk, gv, gr, gz, w2_bd, b2_cat, g_tiled)


def _layer_norm(y, g, b):
    mu = jnp.mean(y, axis=1, keepdims=True)
    d = y - mu
    var = jnp.mean(d * d, axis=1, keepdims=True)
    return d * lax.rsqrt(var + LN_EPS) * g + b


ROW_SUB = 256


def _outproj_kernel(od_ref, of_ref, og_ref, x_ref, w_ref, g_ref, b_ref, o_ref):
    w = w_ref[...].astype(BF16)
    for r in range(x_ref.shape[0] // ROW_SUB):
        rows = slice(r * ROW_SUB, (r + 1) * ROW_SUB)
        mixed = jnp.concatenate([od_ref[0, hd, rows, :] for hd in range(DIFF_HEADS)]
                                + [of_ref[rows, :], og_ref[rows, :]], axis=1)
        m = _dot(mixed, w)
        o_ref[rows, :] = _layer_norm(DEEPNORM_ALPHA * x_ref[rows, :] + m, g_ref[...], b_ref[...])


def _out_proj(o_diff, o_four, o_gla, x2d, w_all, layer, g, b, tm=1024):
    m = x2d.shape[0]
    seq = o_diff.shape[2]
    n_seq_tiles = seq // tm
    row = lambda i: (i, 0)
    const = lambda i: (0, 0)
    return pl.pallas_call(
        _outproj_kernel,
        grid=(m // tm,),
        in_specs=[
            pl.BlockSpec((1, DIFF_HEADS, tm, LANES), lambda i: (i // n_seq_tiles, 0, i % n_seq_tiles, 0)),
            pl.BlockSpec((tm, FOURIER_WIDTH), row),
            pl.BlockSpec((tm, GLA_WIDTH), row),
            pl.BlockSpec((tm, D_MODEL), row),
            _layer_weight(w_all, layer),
            pl.BlockSpec((1, D_MODEL), const),
            pl.BlockSpec((1, D_MODEL), const),
        ],
        out_specs=pl.BlockSpec((tm, D_MODEL), row),
        out_shape=jax.ShapeDtypeStruct((m, D_MODEL), F32),
        compiler_params=_params(1),
        name="out_proj",
    )(o_diff, o_four, o_gla, x2d, w_all, g, b)


FFN_CHUNK = 256


def _ffn_kernel(x_ref, wg_ref, wu_ref, wd_ref, g_ref, b_ref, o_ref, acc_ref):
    x = x_ref[...]
    xb = x.astype(BF16)
    for ci in range(FFN_HIDDEN // FFN_CHUNK):
        cols = slice(ci * FFN_CHUNK, (ci + 1) * FFN_CHUNK)
        hg = _dot(xb, wg_ref[:, cols].astype(BF16))
        hu = _dot(xb, wu_ref[:, cols].astype(BF16))
        act = (hg * (1.0 / (1.0 + jnp.exp(-hg))) * hu).astype(BF16)
        part = _dot(act, wd_ref[cols, :].astype(BF16))
        if ci == 0:
            acc_ref[...] = part
        else:
            acc_ref[...] += part
    o_ref[...] = _layer_norm(DEEPNORM_ALPHA * x + acc_ref[...], g_ref[...], b_ref[...])


def _ffn(x2d, wg_all, wu_all, wd_all, layer, g, b, tm=512):
    m = x2d.shape[0]
    row = lambda i: (i, 0)
    const = lambda i: (0, 0)
    return pl.pallas_call(
        _ffn_kernel,
        grid=(m // tm,),
        in_specs=[
            pl.BlockSpec((tm, D_MODEL), row),
            _layer_weight(wg_all, layer),
            _layer_weight(wu_all, layer),
            _layer_weight(wd_all, layer),
            pl.BlockSpec((1, D_MODEL), const),
            pl.BlockSpec((1, D_MODEL), const),
        ],
        out_specs=pl.BlockSpec((tm, D_MODEL), row),
        out_shape=jax.ShapeDtypeStruct((m, D_MODEL), F32),
        scratch_shapes=[pltpu.VMEM((tm, D_MODEL), F32)],
        compiler_params=_params(1),
        name="ffn",
    )(x2d, wg_all, wu_all, wd_all, g, b)


def _rope_tables(seq):
    half = DIFF_QK_DIM // 2
    pos = jnp.arange(seq, dtype=F32)
    inv_freq = ROPE_THETA ** (-jnp.arange(0, DIFF_QK_DIM, 2, dtype=F32) / DIFF_QK_DIM)
    ang = pos[:, None] * inv_freq[None, :]
    cos, sin = jnp.cos(ang), jnp.sin(ang)
    reps = LANES // DIFF_QK_DIM
    cos128 = jnp.tile(jnp.concatenate([cos, cos], axis=1), (1, reps))
    sin128 = jnp.tile(jnp.concatenate([-sin, sin], axis=1), (1, reps))
    assert cos128.shape == (seq, LANES) and half * 2 == DIFF_QK_DIM
    return cos128, sin128, cos128.T, sin128.T


def _dft_tables(seq):
    half = seq // 2
    j = np.arange(half, dtype=np.int64)[:, None]
    pos = np.arange(half, dtype=np.int64)[None, :]
    ang_even = 2.0 * np.pi * ((j * pos) % half).astype(np.float64) / half
    ang_odd = 2.0 * np.pi * ((j * (2 * pos + 1)) % seq).astype(np.float64) / seq
    f_even = np.concatenate([np.cos(ang_even), -np.sin(ang_even)], axis=1)
    f_odd = np.concatenate([np.cos(ang_odd), -np.sin(ang_odd)], axis=1)
    c = np.arange(FOURIER_GROUP_DIM, dtype=np.int64)
    ang_c = 2.0 * np.pi * ((c[:, None] * c[None, :]) % FOURIER_GROUP_DIM).astype(np.float64) / FOURIER_GROUP_DIM
    eye = np.eye(FOURIER_GROUPS)
    cc_bd = np.kron(eye, np.cos(ang_c))
    sc_bd = np.kron(eye, np.sin(ang_c))
    return (jnp.asarray(f_even, dtype=F32), jnp.asarray(f_odd, dtype=F32),
            jnp.asarray(cc_bd, dtype=F32).astype(BF16), jnp.asarray(sc_bd, dtype=F32).astype(BF16))


def _block_diag(blocks):
    n = len(blocks)
    rows = []
    for i, blk in enumerate(blocks):
        rows.append(jnp.concatenate(
            [blk if j == i else jnp.zeros((blk.shape[0], blocks[j].shape[1]), blk.dtype) for j in range(n)],
            axis=1))
    return jnp.concatenate(rows, axis=0)


def kernel(x, w_in, diff_lambda, diff_norm_g, fourier_w, gla_gate_w2, gla_gate_b2, gla_norm_g, w_out,
           ln1_g, ln1_b, ffn_w_gate, ffn_w_up, ffn_w_down, ln2_g, ln2_b):
    b, s, d = x.shape
    m = b * s
    rope = _rope_tables(s)
    f_even, f_odd, cc_bd, sc_bd = _dft_tables(s)
    x2d = x.reshape(m, d)
    for l in range(DEPTH):
        lam_init = 0.8 - 0.6 * math.exp(-0.3 * l)
        wqv_t = jnp.concatenate([w_in[l, :, OFF_DQ:OFF_DK], w_in[l, :, OFF_DV:OFF_FU]], axis=1).T.astype(BF16)
        w_rest = jnp.concatenate([w_in[l, :, OFF_DK:OFF_DV], w_in[l, :, OFF_FU:]], axis=1).astype(BF16)
        qt, k, vt, fu, gqk, gv, gr, gz = _in_proj(x2d, wqv_t, w_rest, rope, s)
        o_diff = _diff_attention(qt, k, vt, diff_lambda[l], diff_norm_g[l].reshape(1, -1), lam_init)
        w_four = _block_diag([fourier_w[l, g] for g in range(FOURIER_GROUPS)]).astype(BF16)
        o_four = _fourier(fu.reshape(-1, b, s, LANES), cc_bd, sc_bd, f_even, f_odd, w_four)
        w2_bd = _block_diag([gla_gate_w2[l, 0], gla_gate_w2[l, 1]]).astype(BF16)
        b2_cat = gla_gate_b2[l].reshape(1, -1)
        g_gla = jnp.tile(gla_norm_g[l], GLA_HEADS).reshape(1, -1)
        o_gla = _gla(gqk.reshape(b, s, -1), gv.reshape(b, s, -1), gr.reshape(b, s, -1),
                     gz.reshape(b, s, -1), w2_bd, b2_cat, g_gla)
        x2d = _out_proj(o_diff, o_four.reshape(m, -1), o_gla.reshape(m, -1), x2d,
                        w_out, l, ln1_g[l].reshape(1, -1), ln1_b[l].reshape(1, -1))
        x2d = _ffn(x2d, ffn_w_gate, ffn_w_up, ffn_w_down, l,
                   ln2_g[l].reshape(1, -1), ln2_b[l].reshape(1, -1))
    return x2d.reshape(b, s, d)
```

```python
import functools
import math

import jax
import jax.numpy as jnp
import numpy as np
from jax import lax
from jax.experimental import pallas as pl
from jax.experimental.pallas import tpu as pltpu

D_MODEL = 1024
DEPTH = 2
DIFF_HEADS = 4
DIFF_QK_DIM = 64
DIFF_V_DIM = 128
DIFF_WIDTH = 512
DIFF_QK_COLS = 512
ROPE_THETA = 10000.0
FOURIER_GROUPS = 4
FOURIER_GROUP_DIM = 64
FOURIER_WIDTH = 256
GLA_HEADS = 4
GLA_V_DIM = 64
GLA_K_DIM = 32
GLA_WIDTH = 256
GLA_K_COLS = 128
GLA_GATE_RANK = 16
GLA_GATE_TAU = 16.0
GLA_CHUNK = 64
FFN_HIDDEN = 2816
DEEPNORM_ALPHA = (2 * DEPTH) ** 0.25
LN_EPS = 1e-5

OFF_DQ = 0
OFF_DK = OFF_DQ + DIFF_QK_COLS
OFF_DV = OFF_DK + DIFF_QK_COLS
OFF_FU = OFF_DV + DIFF_WIDTH
OFF_GQ = OFF_FU + FOURIER_WIDTH
OFF_GV = OFF_GQ + 2 * GLA_K_COLS
OFF_GR = OFF_GV + GLA_WIDTH
OFF_GZ = OFF_GR + GLA_WIDTH
IN_WIDTH = OFF_GZ + 2 * GLA_GATE_RANK

LANES = 128
SUBLANES = 8
PACKED_SUBLANES = 16
VMEM_LIMIT = 56 * 1024 * 1024
FFN_VMEM_LIMIT = 60 * 1024 * 1024

BF16 = jnp.bfloat16
F32 = jnp.float32


def _dot(a, b):
    return jnp.dot(a, b, preferred_element_type=F32)


def _dot_nt(a, b):
    return lax.dot_general(a, b, (((1,), (1,)), ((), ())), preferred_element_type=F32)


def _params(n_grid_dims, vmem_limit=VMEM_LIMIT):
    return pltpu.CompilerParams(
        dimension_semantics=("arbitrary",) * n_grid_dims,
        vmem_limit_bytes=vmem_limit)


def _rope_slab(t, cos, sin_signed, first_half):
    half = DIFF_QK_DIM // 2
    swapped = jnp.where(first_half, pltpu.roll(t, LANES - half, 1), pltpu.roll(t, half, 1))
    return t * cos + swapped * sin_signed


ATTN_TQ = 256
ATTN_KC = 256


def _rest_col(col):
    assert col >= OFF_DK and not OFF_DV <= col < OFF_FU
    return col - OFF_DK if col < OFF_DV else col - OFF_DK - DIFF_WIDTH


def _inproj_kernel(x_ref, wqv_ref, w_ref, cos_ref, sin_ref, cost_ref, sint_ref,
                   qt_ref, k_ref, vt_ref, fu_ref, gqk_ref, gv_ref, gr_ref, gz_ref):
    xb = x_ref[...].astype(BF16)
    tm = xb.shape[0]
    cos = cos_ref[...]
    sin_signed = sin_ref[...]
    lane = lax.broadcasted_iota(jnp.int32, cos.shape, 1)
    first_half = (lane % DIFF_QK_DIM) < (DIFF_QK_DIM // 2)
    qk_scale = DIFF_QK_DIM ** -0.5 * math.log2(math.e)
    hqv_t = _dot_nt(wqv_ref[...], xb)
    cos_t = cost_ref[...]
    sin_t = sint_ref[...]
    half = DIFF_QK_DIM // 2
    wcols = lambda lo, hi: w_ref[:, _rest_col(lo):_rest_col(hi - 1) + 1]
    hk = _dot(xb, wcols(OFF_DK, OFF_DV))
    for hd in range(DIFF_HEADS):
        qt = hqv_t[hd * LANES:(hd + 1) * LANES]
        swapped = jnp.concatenate(
            [qt[half:2 * half], qt[0:half], qt[3 * half:4 * half], qt[2 * half:3 * half]], axis=0)
        qt = ((qt * cos_t + swapped * sin_t) * qk_scale).astype(BF16)
        vt = hqv_t[DIFF_QK_COLS + hd * LANES:DIFF_QK_COLS + (hd + 1) * LANES].astype(BF16)
        for j in range(tm // ATTN_TQ):
            qt_ref[0, hd, j] = qt[:, j * ATTN_TQ:(j + 1) * ATTN_TQ]
        for j in range(tm // ATTN_KC):
            vt_ref[0, hd, j] = vt[:, j * ATTN_KC:(j + 1) * ATTN_KC]
        slab = slice(hd * LANES, (hd + 1) * LANES)
        k_ref[0, hd] = _rope_slab(hk[:, slab], cos, sin_signed, first_half).astype(BF16)
    hfu = _dot(xb, wcols(OFF_FU, OFF_GQ))
    for lane_half in range(FOURIER_WIDTH // LANES):
        fu_ref[lane_half] = hfu[:, lane_half * LANES:(lane_half + 1) * LANES]
    gqk_ref[...] = _dot(xb, wcols(OFF_GQ, OFF_GV))
    gv_ref[...] = _dot(xb, wcols(OFF_GV, OFF_GR))
    gr_ref[...] = _dot(xb, wcols(OFF_GR, OFF_GZ))
    gz_ref[...] = _dot(xb, wcols(OFF_GZ, IN_WIDTH))


def _layer_weight(w_all, layer):
    return pl.BlockSpec((None,) + w_all.shape[1:], lambda i: (layer,) + (0,) * (w_all.ndim - 1),
                        pipeline_mode=pl.Buffered(1))


def _in_proj(x2d, wqv_t, w_rest, rope, seq, tm=512):
    m = x2d.shape[0]
    batch = m // seq
    n_seq_tiles = seq // tm
    cos128, sin128, cos_t, sin_t = rope
    row = lambda i: (i, 0)
    const = lambda i: (0, 0)
    tab = lambda i: (i % n_seq_tiles, 0)
    tab_t = lambda i: (0, i % n_seq_tiles)
    tiled = lambda t: (jax.ShapeDtypeStruct((batch, DIFF_HEADS, seq // t, LANES, t), BF16),
                       pl.BlockSpec((1, DIFF_HEADS, tm // t, LANES, t),
                                    lambda i: (i // n_seq_tiles, 0, i % n_seq_tiles, 0, 0)))
    qt_shape, qt_spec = tiled(ATTN_TQ)
    vt_shape, vt_spec = tiled(ATTN_KC)
    k_shape = jax.ShapeDtypeStruct((batch, DIFF_HEADS, seq, LANES), BF16)
    k_spec = pl.BlockSpec((1, DIFF_HEADS, tm, LANES), lambda i: (i // n_seq_tiles, 0, i % n_seq_tiles, 0))
    fu_shape = jax.ShapeDtypeStruct((FOURIER_WIDTH // LANES, m, LANES), F32)
    fu_spec = pl.BlockSpec((FOURIER_WIDTH // LANES, tm, LANES), lambda i: (0, i, 0))
    flat_shapes = (
        jax.ShapeDtypeStruct((m, 2 * GLA_K_COLS), F32),
        jax.ShapeDtypeStruct((m, GLA_WIDTH), F32),
        jax.ShapeDtypeStruct((m, GLA_WIDTH), F32),
        jax.ShapeDtypeStruct((m, 2 * GLA_GATE_RANK), F32),
    )
    return pl.pallas_call(
        _inproj_kernel,
        grid=(m // tm,),
        in_specs=[
            pl.BlockSpec((tm, D_MODEL), row),
            pl.BlockSpec((DIFF_QK_COLS + DIFF_WIDTH, D_MODEL), const),
            pl.BlockSpec(w_rest.shape, const),
            pl.BlockSpec((tm, LANES), tab),
            pl.BlockSpec((tm, LANES), tab),
            pl.BlockSpec((LANES, tm), tab_t),
            pl.BlockSpec((LANES, tm), tab_t),
        ],
        out_specs=(qt_spec, k_spec, vt_spec, fu_spec)
        + tuple(pl.BlockSpec((tm, s.shape[1]), row) for s in flat_shapes),
        out_shape=(qt_shape, k_shape, vt_shape, fu_shape) + flat_shapes,
        compiler_params=_params(1),
        name="in_proj",
    )(x2d, wqv_t, w_rest, cos128, sin128, cos_t, sin_t)


ATTN_SUB = 128
ATTN_GROUP = 4


def _attn_kernel(lam_ref, g_ref, qt_ref, k_ref, vt_ref, o_ref,
                 sa_ref, sb_ref, pa_ref, pb_ref, mx_ref, coef_ref, ot_ref, *, lam_init):
    tq, kc, sub = ATTN_TQ, ATTN_KC, ATTN_SUB
    seq = k_ref.shape[2]
    tiles = seq // tq
    n_units = DIFF_HEADS * tiles
    n_chunks = seq // kc
    lp = lam_ref[...]
    lam = (jnp.exp(jnp.sum(lp[0:1] * lp[1:2], axis=1, keepdims=True))
           - jnp.exp(jnp.sum(lp[2:3] * lp[3:4], axis=1, keepdims=True)) + lam_init)
    gain = g_ref[...] * (1.0 - lam_init)
    feature = lax.broadcasted_iota(jnp.int32, (LANES, tq), 0)
    first_map = feature < DIFF_QK_DIM
    s_bufs = (sa_ref, sb_ref)
    p_bufs = (pa_ref, pb_ref)

    n_groups = n_chunks // ATTN_GROUP
    slabs_per_trip = (tq // LANES) // n_groups
    assert slabs_per_trip * n_groups * LANES == tq

    def step(k, par, do_a=True, do_b=True, do_c=True, do_d=True):
        s_w, s_r = s_bufs[par], s_bufs[1 - par]
        p_w, p_r = p_bufs[1 - par], p_bufs[par]
        if do_d:
            h_d = (k - 3) // tiles
            q_base = ((k - 3) % tiles) * tq
        if do_a:
            h_a = k // tiles
            qt = qt_ref[0, h_a, k % tiles]
            zero = jnp.zeros_like(qt)
            qw = (jnp.where(first_map, qt, zero), jnp.where(first_map, zero, qt))
        if do_b:
            m8 = (mx_ref[1 - par, 0], mx_ref[1 - par, 1])
        if do_c:
            h_c = (k - 2) // tiles
            c16 = (coef_ref[par, 0], coef_ref[par, 1])

        def chunk(c, carry):
            mx = [carry[0], carry[1]]
            ls = [carry[2], carry[3]]
            acc = carry[4]
            for part in range(kc // sub):
                rows = pl.ds(pl.multiple_of(c * kc + part * sub, sub), sub)
                if do_a:
                    keys = k_ref[0, h_a, rows, :]
                    for m in range(2):
                        st = _dot(keys, qw[m])
                        s_w[m, rows, :] = st
                        mx[m] = jnp.maximum(mx[m], jnp.max(st.reshape(sub // SUBLANES, SUBLANES, tq), axis=0))
                if do_b:
                    for m in range(2):
                        p = jnp.exp2(s_r[m, rows, :].reshape(sub // SUBLANES, SUBLANES, tq) - m8[m][None])
                        ls[m] = ls[m] + jnp.sum(p, axis=0)
                        p_w[m, rows, :] = p.reshape(sub, tq).astype(BF16)
            if do_c:
                rows = pl.ds(pl.multiple_of(c * kc, kc), kc)
                packed = (kc // PACKED_SUBLANES, PACKED_SUBLANES, tq)
                a = p_r[0, rows, :].reshape(packed) * c16[0][None] - p_r[1, rows, :].reshape(packed) * c16[1][None]
                part_o = _dot(vt_ref[0, h_c, c], a.reshape(kc, tq))
                acc = part_o if acc is None else acc + part_o
            return mx[0], mx[1], ls[0], ls[1], acc

        lowest = jnp.full((SUBLANES, tq), -jnp.inf, F32)
        nothing = jnp.zeros((SUBLANES, tq), F32)
        def group(gi, carry):
            if do_d:
                for slab in range(slabs_per_trip):
                    si = gi * slabs_per_trip + slab
                    o = ot_ref[1 - par, si].T
                    ms = jnp.mean(o * o, axis=1, keepdims=True)
                    q_start = q_base + si * LANES
                    q_rows = pl.ds(q_start if isinstance(q_start, int) else pl.multiple_of(q_start, LANES),
                                   LANES)
                    o_ref[0, h_d, q_rows, :] = (o * lax.rsqrt(ms + LN_EPS) * gain).astype(BF16)
            stats = carry + (None,)
            for c in range(ATTN_GROUP):
                stats = chunk(gi * ATTN_GROUP + c, stats)
            if do_c:
                for si in range(tq // LANES):
                    ot_ref[par, si] += stats[4][:, si * LANES:(si + 1) * LANES]
            return stats[:4]

        if do_c:
            ot_ref[par] = jnp.zeros(ot_ref.shape[1:], F32)
        mx0, mx1, ls0, ls1 = lax.fori_loop(0, n_groups, group, (lowest, lowest, nothing, nothing))
        if do_a:
            mx_ref[par, 0] = jnp.broadcast_to(jnp.max(mx0, axis=0, keepdims=True), (SUBLANES, tq))
            mx_ref[par, 1] = jnp.broadcast_to(jnp.max(mx1, axis=0, keepdims=True), (SUBLANES, tq))
        if do_b:
            l0 = jnp.sum(ls0, axis=0, keepdims=True)
            l1 = jnp.sum(ls1, axis=0, keepdims=True)
            coef_ref[1 - par, 0] = jnp.broadcast_to(1.0 / l0, (PACKED_SUBLANES, tq)).astype(BF16)
            coef_ref[1 - par, 1] = jnp.broadcast_to(lam / l1, (PACKED_SUBLANES, tq)).astype(BF16)

    def step_pair(j, carry):
        step(2 * j, 0)
        step(2 * j + 1, 1)
        return carry

    step(0, 0, do_b=False, do_c=False, do_d=False)
    step(1, 1, do_c=False, do_d=False)
    step(2, 0, do_d=False)
    step(3, 1)
    lax.fori_loop(2, n_units // 2, step_pair, 0)
    step(n_units, 0, do_a=False)
    step(n_units + 1, 1, do_a=False, do_b=False)
    step(n_units + 2, 0, do_a=False, do_b=False, do_c=False)


def _diff_attention(qt, k, vt, lam_params, g, lam_init):
    b, _, s, _ = k.shape
    whole = lambda a: pl.BlockSpec((1,) + a.shape[1:], lambda bi: (bi,) + (0,) * (a.ndim - 1))
    return pl.pallas_call(
        functools.partial(_attn_kernel, lam_init=lam_init),
        grid=(b,),
        in_specs=[
            pl.BlockSpec((4, DIFF_QK_DIM), lambda bi: (0, 0)),
            pl.BlockSpec((1, DIFF_V_DIM), lambda bi: (0, 0)),
            whole(qt), whole(k), whole(vt),
        ],
        out_specs=whole(k),
        out_shape=jax.ShapeDtypeStruct((b, DIFF_HEADS, s, LANES), BF16),
        scratch_shapes=[
            pltpu.VMEM((2, s, ATTN_TQ), F32), pltpu.VMEM((2, s, ATTN_TQ), F32),
            pltpu.VMEM((2, s, ATTN_TQ), BF16), pltpu.VMEM((2, s, ATTN_TQ), BF16),
            pltpu.VMEM((2, 2, SUBLANES, ATTN_TQ), F32),
            pltpu.VMEM((2, 2, PACKED_SUBLANES, ATTN_TQ), BF16),
            pltpu.VMEM((2, ATTN_TQ // LANES, DIFF_V_DIM, LANES), F32),
        ],
        compiler_params=_params(1),
        name="diff_attn",
    )(lam_params, g, qt, k, vt)


FOURIER_PACK = 2


def _fourier_kernel(u_ref, cc_ref, sc_ref, fe_ref, fo_ref, w_ref, o_ref, ab_ref, *, scale):
    s = u_ref.shape[2]
    half = s // 2
    w = FOURIER_WIDTH
    for e in range(FOURIER_PACK):
        lanes = slice(e * w, (e + 1) * w)
        for parity in range(2):
            u = jnp.concatenate([u_ref[lh, e, pl.ds(parity, half, stride=2), :] for lh in range(2)],
                                axis=1).astype(BF16)
            ab_ref[parity, 0:half, lanes] = _dot(u, cc_ref[...]).astype(BF16)
            ab_ref[parity, half:s, lanes] = _dot(u, sc_ref[...]).astype(BF16)
    even = _dot(fe_ref[...].astype(BF16), ab_ref[0])
    odd = _dot(fo_ref[...].astype(BF16), ab_ref[1])
    for rows, z in ((slice(0, half), (even + odd) * scale), (slice(half, s), (even - odd) * scale)):
        for e in range(FOURIER_PACK):
            y = _dot(z[:, e * w:(e + 1) * w].astype(BF16), w_ref[...])
            o_ref[e, rows, :] = y.astype(BF16)


def _fourier(u, cc_bd, sc_bd, f_even, f_odd, w_bd):
    _, b, s, _ = u.shape
    pack = FOURIER_PACK
    scale = 1.0 / math.sqrt(s * FOURIER_GROUP_DIM)
    const2 = lambda bi: (0, 0)
    table = pl.BlockSpec((s // 2, s), const2, pipeline_mode=pl.Buffered(1))
    return pl.pallas_call(
        functools.partial(_fourier_kernel, scale=scale),
        grid=(b // pack,),
        in_specs=[
            pl.BlockSpec((2, pack, s, LANES), lambda bi: (0, bi, 0, 0)),
            pl.BlockSpec((FOURIER_WIDTH, FOURIER_WIDTH), const2),
            pl.BlockSpec((FOURIER_WIDTH, FOURIER_WIDTH), const2),
            table,
            table,
            pl.BlockSpec((FOURIER_WIDTH, FOURIER_WIDTH), const2),
        ],
        out_specs=pl.BlockSpec((pack, s, FOURIER_WIDTH), lambda bi: (bi, 0, 0)),
        out_shape=jax.ShapeDtypeStruct((b, s, FOURIER_WIDTH), BF16),
        scratch_shapes=[pltpu.VMEM((2, s, pack * FOURIER_WIDTH), BF16)],
        compiler_params=_params(1),
        name="fourier",
    )(u, cc_bd, sc_bd, f_even, f_odd, w_bd)


def _split3(g):
    hi = g.astype(BF16)
    r1 = g - hi.astype(F32)
    mid = r1.astype(BF16)
    lo = (r1 - mid.astype(F32)).astype(BF16)
    return hi, mid, lo


def _gla_kernel(qk_ref, v_ref, r_ref, z_ref, w2_ref, b2_ref, g_ref, o_ref,
                la_ref, vt_ref, acc_ref, qt_ref, kt_ref, a_ref, kv_ref, dec_ref, st_ref):
    s = v_ref.shape[1]
    c = GLA_CHUNK
    pair = 2 * c
    n_pairs = s // pair
    n_chunks = s // c
    kc = GLA_K_COLS

    logit = _dot(z_ref[0].astype(BF16), w2_ref[...]) + b2_ref[...]
    la_ref[...] = (jnp.minimum(logit, 0.0) - jnp.log(1.0 + jnp.exp(-jnp.abs(logit)))) * (1.0 / GLA_GATE_TAU)
    vt_ref[...] = v_ref[0].T.astype(BF16)

    row_i = lax.broadcasted_iota(jnp.int32, (pair, pair), 0)
    col_i = lax.broadcasted_iota(jnp.int32, (pair, pair), 1)
    same_chunk = (row_i // c) == (col_i // c)
    cum_mats = ((same_chunk & (col_i <= row_i)).astype(BF16), (same_chunk & (col_i >= row_i)).astype(BF16))
    r_hk = lax.broadcasted_iota(jnp.int32, (GLA_HEADS * c, kc), 0) // c
    l_hk = lax.broadcasted_iota(jnp.int32, (GLA_HEADS * c, kc), 1) // GLA_K_DIM
    blk_k = r_hk == l_hk
    r_hv = lax.broadcasted_iota(jnp.int32, (GLA_HEADS * c, GLA_WIDTH), 0) // c
    l_hv = lax.broadcasted_iota(jnp.int32, (GLA_HEADS * c, GLA_WIDTH), 1) // GLA_V_DIM
    blk_v = r_hv == l_hv
    qi = lax.broadcasted_iota(jnp.int32, (c, GLA_HEADS * c), 0)
    kj = lax.broadcasted_iota(jnp.int32, (c, GLA_HEADS * c), 1) % c
    q_scale = GLA_K_DIM ** -0.5

    causal = (kj <= qi, kj >= qi)

    def pair_rows(p):
        start = p * pair
        return pl.ds(start if isinstance(start, int) else pl.multiple_of(start, pair), pair)

    def decayed_qk(p):
        rows = pair_rows(p)
        q2 = qk_ref[0, rows, 0:kc] * q_scale
        k2 = qk_ref[0, rows, kc:2 * kc]
        for d in range(2):
            hi, mid, lo = _split3(la_ref[rows, d * kc:(d + 1) * kc])
            c3 = _dot(cum_mats[d], jnp.concatenate([hi, mid, lo], axis=1))
            bcum = c3[:, 0:kc] + c3[:, kc:2 * kc] + c3[:, 2 * kc:3 * kc]
            eb = jnp.exp(bcum)
            qt_ref[d, rows, :] = (q2 * eb).astype(BF16)
            kt_ref[d, rows, :] = k2 * jnp.exp(-bcum)
            for ci in range(2):
                edge = ci * c if d == 1 else (ci + 1) * c - 1
                dec_ref[d, 2 * p + ci] = jnp.broadcast_to(eb[edge:edge + 1], (SUBLANES, kc))

    def intra_scores(p):
        for ci in range(2):
            start = p * pair + ci * c
            rows = pl.ds(start if isinstance(start, int) else pl.multiple_of(start, c), c)
            for d in range(2):
                kt_bd = jnp.where(blk_k, jnp.concatenate([kt_ref[d, rows, :]] * GLA_HEADS, axis=0), 0.0)
                a = _dot_nt(qt_ref[d, rows, :], kt_bd.astype(BF16))
                a_ref[2 * p + ci, d * c:(d + 1) * c, :] = jnp.where(causal[d], a, 0.0).astype(BF16)

    def intra_out_and_kv(p):
        rows2 = pair_rows(p)
        v2 = v_ref[0, rows2, :]
        vt2 = vt_ref[:, rows2]
        kt2 = jnp.concatenate([kt_ref[0, rows2, :], kt_ref[1, rows2, :]], axis=1)
        for ci in range(2):
            sl = slice(ci * c, (ci + 1) * c)
            start = p * pair + ci * c
            rows = pl.ds(start if isinstance(start, int) else pl.multiple_of(start, c), c)
            v_bd = jnp.where(blk_v, jnp.concatenate([v2[sl]] * GLA_HEADS, axis=0), 0.0).astype(BF16)
            o_both = _dot(a_ref[2 * p + ci], v_bd)
            acc_ref[rows, :] = o_both[0:c] + o_both[c:2 * c]
            gap = jnp.zeros((c, 2 * kc), F32)
            kt_pair = jnp.concatenate([kt2[sl], gap] if ci == 0 else [gap, kt2[sl]], axis=0).astype(BF16)
            kv = _dot(vt2, kt_pair)
            kv_ref[0, 2 * p + ci] = jnp.where(blk_k, kv[:, 0:kc], 0.0)
            kv_ref[1, 2 * p + ci] = jnp.where(blk_k, kv[:, kc:2 * kc], 0.0)

    decayed_qk(0)
    intra_scores(0)
    decayed_qk(1)

    def intra_step(p, carry):
        intra_out_and_kv(p)
        intra_scores(p + 1)
        decayed_qk(p + 2)
        return carry

    lax.fori_loop(0, n_pairs - 2, intra_step, 0, unroll=2)
    intra_out_and_kv(n_pairs - 2)
    intra_scores(n_pairs - 1)
    intra_out_and_kv(n_pairs - 1)

    st_ref[...] = jnp.zeros_like(st_ref)

    def scan_step(i, carry):
        for d in range(2):
            n = i if d == 0 else n_chunks - 1 - i
            rows = pl.ds(pl.multiple_of(n * c, c), c)
            st = st_ref[d]
            acc_ref[rows, :] += _dot_nt(qt_ref[d, rows, :], st.astype(BF16))
            st = (st + kv_ref[d, n]).reshape(GLA_WIDTH // SUBLANES, SUBLANES, kc) * dec_ref[d, n][None]
            st_ref[d] = st.reshape(GLA_WIDTH, kc)
        return carry

    lax.fori_loop(0, n_chunks, scan_step, 0, unroll=8)

    o = acc_ref[...]
    o2 = o * o
    o2_hi = o2.astype(BF16)
    o2_lo = (o2 - o2_hi.astype(F32)).astype(BF16)
    gi = lax.broadcasted_iota(jnp.int32, (GLA_WIDTH, GLA_WIDTH), 0) // GLA_V_DIM
    gj = lax.broadcasted_iota(jnp.int32, (GLA_WIDTH, GLA_WIDTH), 1) // GLA_V_DIM
    ones_bd = (gi == gj).astype(BF16)
    ms = (_dot(o2_hi, ones_bd) + _dot(o2_lo, ones_bd)) * (1.0 / GLA_V_DIM)
    r = r_ref[0]
    gate = r * (1.0 / (1.0 + jnp.exp(-r)))
    o_ref[0] = (o * lax.rsqrt(ms + LN_EPS) * g_ref[...] * gate).astype(BF16)


def _gla(gqk, gv, gr, gz, w2_bd, b2_cat, g_tiled):
    b, s, _ = gv.shape
    per_b = lambda bi: (bi, 0, 0)
    const = lambda bi: (0, 0)
    return pl.pallas_call(
        _gla_kernel,
        grid=(b,),
        in_specs=[
            pl.BlockSpec((1, s, 2 * GLA_K_COLS), per_b),
            pl.BlockSpec((1, s, GLA_WIDTH), per_b),
            pl.BlockSpec((1, s, GLA_WIDTH), per_b),
            pl.BlockSpec((1, s, 2 * GLA_GATE_RANK), per_b),
            pl.BlockSpec((2 * GLA_GATE_RANK, 2 * GLA_K_COLS), const),
            pl.BlockSpec((1, 2 * GLA_K_COLS), const),
            pl.BlockSpec((1, GLA_WIDTH), const),
        ],
        out_specs=pl.BlockSpec((1, s, GLA_WIDTH), per_b),
        out_shape=jax.ShapeDtypeStruct((b, s, GLA_WIDTH), BF16),
        scratch_shapes=[
            pltpu.VMEM((s, 2 * GLA_K_COLS), F32),
            pltpu.VMEM((GLA_WIDTH, s), BF16),
            pltpu.VMEM((s, GLA_WIDTH), F32),
            pltpu.VMEM((2, s, GLA_K_COLS), BF16),
            pltpu.VMEM((2, s, GLA_K_COLS), F32),
            pltpu.VMEM((s // GLA_CHUNK, 2 * GLA_CHUNK, GLA_WIDTH), BF16),
            pltpu.VMEM((2, s // GLA_CHUNK, GLA_WIDTH, GLA_K_COLS), F32),
            pltpu.VMEM((2, s // GLA_CHUNK, SUBLANES, GLA_K_COLS), F32),
            pltpu.VMEM((2, GLA_WIDTH, GLA_K_COLS), F32),
        ],
        compiler_params=_params(1),
        name="gla",
    )(gqk, gv, gr, gz, w2_bd, b2_cat, g_tiled)


def _layer_norm(y, g, b):
    mu = jnp.mean(y, axis=1, keepdims=True)
    d = y - mu
    var = jnp.mean(d * d, axis=1, keepdims=True)
    return d * lax.rsqrt(var + LN_EPS) * g + b


ROW_SUB = 256


def _outproj_kernel(od_ref, of_ref, og_ref, x_ref, w_ref, g_ref, b_ref, o_ref):
    w = w_ref[...].astype(BF16)
    for r in range(x_ref.shape[0] // ROW_SUB):
        rows = slice(r * ROW_SUB, (r + 1) * ROW_SUB)
        mixed = jnp.concatenate([od_ref[0, hd, rows, :] for hd in range(DIFF_HEADS)]
                                + [of_ref[rows, :], og_ref[rows, :]], axis=1)
        m = _dot(mixed, w)
        o_ref[rows, :] = _layer_norm(DEEPNORM_ALPHA * x_ref[rows, :] + m, g_ref[...], b_ref[...])


def _out_proj(o_diff, o_four, o_gla, x2d, w_all, layer, g, b, tm=1024):
    m = x2d.shape[0]
    seq = o_diff.shape[2]
    n_seq_tiles = seq // tm
    row = lambda i: (i, 0)
    const = lambda i: (0, 0)
    return pl.pallas_call(
        _outproj_kernel,
        grid=(m // tm,),
        in_specs=[
            pl.BlockSpec((1, DIFF_HEADS, tm, LANES), lambda i: (i // n_seq_tiles, 0, i % n_seq_tiles, 0)),
            pl.BlockSpec((tm, FOURIER_WIDTH), row),
            pl.BlockSpec((tm, GLA_WIDTH), row),
            pl.BlockSpec((tm, D_MODEL), row),
            _layer_weight(w_all, layer),
            pl.BlockSpec((1, D_MODEL), const),
            pl.BlockSpec((1, D_MODEL), const),
        ],
        out_specs=pl.BlockSpec((tm, D_MODEL), row),
        out_shape=jax.ShapeDtypeStruct((m, D_MODEL), F32),
        compiler_params=_params(1),
        name="out_proj",
    )(o_diff, o_four, o_gla, x2d, w_all, g, b)


FFN_CHUNK = 256


FFN_ROWS = 512


def _ffn_kernel(x_ref, wg_ref, wu_ref, wd_ref, g_ref, b_ref, o_ref):
    sub_tiles = [slice(r * FFN_ROWS, (r + 1) * FFN_ROWS) for r in range(x_ref.shape[0] // FFN_ROWS)]
    xbs = [x_ref[rows, :].astype(BF16) for rows in sub_tiles]
    for ci in range(FFN_HIDDEN // FFN_CHUNK):
        cols = slice(ci * FFN_CHUNK, (ci + 1) * FFN_CHUNK)
        wg = wg_ref[:, cols].astype(BF16)
        wu = wu_ref[:, cols].astype(BF16)
        wd = wd_ref[cols, :].astype(BF16)
        for rows, xb in zip(sub_tiles, xbs):
            hg = _dot(xb, wg)
            hu = _dot(xb, wu)
            act = (hg * (1.0 / (1.0 + jnp.exp(-hg))) * hu).astype(BF16)
            part = _dot(act, wd)
            if ci == 0:
                o_ref[rows, :] = part
            else:
                o_ref[rows, :] += part
    for rows in sub_tiles:
        o_ref[rows, :] = _layer_norm(DEEPNORM_ALPHA * x_ref[rows, :] + o_ref[rows, :], g_ref[...], b_ref[...])


def _ffn(x2d, wg_all, wu_all, wd_all, layer, g, b, tm=1024):
    m = x2d.shape[0]
    row = lambda i: (i, 0)
    const = lambda i: (0, 0)
    return pl.pallas_call(
        _ffn_kernel,
        grid=(m // tm,),
        in_specs=[
            pl.BlockSpec((tm, D_MODEL), row),
            _layer_weight(wg_all, layer),
            _layer_weight(wu_all, layer),
            _layer_weight(wd_all, layer),
            pl.BlockSpec((1, D_MODEL), const),
            pl.BlockSpec((1, D_MODEL), const),
        ],
        out_specs=pl.BlockSpec((tm, D_MODEL), row),
        out_shape=jax.ShapeDtypeStruct((m, D_MODEL), F32),
        compiler_params=_params(1, FFN_VMEM_LIMIT),
        name="ffn",
    )(x2d, wg_all, wu_all, wd_all, g, b)


def _rope_tables(seq):
    half = DIFF_QK_DIM // 2
    pos = jnp.arange(seq, dtype=F32)
    inv_freq = ROPE_THETA ** (-jnp.arange(0, DIFF_QK_DIM, 2, dtype=F32) / DIFF_QK_DIM)
    ang = pos[:, None] * inv_freq[None, :]
    cos, sin = jnp.cos(ang), jnp.sin(ang)
    reps = LANES // DIFF_QK_DIM
    cos128 = jnp.tile(jnp.concatenate([cos, cos], axis=1), (1, reps))
    sin128 = jnp.tile(jnp.concatenate([-sin, sin], axis=1), (1, reps))
    assert cos128.shape == (seq, LANES) and half * 2 == DIFF_QK_DIM
    return cos128, sin128, cos128.T, sin128.T


def _dft_tables(seq):
    half = seq // 2
    j = np.arange(half, dtype=np.int64)[:, None]
    pos = np.arange(half, dtype=np.int64)[None, :]
    ang_even = 2.0 * np.pi * ((j * pos) % half).astype(np.float64) / half
    ang_odd = 2.0 * np.pi * ((j * (2 * pos + 1)) % seq).astype(np.float64) / seq
    f_even = np.concatenate([np.cos(ang_even), -np.sin(ang_even)], axis=1)
    f_odd = np.concatenate([np.cos(ang_odd), -np.sin(ang_odd)], axis=1)
    c = np.arange(FOURIER_GROUP_DIM, dtype=np.int64)
    ang_c = 2.0 * np.pi * ((c[:, None] * c[None, :]) % FOURIER_GROUP_DIM).astype(np.float64) / FOURIER_GROUP_DIM
    eye = np.eye(FOURIER_GROUPS)
    cc_bd = np.kron(eye, np.cos(ang_c))
    sc_bd = np.kron(eye, np.sin(ang_c))
    return (jnp.asarray(f_even, dtype=F32), jnp.asarray(f_odd, dtype=F32),
            jnp.asarray(cc_bd, dtype=F32).astype(BF16), jnp.asarray(sc_bd, dtype=F32).astype(BF16))


def _block_diag(blocks):
    n = len(blocks)
    rows = []
    for i, blk in enumerate(blocks):
        rows.append(jnp.concatenate(
            [blk if j == i else jnp.zeros((blk.shape[0], blocks[j].shape[1]), blk.dtype) for j in range(n)],
            axis=1))
    return jnp.concatenate(rows, axis=0)


def kernel(x, w_in, diff_lambda, diff_norm_g, fourier_w, gla_gate_w2, gla_gate_b2, gla_norm_g, w_out,
           ln1_g, ln1_b, ffn_w_gate, ffn_w_up, ffn_w_down, ln2_g, ln2_b):
    b, s, d = x.shape
    m = b * s
    rope = _rope_tables(s)
    f_even, f_odd, cc_bd, sc_bd = _dft_tables(s)
    x2d = x.reshape(m, d)
    for l in range(DEPTH):
        lam_init = 0.8 - 0.6 * math.exp(-0.3 * l)
        wqv_t = jnp.concatenate([w_in[l, :, OFF_DQ:OFF_DK], w_in[l, :, OFF_DV:OFF_FU]], axis=1).T.astype(BF16)
        w_rest = jnp.concatenate([w_in[l, :, OFF_DK:OFF_DV], w_in[l, :, OFF_FU:]], axis=1).astype(BF16)
        qt, k, vt, fu, gqk, gv, gr, gz = _in_proj(x2d, wqv_t, w_rest, rope, s)
        o_diff = _diff_attention(qt, k, vt, diff_lambda[l], diff_norm_g[l].reshape(1, -1), lam_init)
        w_four = _block_diag([fourier_w[l, g] for g in range(FOURIER_GROUPS)]).astype(BF16)
        o_four = _fourier(fu.reshape(-1, b, s, LANES), cc_bd, sc_bd, f_even, f_odd, w_four)
        w2_bd = _block_diag([gla_gate_w2[l, 0], gla_gate_w2[l, 1]]).astype(BF16)
        b2_cat = gla_gate_b2[l].reshape(1, -1)
        g_gla = jnp.tile(gla_norm_g[l], GLA_HEADS).reshape(1, -1)
        o_gla = _gla(gqk.reshape(b, s, -1), gv.reshape(b, s, -1), gr.reshape(b, s, -1),
                     gz.reshape(b, s, -1), w2_bd, b2_cat, g_gla)
        x2d = _out_proj(o_diff, o_four.reshape(m, -1), o_gla.reshape(m, -1), x2d,
                        w_out, l, ln1_g[l].reshape(1, -1), ln1_b[l].reshape(1, -1))
        x2d = _ffn(x2d, ffn_w_gate, ffn_w_up, ffn_w_down, l,
                   ln2_g[l].reshape(1, -1), ln2_b[l].reshape(1, -1))
    return x2d.reshape(b, s, d)
```

```python
import functools
import math

import jax
import jax.numpy as jnp
import numpy as np
from jax import lax
from jax.experimental import pallas as pl
from jax.experimental.pallas import tpu as pltpu

D_MODEL = 1024
DEPTH = 2
DIFF_HEADS = 4
DIFF_QK_DIM = 64
DIFF_V_DIM = 128
DIFF_WIDTH = 512
DIFF_QK_COLS = 512
ROPE_THETA = 10000.0
FOURIER_GROUPS = 4
FOURIER_GROUP_DIM = 64
FOURIER_WIDTH = 256
GLA_HEADS = 4
GLA_V_DIM = 64
GLA_K_DIM = 32
GLA_WIDTH = 256
GLA_K_COLS = 128
GLA_GATE_RANK = 16
GLA_GATE_TAU = 16.0
GLA_CHUNK = 64
FFN_HIDDEN = 2816
DEEPNORM_ALPHA = (2 * DEPTH) ** 0.25
LN_EPS = 1e-5

OFF_DQ = 0
OFF_DK = OFF_DQ + DIFF_QK_COLS
OFF_DV = OFF_DK + DIFF_QK_COLS
OFF_FU = OFF_DV + DIFF_WIDTH
OFF_GQ = OFF_FU + FOURIER_WIDTH
OFF_GV = OFF_GQ + 2 * GLA_K_COLS
OFF_GR = OFF_GV + GLA_WIDTH
OFF_GZ = OFF_GR + GLA_WIDTH
IN_WIDTH = OFF_GZ + 2 * GLA_GATE_RANK

LANES = 128
SUBLANES = 8
PACKED_SUBLANES = 16
VMEM_LIMIT = 56 * 1024 * 1024
FFN_VMEM_LIMIT = 60 * 1024 * 1024

BF16 = jnp.bfloat16
F32 = jnp.float32


def _dot(a, b):
    return jnp.dot(a, b, preferred_element_type=F32)


def _dot_nt(a, b):
    return lax.dot_general(a, b, (((1,), (1,)), ((), ())), preferred_element_type=F32)


def _params(n_grid_dims, vmem_limit=VMEM_LIMIT):
    return pltpu.CompilerParams(
        dimension_semantics=("arbitrary",) * n_grid_dims,
        vmem_limit_bytes=vmem_limit)


def _rope_slab(t, cos, sin_signed, first_half):
    half = DIFF_QK_DIM // 2
    swapped = jnp.where(first_half, pltpu.roll(t, LANES - half, 1), pltpu.roll(t, half, 1))
    return t * cos + swapped * sin_signed


ATTN_TQ = 256
ATTN_KC = 256


def _rest_col(col):
    assert col >= OFF_DK and not OFF_DV <= col < OFF_FU
    return col - OFF_DK if col < OFF_DV else col - OFF_DK - DIFF_WIDTH


def _inproj_kernel(x_ref, wqv_ref, w_ref, cos_ref, sin_ref, cost_ref, sint_ref,
                   qt_ref, k_ref, vt_ref, fu_ref, gqk_ref, gv_ref, gr_ref, gz_ref):
    xb = x_ref[...].astype(BF16)
    tm = xb.shape[0]
    cos = cos_ref[...]
    sin_signed = sin_ref[...]
    lane = lax.broadcasted_iota(jnp.int32, cos.shape, 1)
    first_half = (lane % DIFF_QK_DIM) < (DIFF_QK_DIM // 2)
    qk_scale = DIFF_QK_DIM ** -0.5 * math.log2(math.e)
    hqv_t = _dot_nt(wqv_ref[...], xb)
    cos_t = cost_ref[...]
    sin_t = sint_ref[...]
    half = DIFF_QK_DIM // 2
    wcols = lambda lo, hi: w_ref[:, _rest_col(lo):_rest_col(hi - 1) + 1]
    hk = _dot(xb, wcols(OFF_DK, OFF_DV))
    for hd in range(DIFF_HEADS):
        qt = hqv_t[hd * LANES:(hd + 1) * LANES]
        swapped = jnp.concatenate(
            [qt[half:2 * half], qt[0:half], qt[3 * half:4 * half], qt[2 * half:3 * half]], axis=0)
        qt = ((qt * cos_t + swapped * sin_t) * qk_scale).astype(BF16)
        vt = hqv_t[DIFF_QK_COLS + hd * LANES:DIFF_QK_COLS + (hd + 1) * LANES].astype(BF16)
        for j in range(tm // ATTN_TQ):
            qt_ref[0, hd, j] = qt[:, j * ATTN_TQ:(j + 1) * ATTN_TQ]
        for j in range(tm // ATTN_KC):
            vt_ref[0, hd, j] = vt[:, j * ATTN_KC:(j + 1) * ATTN_KC]
        slab = slice(hd * LANES, (hd + 1) * LANES)
        k_ref[0, hd] = _rope_slab(hk[:, slab], cos, sin_signed, first_half).astype(BF16)
    hfu = _dot(xb, wcols(OFF_FU, OFF_GQ))
    for lane_half in range(FOURIER_WIDTH // LANES):
        fu_ref[lane_half] = hfu[:, lane_half * LANES:(lane_half + 1) * LANES]
    gqk_ref[...] = _dot(xb, wcols(OFF_GQ, OFF_GV))
    gv_ref[...] = _dot(xb, wcols(OFF_GV, OFF_GR))
    gr_ref[...] = _dot(xb, wcols(OFF_GR, OFF_GZ))
    gz_ref[...] = _dot(xb, wcols(OFF_GZ, IN_WIDTH))


def _layer_weight(w_all, layer):
    return pl.BlockSpec((None,) + w_all.shape[1:], lambda i: (layer,) + (0,) * (w_all.ndim - 1),
                        pipeline_mode=pl.Buffered(1))


def _in_proj(x2d, wqv_t, w_rest, rope, seq, tm=1024):
    m = x2d.shape[0]
    batch = m // seq
    n_seq_tiles = seq // tm
    cos128, sin128, cos_t, sin_t = rope
    row = lambda i: (i, 0)
    const = lambda i: (0, 0)
    tab = lambda i: (i % n_seq_tiles, 0)
    tab_t = lambda i: (0, i % n_seq_tiles)
    tiled = lambda t: (jax.ShapeDtypeStruct((batch, DIFF_HEADS, seq // t, LANES, t), BF16),
                       pl.BlockSpec((1, DIFF_HEADS, tm // t, LANES, t),
                                    lambda i: (i // n_seq_tiles, 0, i % n_seq_tiles, 0, 0)))
    qt_shape, qt_spec = tiled(ATTN_TQ)
    vt_shape, vt_spec = tiled(ATTN_KC)
    k_shape = jax.ShapeDtypeStruct((batch, DIFF_HEADS, seq, LANES), BF16)
    k_spec = pl.BlockSpec((1, DIFF_HEADS, tm, LANES), lambda i: (i // n_seq_tiles, 0, i % n_seq_tiles, 0))
    fu_shape = jax.ShapeDtypeStruct((FOURIER_WIDTH // LANES, m, LANES), F32)
    fu_spec = pl.BlockSpec((FOURIER_WIDTH // LANES, tm, LANES), lambda i: (0, i, 0))
    flat_shapes = (
        jax.ShapeDtypeStruct((m, 2 * GLA_K_COLS), F32),
        jax.ShapeDtypeStruct((m, GLA_WIDTH), F32),
        jax.ShapeDtypeStruct((m, GLA_WIDTH), F32),
        jax.ShapeDtypeStruct((m, 2 * GLA_GATE_RANK), F32),
    )
    return pl.pallas_call(
        _inproj_kernel,
        grid=(m // tm,),
        in_specs=[
            pl.BlockSpec((tm, D_MODEL), row),
            pl.BlockSpec((DIFF_QK_COLS + DIFF_WIDTH, D_MODEL), const),
            pl.BlockSpec(w_rest.shape, const),
            pl.BlockSpec((tm, LANES), tab),
            pl.BlockSpec((tm, LANES), tab),
            pl.BlockSpec((LANES, tm), tab_t),
            pl.BlockSpec((LANES, tm), tab_t),
        ],
        out_specs=(qt_spec, k_spec, vt_spec, fu_spec)
        + tuple(pl.BlockSpec((tm, s.shape[1]), row) for s in flat_shapes),
        out_shape=(qt_shape, k_shape, vt_shape, fu_shape) + flat_shapes,
        compiler_params=_params(1),
        name="in_proj",
    )(x2d, wqv_t, w_rest, cos128, sin128, cos_t, sin_t)


ATTN_SUB = 128
ATTN_GROUP = 4


def _attn_kernel(lam_ref, g_ref, qt_ref, k_ref, vt_ref, o_ref,
                 sa_ref, sb_ref, pa_ref, pb_ref, mx_ref, coef_ref, ot_ref, *, lam_init):
    tq, kc, sub = ATTN_TQ, ATTN_KC, ATTN_SUB
    seq = k_ref.shape[2]
    tiles = seq // tq
    n_units = DIFF_HEADS * tiles
    n_chunks = seq // kc
    lp = lam_ref[...]
    lam = (jnp.exp(jnp.sum(lp[0:1] * lp[1:2], axis=1, keepdims=True))
           - jnp.exp(jnp.sum(lp[2:3] * lp[3:4], axis=1, keepdims=True)) + lam_init)
    gain = g_ref[...] * (1.0 - lam_init)
    feature = lax.broadcasted_iota(jnp.int32, (LANES, tq), 0)
    first_map = feature < DIFF_QK_DIM
    s_bufs = (sa_ref, sb_ref)
    p_bufs = (pa_ref, pb_ref)

    n_groups = n_chunks // ATTN_GROUP
    slabs_per_trip = (tq // LANES) // n_groups
    assert slabs_per_trip * n_groups * LANES == tq

    def step(k, par, do_a=True, do_b=True, do_c=True, do_d=True):
        s_w, s_r = s_bufs[par], s_bufs[1 - par]
        p_w, p_r = p_bufs[1 - par], p_bufs[par]
        if do_d:
            h_d = (k - 3) // tiles
            q_base = ((k - 3) % tiles) * tq
        if do_a:
            h_a = k // tiles
            qt = qt_ref[0, h_a, k % tiles]
            zero = jnp.zeros_like(qt)
            qw = (jnp.where(first_map, qt, zero), jnp.where(first_map, zero, qt))
        if do_b:
            m8 = (mx_ref[1 - par, 0], mx_ref[1 - par, 1])
        if do_c:
            h_c = (k - 2) // tiles
            c16 = (coef_ref[par, 0], coef_ref[par, 1])

        def chunk(c, carry):
            mx = [carry[0], carry[1]]
            ls = [carry[2], carry[3]]
            acc = carry[4]
            for part in range(kc // sub):
                rows = pl.ds(pl.multiple_of(c * kc + part * sub, sub), sub)
                if do_a:
                    keys = k_ref[0, h_a, rows, :]
                    for m in range(2):
                        st = _dot(keys, qw[m])
                        s_w[m, rows, :] = st
                        mx[m] = jnp.maximum(mx[m], jnp.max(st.reshape(sub // SUBLANES, SUBLANES, tq), axis=0))
                if do_b:
                    for m in range(2):
                        p = jnp.exp2(s_r[m, rows, :].reshape(sub // SUBLANES, SUBLANES, tq) - m8[m][None])
                        ls[m] = ls[m] + jnp.sum(p, axis=0)
                        p_w[m, rows, :] = p.reshape(sub, tq).astype(BF16)
            if do_c:
                rows = pl.ds(pl.multiple_of(c * kc, kc), kc)
                packed = (kc // PACKED_SUBLANES, PACKED_SUBLANES, tq)
                a = p_r[0, rows, :].reshape(packed) * c16[0][None] - p_r[1, rows, :].reshape(packed) * c16[1][None]
                part_o = _dot(vt_ref[0, h_c, c], a.reshape(kc, tq))
                acc = part_o if acc is None else acc + part_o
            return mx[0], mx[1], ls[0], ls[1], acc

        lowest = jnp.full((SUBLANES, tq), -jnp.inf, F32)
        nothing = jnp.zeros((SUBLANES, tq), F32)
        def group(gi, carry):
            if do_d:
                for slab in range(slabs_per_trip):
                    si = gi * slabs_per_trip + slab
                    o = ot_ref[1 - par, si].T
                    ms = jnp.mean(o * o, axis=1, keepdims=True)
                    q_start = q_base + si * LANES
                    q_rows = pl.ds(q_start if isinstance(q_start, int) else pl.multiple_of(q_start, LANES),
                                   LANES)
                    o_ref[0, h_d, q_rows, :] = (o * lax.rsqrt(ms + LN_EPS) * gain).astype(BF16)
            stats = carry + (None,)
            for c in range(ATTN_GROUP):
                stats = chunk(gi * ATTN_GROUP + c, stats)
            if do_c:
                for si in range(tq // LANES):
                    ot_ref[par, si] += stats[4][:, si * LANES:(si + 1) * LANES]
            return stats[:4]

        if do_c:
            ot_ref[par] = jnp.zeros(ot_ref.shape[1:], F32)
        mx0, mx1, ls0, ls1 = lax.fori_loop(0, n_groups, group, (lowest, lowest, nothing, nothing))
        if do_a:
            mx_ref[par, 0] = jnp.broadcast_to(jnp.max(mx0, axis=0, keepdims=True), (SUBLANES, tq))
            mx_ref[par, 1] = jnp.broadcast_to(jnp.max(mx1, axis=0, keepdims=True), (SUBLANES, tq))
        if do_b:
            l0 = jnp.sum(ls0, axis=0, keepdims=True)
            l1 = jnp.sum(ls1, axis=0, keepdims=True)
            coef_ref[1 - par, 0] = jnp.broadcast_to(1.0 / l0, (PACKED_SUBLANES, tq)).astype(BF16)
            coef_ref[1 - par, 1] = jnp.broadcast_to(lam / l1, (PACKED_SUBLANES, tq)).astype(BF16)

    def step_pair(j, carry):
        step(2 * j, 0)
        step(2 * j + 1, 1)
        return carry

    step(0, 0, do_b=False, do_c=False, do_d=False)
    step(1, 1, do_c=False, do_d=False)
    step(2, 0, do_d=False)
    step(3, 1)
    lax.fori_loop(2, n_units // 2, step_pair, 0)
    step(n_units, 0, do_a=False)
    step(n_units + 1, 1, do_a=False, do_b=False)
    step(n_units + 2, 0, do_a=False, do_b=False, do_c=False)


def _diff_attention(qt, k, vt, lam_params, g, lam_init):
    b, _, s, _ = k.shape
    whole = lambda a: pl.BlockSpec((1,) + a.shape[1:], lambda bi: (bi,) + (0,) * (a.ndim - 1))
    return pl.pallas_call(
        functools.partial(_attn_kernel, lam_init=lam_init),
        grid=(b,),
        in_specs=[
            pl.BlockSpec((4, DIFF_QK_DIM), lambda bi: (0, 0)),
            pl.BlockSpec((1, DIFF_V_DIM), lambda bi: (0, 0)),
            whole(qt), whole(k), whole(vt),
        ],
        out_specs=whole(k),
        out_shape=jax.ShapeDtypeStruct((b, DIFF_HEADS, s, LANES), BF16),
        scratch_shapes=[
            pltpu.VMEM((2, s, ATTN_TQ), F32), pltpu.VMEM((2, s, ATTN_TQ), F32),
            pltpu.VMEM((2, s, ATTN_TQ), BF16), pltpu.VMEM((2, s, ATTN_TQ), BF16),
            pltpu.VMEM((2, 2, SUBLANES, ATTN_TQ), F32),
            pltpu.VMEM((2, 2, PACKED_SUBLANES, ATTN_TQ), BF16),
            pltpu.VMEM((2, ATTN_TQ // LANES, DIFF_V_DIM, LANES), F32),
        ],
        compiler_params=_params(1),
        name="diff_attn",
    )(lam_params, g, qt, k, vt)


FOURIER_PACK = 2


def _fourier_kernel(u_ref, cc_ref, sc_ref, fe_ref, fo_ref, w_ref, o_ref, ab_ref, *, scale):
    s = u_ref.shape[2]
    half = s // 2
    w = FOURIER_WIDTH
    for e in range(FOURIER_PACK):
        lanes = slice(e * w, (e + 1) * w)
        for parity in range(2):
            u = jnp.concatenate([u_ref[lh, e, pl.ds(parity, half, stride=2), :] for lh in range(2)],
                                axis=1).astype(BF16)
            ab_ref[parity, 0:half, lanes] = _dot(u, cc_ref[...]).astype(BF16)
            ab_ref[parity, half:s, lanes] = _dot(u, sc_ref[...]).astype(BF16)
    even = _dot(fe_ref[...].astype(BF16), ab_ref[0])
    odd = _dot(fo_ref[...].astype(BF16), ab_ref[1])
    for rows, z in ((slice(0, half), (even + odd) * scale), (slice(half, s), (even - odd) * scale)):
        for e in range(FOURIER_PACK):
            y = _dot(z[:, e * w:(e + 1) * w].astype(BF16), w_ref[...])
            o_ref[e, rows, :] = y.astype(BF16)


def _fourier(u, cc_bd, sc_bd, f_even, f_odd, w_bd):
    _, b, s, _ = u.shape
    pack = FOURIER_PACK
    scale = 1.0 / math.sqrt(s * FOURIER_GROUP_DIM)
    const2 = lambda bi: (0, 0)
    table = pl.BlockSpec((s // 2, s), const2, pipeline_mode=pl.Buffered(1))
    return pl.pallas_call(
        functools.partial(_fourier_kernel, scale=scale),
        grid=(b // pack,),
        in_specs=[
            pl.BlockSpec((2, pack, s, LANES), lambda bi: (0, bi, 0, 0)),
            pl.BlockSpec((FOURIER_WIDTH, FOURIER_WIDTH), const2),
            pl.BlockSpec((FOURIER_WIDTH, FOURIER_WIDTH), const2),
            table,
            table,
            pl.BlockSpec((FOURIER_WIDTH, FOURIER_WIDTH), const2),
        ],
        out_specs=pl.BlockSpec((pack, s, FOURIER_WIDTH), lambda bi: (bi, 0, 0)),
        out_shape=jax.ShapeDtypeStruct((b, s, FOURIER_WIDTH), BF16),
        scratch_shapes=[pltpu.VMEM((2, s, pack * FOURIER_WIDTH), BF16)],
        compiler_params=_params(1),
        name="fourier",
    )(u, cc_bd, sc_bd, f_even, f_odd, w_bd)


def _split3(g):
    hi = g.astype(BF16)
    r1 = g - hi.astype(F32)
    mid = r1.astype(BF16)
    lo = (r1 - mid.astype(F32)).astype(BF16)
    return hi, mid, lo


def _gla_kernel(qk_ref, v_ref, r_ref, z_ref, w2_ref, b2_ref, g_ref, o_ref,
                la_ref, vt_ref, acc_ref, qt_ref, kt_ref, a_ref, kv_ref, dec_ref, st_ref):
    s = v_ref.shape[1]
    c = GLA_CHUNK
    pair = 2 * c
    n_pairs = s // pair
    n_chunks = s // c
    kc = GLA_K_COLS

    logit = _dot(z_ref[0].astype(BF16), w2_ref[...]) + b2_ref[...]
    la_ref[...] = (jnp.minimum(logit, 0.0) - jnp.log(1.0 + jnp.exp(-jnp.abs(logit)))) * (1.0 / GLA_GATE_TAU)
    vt_ref[...] = v_ref[0].T.astype(BF16)

    row_i = lax.broadcasted_iota(jnp.int32, (pair, pair), 0)
    col_i = lax.broadcasted_iota(jnp.int32, (pair, pair), 1)
    same_chunk = (row_i // c) == (col_i // c)
    cum_mats = ((same_chunk & (col_i <= row_i)).astype(BF16), (same_chunk & (col_i >= row_i)).astype(BF16))
    r_hk = lax.broadcasted_iota(jnp.int32, (GLA_HEADS * c, kc), 0) // c
    l_hk = lax.broadcasted_iota(jnp.int32, (GLA_HEADS * c, kc), 1) // GLA_K_DIM
    blk_k = r_hk == l_hk
    r_hv = lax.broadcasted_iota(jnp.int32, (GLA_HEADS * c, GLA_WIDTH), 0) // c
    l_hv = lax.broadcasted_iota(jnp.int32, (GLA_HEADS * c, GLA_WIDTH), 1) // GLA_V_DIM
    blk_v = r_hv == l_hv
    qi = lax.broadcasted_iota(jnp.int32, (c, GLA_HEADS * c), 0)
    kj = lax.broadcasted_iota(jnp.int32, (c, GLA_HEADS * c), 1) % c
    q_scale = GLA_K_DIM ** -0.5

    causal = (kj <= qi, kj >= qi)

    def pair_rows(p):
        start = p * pair
        return pl.ds(start if isinstance(start, int) else pl.multiple_of(start, pair), pair)

    def decayed_qk(p):
        rows = pair_rows(p)
        q2 = qk_ref[0, rows, 0:kc] * q_scale
        k2 = qk_ref[0, rows, kc:2 * kc]
        for d in range(2):
            hi, mid, lo = _split3(la_ref[rows, d * kc:(d + 1) * kc])
            c3 = _dot(cum_mats[d], jnp.concatenate([hi, mid, lo], axis=1))
            bcum = c3[:, 0:kc] + c3[:, kc:2 * kc] + c3[:, 2 * kc:3 * kc]
            eb = jnp.exp(bcum)
            qt_ref[d, rows, :] = (q2 * eb).astype(BF16)
            kt_ref[d, rows, :] = k2 * jnp.exp(-bcum)
            for ci in range(2):
                edge = ci * c if d == 1 else (ci + 1) * c - 1
                dec_ref[d, 2 * p + ci] = jnp.broadcast_to(eb[edge:edge + 1], (SUBLANES, kc))

    def intra_scores(p):
        for ci in range(2):
            start = p * pair + ci * c
            rows = pl.ds(start if isinstance(start, int) else pl.multiple_of(start, c), c)
            for d in range(2):
                kt_bd = jnp.where(blk_k, jnp.concatenate([kt_ref[d, rows, :]] * GLA_HEADS, axis=0), 0.0)
                a = _dot_nt(qt_ref[d, rows, :], kt_bd.astype(BF16))
                a_ref[2 * p + ci, d * c:(d + 1) * c, :] = jnp.where(causal[d], a, 0.0).astype(BF16)

    def intra_out_and_kv(p):
        rows2 = pair_rows(p)
        v2 = v_ref[0, rows2, :]
        vt2 = vt_ref[:, rows2]
        kt2 = jnp.concatenate([kt_ref[0, rows2, :], kt_ref[1, rows2, :]], axis=1)
        for ci in range(2):
            sl = slice(ci * c, (ci + 1) * c)
            start = p * pair + ci * c
            rows = pl.ds(start if isinstance(start, int) else pl.multiple_of(start, c), c)
            v_bd = jnp.where(blk_v, jnp.concatenate([v2[sl]] * GLA_HEADS, axis=0), 0.0).astype(BF16)
            o_both = _dot(a_ref[2 * p + ci], v_bd)
            acc_ref[rows, :] = o_both[0:c] + o_both[c:2 * c]
            gap = jnp.zeros((c, 2 * kc), F32)
            kt_pair = jnp.concatenate([kt2[sl], gap] if ci == 0 else [gap, kt2[sl]], axis=0).astype(BF16)
            kv = _dot(vt2, kt_pair)
            kv_ref[0, 2 * p + ci] = jnp.where(blk_k, kv[:, 0:kc], 0.0)
            kv_ref[1, 2 * p + ci] = jnp.where(blk_k, kv[:, kc:2 * kc], 0.0)

    decayed_qk(0)
    intra_scores(0)
    decayed_qk(1)

    def intra_step(p, carry):
        intra_out_and_kv(p)
        intra_scores(p + 1)
        decayed_qk(p + 2)
        return carry

    lax.fori_loop(0, n_pairs - 2, intra_step, 0, unroll=2)
    intra_out_and_kv(n_pairs - 2)
    intra_scores(n_pairs - 1)
    intra_out_and_kv(n_pairs - 1)

    st_ref[...] = jnp.zeros_like(st_ref)

    def scan_step(i, carry):
        for d in range(2):
            n = i if d == 0 else n_chunks - 1 - i
            rows = pl.ds(pl.multiple_of(n * c, c), c)
            st = st_ref[d]
            acc_ref[rows, :] += _dot_nt(qt_ref[d, rows, :], st.astype(BF16))
            st = (st + kv_ref[d, n]).reshape(GLA_WIDTH // SUBLANES, SUBLANES, kc) * dec_ref[d, n][None]
            st_ref[d] = st.reshape(GLA_WIDTH, kc)
        return carry

    lax.fori_loop(0, n_chunks, scan_step, 0, unroll=8)

    o = acc_ref[...]
    o2 = o * o
    o2_hi = o2.astype(BF16)
    o2_lo = (o2 - o2_hi.astype(F32)).astype(BF16)
    gi = lax.broadcasted_iota(jnp.int32, (GLA_WIDTH, GLA_WIDTH), 0) // GLA_V_DIM
    gj = lax.broadcasted_iota(jnp.int32, (GLA_WIDTH, GLA_WIDTH), 1) // GLA_V_DIM
    ones_bd = (gi == gj).astype(BF16)
    ms = (_dot(o2_hi, ones_bd) + _dot(o2_lo, ones_bd)) * (1.0 / GLA_V_DIM)
    r = r_ref[0]
    gate = r * (1.0 / (1.0 + jnp.exp(-r)))
    o_ref[0] = (o * lax.rsqrt(ms + LN_EPS) * g_ref[...] * gate).astype(BF16)


def _gla(gqk, gv, gr, gz, w2_bd, b2_cat, g_tiled):
    b, s, _ = gv.shape
    per_b = lambda bi: (bi, 0, 0)
    const = lambda bi: (0, 0)
    return pl.pallas_call(
        _gla_kernel,
        grid=(b,),
        in_specs=[
            pl.BlockSpec((1, s, 2 * GLA_K_COLS), per_b),
            pl.BlockSpec((1, s, GLA_WIDTH), per_b),
            pl.BlockSpec((1, s, GLA_WIDTH), per_b),
            pl.BlockSpec((1, s, 2 * GLA_GATE_RANK), per_b),
            pl.BlockSpec((2 * GLA_GATE_RANK, 2 * GLA_K_COLS), const),
            pl.BlockSpec((1, 2 * GLA_K_COLS), const),
            pl.BlockSpec((1, GLA_WIDTH), const),
        ],
        out_specs=pl.BlockSpec((1, s, GLA_WIDTH), per_b),
        out_shape=jax.ShapeDtypeStruct((b, s, GLA_WIDTH), BF16),
        scratch_shapes=[
            pltpu.VMEM((s, 2 * GLA_K_COLS), F32),
            pltpu.VMEM((GLA_WIDTH, s), BF16),
            pltpu.VMEM((s, GLA_WIDTH), F32),
            pltpu.VMEM((2, s, GLA_K_COLS), BF16),
            pltpu.VMEM((2, s, GLA_K_COLS), F32),
            pltpu.VMEM((s // GLA_CHUNK, 2 * GLA_CHUNK, GLA_WIDTH), BF16),
            pltpu.VMEM((2, s // GLA_CHUNK, GLA_WIDTH, GLA_K_COLS), F32),
            pltpu.VMEM((2, s // GLA_CHUNK, SUBLANES, GLA_K_COLS), F32),
            pltpu.VMEM((2, GLA_WIDTH, GLA_K_COLS), F32),
        ],
        compiler_params=_params(1),
        name="gla",
    )(gqk, gv, gr, gz, w2_bd, b2_cat, g_tiled)


def _layer_norm(y, g, b):
    mu = jnp.mean(y, axis=1, keepdims=True)
    d = y - mu
    var = jnp.mean(d * d, axis=1, keepdims=True)
    return d * lax.rsqrt(var + LN_EPS) * g + b


ROW_SUB = 256


def _outproj_kernel(od_ref, of_ref, og_ref, x_ref, w_ref, g_ref, b_ref, o_ref):
    w = w_ref[...].astype(BF16)
    for r in range(x_ref.shape[0] // ROW_SUB):
        rows = slice(r * ROW_SUB, (r + 1) * ROW_SUB)
        mixed = jnp.concatenate([od_ref[0, hd, rows, :] for hd in range(DIFF_HEADS)]
                                + [of_ref[rows, :], og_ref[rows, :]], axis=1)
        m = _dot(mixed, w)
        o_ref[rows, :] = _layer_norm(DEEPNORM_ALPHA * x_ref[rows, :] + m, g_ref[...], b_ref[...])


def _out_proj(o_diff, o_four, o_gla, x2d, w_all, layer, g, b, tm=1024):
    m = x2d.shape[0]
    seq = o_diff.shape[2]
    n_seq_tiles = seq // tm
    row = lambda i: (i, 0)
    const = lambda i: (0, 0)
    return pl.pallas_call(
        _outproj_kernel,
        grid=(m // tm,),
        in_specs=[
            pl.BlockSpec((1, DIFF_HEADS, tm, LANES), lambda i: (i // n_seq_tiles, 0, i % n_seq_tiles, 0)),
            pl.BlockSpec((tm, FOURIER_WIDTH), row),
            pl.BlockSpec((tm, GLA_WIDTH), row),
            pl.BlockSpec((tm, D_MODEL), row),
            _layer_weight(w_all, layer),
            pl.BlockSpec((1, D_MODEL), const),
            pl.BlockSpec((1, D_MODEL), const),
        ],
        out_specs=pl.BlockSpec((tm, D_MODEL), row),
        out_shape=jax.ShapeDtypeStruct((m, D_MODEL), F32),
        compiler_params=_params(1),
        name="out_proj",
    )(o_diff, o_four, o_gla, x2d, w_all, g, b)


FFN_CHUNK = 256


FFN_ROWS = 512


def _ffn_kernel(x_ref, wg_ref, wu_ref, wd_ref, g_ref, b_ref, o_ref):
    sub_tiles = [slice(r * FFN_ROWS, (r + 1) * FFN_ROWS) for r in range(x_ref.shape[0] // FFN_ROWS)]
    xbs = [x_ref[rows, :].astype(BF16) for rows in sub_tiles]
    for ci in range(FFN_HIDDEN // FFN_CHUNK):
        cols = slice(ci * FFN_CHUNK, (ci + 1) * FFN_CHUNK)
        wg = wg_ref[:, cols].astype(BF16)
        wu = wu_ref[:, cols].astype(BF16)
        wd = wd_ref[cols, :].astype(BF16)
        for rows, xb in zip(sub_tiles, xbs):
            hg = _dot(xb, wg)
            hu = _dot(xb, wu)
            act = (hg * (1.0 / (1.0 + jnp.exp(-hg))) * hu).astype(BF16)
            part = _dot(act, wd)
            if ci == 0:
                o_ref[rows, :] = part
            else:
                o_ref[rows, :] += part
    for rows in sub_tiles:
        o_ref[rows, :] = _layer_norm(DEEPNORM_ALPHA * x_ref[rows, :] + o_ref[rows, :], g_ref[...], b_ref[...])


def _ffn(x2d, wg_all, wu_all, wd_all, layer, g, b, tm=1024):
    m = x2d.shape[0]
    row = lambda i: (i, 0)
    const = lambda i: (0, 0)
    return pl.pallas_call(
        _ffn_kernel,
        grid=(m // tm,),
        in_specs=[
            pl.BlockSpec((tm, D_MODEL), row),
            _layer_weight(wg_all, layer),
            _layer_weight(wu_all, layer),
            _layer_weight(wd_all, layer),
            pl.BlockSpec((1, D_MODEL), const),
            pl.BlockSpec((1, D_MODEL), const),
        ],
        out_specs=pl.BlockSpec((tm, D_MODEL), row),
        out_shape=jax.ShapeDtypeStruct((m, D_MODEL), F32),
        compiler_params=_params(1, FFN_VMEM_LIMIT),
        name="ffn",
    )(x2d, wg_all, wu_all, wd_all, g, b)


def _rope_tables(seq):
    half = DIFF_QK_DIM // 2
    pos = jnp.arange(seq, dtype=F32)
    inv_freq = ROPE_THETA ** (-jnp.arange(0, DIFF_QK_DIM, 2, dtype=F32) / DIFF_QK_DIM)
    ang = pos[:, None] * inv_freq[None, :]
    cos, sin = jnp.cos(ang), jnp.sin(ang)
    reps = LANES // DIFF_QK_DIM
    cos128 = jnp.tile(jnp.concatenate([cos, cos], axis=1), (1, reps))
    sin128 = jnp.tile(jnp.concatenate([-sin, sin], axis=1), (1, reps))
    assert cos128.shape == (seq, LANES) and half * 2 == DIFF_QK_DIM
    return cos128, sin128, cos128.T, sin128.T


def _dft_tables(seq):
    half = seq // 2
    j = np.arange(half, dtype=np.int64)[:, None]
    pos = np.arange(half, dtype=np.int64)[None, :]
    ang_even = 2.0 * np.pi * ((j * pos) % half).astype(np.float64) / half
    ang_odd = 2.0 * np.pi * ((j * (2 * pos + 1)) % seq).astype(np.float64) / seq
    f_even = np.concatenate([np.cos(ang_even), -np.sin(ang_even)], axis=1)
    f_odd = np.concatenate([np.cos(ang_odd), -np.sin(ang_odd)], axis=1)
    c = np.arange(FOURIER_GROUP_DIM, dtype=np.int64)
    ang_c = 2.0 * np.pi * ((c[:, None] * c[None, :]) % FOURIER_GROUP_DIM).astype(np.float64) / FOURIER_GROUP_DIM
    eye = np.eye(FOURIER_GROUPS)
    cc_bd = np.kron(eye, np.cos(ang_c))
    sc_bd = np.kron(eye, np.sin(ang_c))
    return (jnp.asarray(f_even, dtype=F32), jnp.asarray(f_odd, dtype=F32),
            jnp.asarray(cc_bd, dtype=F32).astype(BF16), jnp.asarray(sc_bd, dtype=F32).astype(BF16))


def _block_diag(blocks):
    n = len(blocks)
    rows = []
    for i, blk in enumerate(blocks):
        rows.append(jnp.concatenate(
            [blk if j == i else jnp.zeros((blk.shape[0], blocks[j].shape[1]), blk.dtype) for j in range(n)],
            axis=1))
    return jnp.concatenate(rows, axis=0)


def kernel(x, w_in, diff_lambda, diff_norm_g, fourier_w, gla_gate_w2, gla_gate_b2, gla_norm_g, w_out,
           ln1_g, ln1_b, ffn_w_gate, ffn_w_up, ffn_w_down, ln2_g, ln2_b):
    b, s, d = x.shape
    m = b * s
    rope = _rope_tables(s)
    f_even, f_odd, cc_bd, sc_bd = _dft_tables(s)
    x2d = x.reshape(m, d)
    for l in range(DEPTH):
        lam_init = 0.8 - 0.6 * math.exp(-0.3 * l)
        wqv_t = jnp.concatenate([w_in[l, :, OFF_DQ:OFF_DK], w_in[l, :, OFF_DV:OFF_FU]], axis=1).T.astype(BF16)
        w_rest = jnp.concatenate([w_in[l, :, OFF_DK:OFF_DV], w_in[l, :, OFF_FU:]], axis=1).astype(BF16)
        qt, k, vt, fu, gqk, gv, gr, gz = _in_proj(x2d, wqv_t, w_rest, rope, s)
        o_diff = _diff_attention(qt, k, vt, diff_lambda[l], diff_norm_g[l].reshape(1, -1), lam_init)
        w_four = _block_diag([fourier_w[l, g] for g in range(FOURIER_GROUPS)]).astype(BF16)
        o_four = _fourier(fu.reshape(-1, b, s, LANES), cc_bd, sc_bd, f_even, f_odd, w_four)
        w2_bd = _block_diag([gla_gate_w2[l, 0], gla_gate_w2[l, 1]]).astype(BF16)
        b2_cat = gla_gate_b2[l].reshape(1, -1)
        g_gla = jnp.tile(gla_norm_g[l], GLA_HEADS).reshape(1, -1)
        o_gla = _gla(gqk.reshape(b, s, -1), gv.reshape(b, s, -1), gr.reshape(b, s, -1),
                     gz.reshape(b, s, -1), w2_bd, b2_cat, g_gla)
        x2d = _out_proj(o_diff, o_four.reshape(m, -1), o_gla.reshape(m, -1), x2d,
                        w_out, l, ln1_g[l].reshape(1, -1), ln1_b[l].reshape(1, -1))
        x2d = _ffn(x2d, ffn_w_gate, ffn_w_up, ffn_w_down, l,
                   ln2_g[l].reshape(1, -1), ln2_b[l].reshape(1, -1))
    return x2d.reshape(b, s, d)
```

```python
import functools
import math

import jax
import jax.numpy as jnp
import numpy as np
from jax import lax
from jax.experimental import pallas as pl
from jax.experimental.pallas import tpu as pltpu

D_MODEL = 1024
DEPTH = 2
DIFF_HEADS = 4
DIFF_QK_DIM = 64
DIFF_V_DIM = 128
DIFF_WIDTH = 512
DIFF_QK_COLS = 512
ROPE_THETA = 10000.0
FOURIER_GROUPS = 4
FOURIER_GROUP_DIM = 64
FOURIER_WIDTH = 256
GLA_HEADS = 4
GLA_V_DIM = 64
GLA_K_DIM = 32
GLA_WIDTH = 256
GLA_K_COLS = 128
GLA_GATE_RANK = 16
GLA_GATE_TAU = 16.0
GLA_CHUNK = 64
FFN_HIDDEN = 2816
DEEPNORM_ALPHA = (2 * DEPTH) ** 0.25
LN_EPS = 1e-5

OFF_DQ = 0
OFF_DK = OFF_DQ + DIFF_QK_COLS
OFF_DV = OFF_DK + DIFF_QK_COLS
OFF_FU = OFF_DV + DIFF_WIDTH
OFF_GQ = OFF_FU + FOURIER_WIDTH
OFF_GV = OFF_GQ + 2 * GLA_K_COLS
OFF_GR = OFF_GV + GLA_WIDTH
OFF_GZ = OFF_GR + GLA_WIDTH
IN_WIDTH = OFF_GZ + 2 * GLA_GATE_RANK

LANES = 128
SUBLANES = 8
PACKED_SUBLANES = 16
VMEM_LIMIT = 56 * 1024 * 1024
FFN_VMEM_LIMIT = 60 * 1024 * 1024

BF16 = jnp.bfloat16
F32 = jnp.float32


def _dot(a, b):
    return jnp.dot(a, b, preferred_element_type=F32)


def _dot_nt(a, b):
    return lax.dot_general(a, b, (((1,), (1,)), ((), ())), preferred_element_type=F32)


def _params(n_grid_dims, vmem_limit=VMEM_LIMIT):
    return pltpu.CompilerParams(
        dimension_semantics=("arbitrary",) * n_grid_dims,
        vmem_limit_bytes=vmem_limit)


def _rope_slab(t, cos, sin_signed, first_half):
    half = DIFF_QK_DIM // 2
    swapped = jnp.where(first_half, pltpu.roll(t, LANES - half, 1), pltpu.roll(t, half, 1))
    return t * cos + swapped * sin_signed


ATTN_TQ = 256
ATTN_KC = 256


def _rest_col(col):
    assert col >= OFF_DK and not OFF_DV <= col < OFF_FU
    return col - OFF_DK if col < OFF_DV else col - OFF_DK - DIFF_WIDTH


def _inproj_kernel(x_ref, wqv_ref, w_ref, cos_ref, sin_ref, cost_ref, sint_ref,
                   qt_ref, k_ref, vt_ref, fu_ref, gqk_ref, gv_ref, gr_ref, gz_ref):
    xb = x_ref[...].astype(BF16)
    tm = xb.shape[0]
    cos = cos_ref[...]
    sin_signed = sin_ref[...]
    lane = lax.broadcasted_iota(jnp.int32, cos.shape, 1)
    first_half = (lane % DIFF_QK_DIM) < (DIFF_QK_DIM // 2)
    qk_scale = DIFF_QK_DIM ** -0.5 * math.log2(math.e)
    hqv_t = _dot_nt(wqv_ref[...], xb)
    cos_t = cost_ref[...]
    sin_t = sint_ref[...]
    half = DIFF_QK_DIM // 2
    wcols = lambda lo, hi: w_ref[:, _rest_col(lo):_rest_col(hi - 1) + 1]
    hk = _dot(xb, wcols(OFF_DK, OFF_DV))
    for hd in range(DIFF_HEADS):
        qt = hqv_t[hd * LANES:(hd + 1) * LANES]
        swapped = jnp.concatenate(
            [qt[half:2 * half], qt[0:half], qt[3 * half:4 * half], qt[2 * half:3 * half]], axis=0)
        qt = ((qt * cos_t + swapped * sin_t) * qk_scale).astype(BF16)
        vt = hqv_t[DIFF_QK_COLS + hd * LANES:DIFF_QK_COLS + (hd + 1) * LANES].astype(BF16)
        for j in range(tm // ATTN_TQ):
            qt_ref[0, hd, j] = qt[:, j * ATTN_TQ:(j + 1) * ATTN_TQ]
        for j in range(tm // ATTN_KC):
            vt_ref[0, hd, j] = vt[:, j * ATTN_KC:(j + 1) * ATTN_KC]
        slab = slice(hd * LANES, (hd + 1) * LANES)
        k_ref[0, hd] = _rope_slab(hk[:, slab], cos, sin_signed, first_half).astype(BF16)
    hfu = _dot(xb, wcols(OFF_FU, OFF_GQ))
    for lane_half in range(FOURIER_WIDTH // LANES):
        fu_ref[lane_half] = hfu[:, lane_half * LANES:(lane_half + 1) * LANES]
    gqk_ref[...] = _dot(xb, wcols(OFF_GQ, OFF_GV))
    gv_ref[...] = _dot(xb, wcols(OFF_GV, OFF_GR))
    gr_ref[...] = _dot(xb, wcols(OFF_GR, OFF_GZ))
    gz_ref[...] = _dot(xb, wcols(OFF_GZ, IN_WIDTH))


def _layer_weight(w_all, layer):
    return pl.BlockSpec((None,) + w_all.shape[1:], lambda i: (layer,) + (0,) * (w_all.ndim - 1),
                        pipeline_mode=pl.Buffered(1))


def _in_proj(x2d, wqv_t, w_rest, rope, seq, tm=1024):
    m = x2d.shape[0]
    batch = m // seq
    n_seq_tiles = seq // tm
    cos128, sin128, cos_t, sin_t = rope
    row = lambda i: (i, 0)
    const = lambda i: (0, 0)
    tab = lambda i: (i % n_seq_tiles, 0)
    tab_t = lambda i: (0, i % n_seq_tiles)
    tiled = lambda t: (jax.ShapeDtypeStruct((batch, DIFF_HEADS, seq // t, LANES, t), BF16),
                       pl.BlockSpec((1, DIFF_HEADS, tm // t, LANES, t),
                                    lambda i: (i // n_seq_tiles, 0, i % n_seq_tiles, 0, 0)))
    qt_shape, qt_spec = tiled(ATTN_TQ)
    vt_shape, vt_spec = tiled(ATTN_KC)
    k_shape = jax.ShapeDtypeStruct((batch, DIFF_HEADS, seq, LANES), BF16)
    k_spec = pl.BlockSpec((1, DIFF_HEADS, tm, LANES), lambda i: (i // n_seq_tiles, 0, i % n_seq_tiles, 0))
    fu_shape = jax.ShapeDtypeStruct((FOURIER_WIDTH // LANES, m, LANES), F32)
    fu_spec = pl.BlockSpec((FOURIER_WIDTH // LANES, tm, LANES), lambda i: (0, i, 0))
    flat_shapes = (
        jax.ShapeDtypeStruct((m, 2 * GLA_K_COLS), F32),
        jax.ShapeDtypeStruct((m, GLA_WIDTH), F32),
        jax.ShapeDtypeStruct((m, GLA_WIDTH), F32),
        jax.ShapeDtypeStruct((m, 2 * GLA_GATE_RANK), F32),
    )
    return pl.pallas_call(
        _inproj_kernel,
        grid=(m // tm,),
        in_specs=[
            pl.BlockSpec((tm, D_MODEL), row),
            pl.BlockSpec((DIFF_QK_COLS + DIFF_WIDTH, D_MODEL), const),
            pl.BlockSpec(w_rest.shape, const),
            pl.BlockSpec((tm, LANES), tab),
            pl.BlockSpec((tm, LANES), tab),
            pl.BlockSpec((LANES, tm), tab_t),
            pl.BlockSpec((LANES, tm), tab_t),
        ],
        out_specs=(qt_spec, k_spec, vt_spec, fu_spec)
        + tuple(pl.BlockSpec((tm, s.shape[1]), row) for s in flat_shapes),
        out_shape=(qt_shape, k_shape, vt_shape, fu_shape) + flat_shapes,
        compiler_params=_params(1),
        name="in_proj",
    )(x2d, wqv_t, w_rest, cos128, sin128, cos_t, sin_t)


ATTN_SUB = 128
ATTN_GROUP = 4


def _attn_kernel(lam_ref, g_ref, qt_ref, k_ref, vt_ref, o_ref,
                 sa_ref, sb_ref, pa_ref, pb_ref, mx_ref, coef_ref, ot_ref, *, lam_init):
    tq, kc, sub = ATTN_TQ, ATTN_KC, ATTN_SUB
    seq = k_ref.shape[2]
    tiles = seq // tq
    n_units = DIFF_HEADS * tiles
    n_chunks = seq // kc
    lp = lam_ref[...]
    lam = (jnp.exp(jnp.sum(lp[0:1] * lp[1:2], axis=1, keepdims=True))
           - jnp.exp(jnp.sum(lp[2:3] * lp[3:4], axis=1, keepdims=True)) + lam_init)
    gain = g_ref[...] * (1.0 - lam_init)
    feature = lax.broadcasted_iota(jnp.int32, (LANES, tq), 0)
    first_map = feature < DIFF_QK_DIM
    s_bufs = (sa_ref, sb_ref)
    p_bufs = (pa_ref, pb_ref)

    n_groups = n_chunks // ATTN_GROUP
    slabs_per_trip = (tq // LANES) // n_groups
    assert slabs_per_trip * n_groups * LANES == tq

    def step(k, par, do_a=True, do_b=True, do_c=True, do_d=True):
        s_w, s_r = s_bufs[par], s_bufs[1 - par]
        p_w, p_r = p_bufs[1 - par], p_bufs[par]
        if do_d:
            h_d = (k - 3) // tiles
            q_base = ((k - 3) % tiles) * tq
        if do_a:
            h_a = k // tiles
            qt = qt_ref[0, h_a, k % tiles]
            zero = jnp.zeros_like(qt)
            qw = (jnp.where(first_map, qt, zero), jnp.where(first_map, zero, qt))
        if do_b:
            m8 = (mx_ref[1 - par, 0], mx_ref[1 - par, 1])
        if do_c:
            h_c = (k - 2) // tiles
            c16 = (coef_ref[par, 0], coef_ref[par, 1])

        def chunk(c, carry):
            mx = [carry[0], carry[1]]
            ls = [carry[2], carry[3]]
            acc = carry[4]
            for part in range(kc // sub):
                rows = pl.ds(pl.multiple_of(c * kc + part * sub, sub), sub)
                if do_a:
                    keys = k_ref[0, h_a, rows, :]
                    for m in range(2):
                        st = _dot(keys, qw[m])
                        s_w[m, rows, :] = st
                        mx[m] = jnp.maximum(mx[m], jnp.max(st.reshape(sub // SUBLANES, SUBLANES, tq), axis=0))
                if do_b:
                    for m in range(2):
                        p = jnp.exp2(s_r[m, rows, :].reshape(sub // SUBLANES, SUBLANES, tq) - m8[m][None])
                        ls[m] = ls[m] + jnp.sum(p, axis=0)
                        p_w[m, rows, :] = p.reshape(sub, tq).astype(BF16)
            if do_c:
                rows = pl.ds(pl.multiple_of(c * kc, kc), kc)
                packed = (kc // PACKED_SUBLANES, PACKED_SUBLANES, tq)
                a = p_r[0, rows, :].reshape(packed) * c16[0][None] - p_r[1, rows, :].reshape(packed) * c16[1][None]
                part_o = _dot(vt_ref[0, h_c, c], a.reshape(kc, tq))
                acc = part_o if acc is None else acc + part_o
            return mx[0], mx[1], ls[0], ls[1], acc

        lowest = jnp.full((SUBLANES, tq), -jnp.inf, F32)
        nothing = jnp.zeros((SUBLANES, tq), F32)
        def group(gi, carry):
            if do_d:
                for slab in range(slabs_per_trip):
                    si = gi * slabs_per_trip + slab
                    o = ot_ref[1 - par, si].T
                    ms = jnp.mean(o * o, axis=1, keepdims=True)
                    q_start = q_base + si * LANES
                    q_rows = pl.ds(q_start if isinstance(q_start, int) else pl.multiple_of(q_start, LANES),
                                   LANES)
                    o_ref[0, h_d, q_rows, :] = (o * lax.rsqrt(ms + LN_EPS) * gain).astype(BF16)
            stats = carry + (None,)
            for c in range(ATTN_GROUP):
                stats = chunk(gi * ATTN_GROUP + c, stats)
            if do_c:
                for si in range(tq // LANES):
                    ot_ref[par, si] += stats[4][:, si * LANES:(si + 1) * LANES]
            return stats[:4]

        if do_c:
            ot_ref[par] = jnp.zeros(ot_ref.shape[1:], F32)
        mx0, mx1, ls0, ls1 = lax.fori_loop(0, n_groups, group, (lowest, lowest, nothing, nothing))
        if do_a:
            mx_ref[par, 0] = jnp.broadcast_to(jnp.max(mx0, axis=0, keepdims=True), (SUBLANES, tq))
            mx_ref[par, 1] = jnp.broadcast_to(jnp.max(mx1, axis=0, keepdims=True), (SUBLANES, tq))
        if do_b:
            l0 = jnp.sum(ls0, axis=0, keepdims=True)
            l1 = jnp.sum(ls1, axis=0, keepdims=True)
            coef_ref[1 - par, 0] = jnp.broadcast_to(1.0 / l0, (PACKED_SUBLANES, tq)).astype(BF16)
            coef_ref[1 - par, 1] = jnp.broadcast_to(lam / l1, (PACKED_SUBLANES, tq)).astype(BF16)

    def step_pair(j, carry):
        step(2 * j, 0)
        step(2 * j + 1, 1)
        return carry

    step(0, 0, do_b=False, do_c=False, do_d=False)
    step(1, 1, do_c=False, do_d=False)
    step(2, 0, do_d=False)
    step(3, 1)
    lax.fori_loop(2, n_units // 2, step_pair, 0)
    step(n_units, 0, do_a=False)
    step(n_units + 1, 1, do_a=False, do_b=False)
    step(n_units + 2, 0, do_a=False, do_b=False, do_c=False)


def _diff_attention(qt, k, vt, lam_params, g, lam_init):
    b, _, s, _ = k.shape
    whole = lambda a: pl.BlockSpec((1,) + a.shape[1:], lambda bi: (bi,) + (0,) * (a.ndim - 1))
    return pl.pallas_call(
        functools.partial(_attn_kernel, lam_init=lam_init),
        grid=(b,),
        in_specs=[
            pl.BlockSpec((4, DIFF_QK_DIM), lambda bi: (0, 0)),
            pl.BlockSpec((1, DIFF_V_DIM), lambda bi: (0, 0)),
            whole(qt), whole(k), whole(vt),
        ],
        out_specs=whole(k),
        out_shape=jax.ShapeDtypeStruct((b, DIFF_HEADS, s, LANES), BF16),
        scratch_shapes=[
            pltpu.VMEM((2, s, ATTN_TQ), F32), pltpu.VMEM((2, s, ATTN_TQ), F32),
            pltpu.VMEM((2, s, ATTN_TQ), BF16), pltpu.VMEM((2, s, ATTN_TQ), BF16),
            pltpu.VMEM((2, 2, SUBLANES, ATTN_TQ), F32),
            pltpu.VMEM((2, 2, PACKED_SUBLANES, ATTN_TQ), BF16),
            pltpu.VMEM((2, ATTN_TQ // LANES, DIFF_V_DIM, LANES), F32),
        ],
        compiler_params=_params(1),
        name="diff_attn",
    )(lam_params, g, qt, k, vt)


FOURIER_PACK = 2


def _fourier_kernel(u_ref, cc_ref, sc_ref, fe_ref, fo_ref, w_ref, o_ref, ab_ref, *, scale):
    s = u_ref.shape[2]
    half = s // 2
    w = FOURIER_WIDTH
    for e in range(FOURIER_PACK):
        lanes = slice(e * w, (e + 1) * w)
        for parity in range(2):
            u = jnp.concatenate([u_ref[lh, e, pl.ds(parity, half, stride=2), :] for lh in range(2)],
                                axis=1).astype(BF16)
            ab_ref[parity, 0:half, lanes] = _dot(u, cc_ref[...]).astype(BF16)
            ab_ref[parity, half:s, lanes] = _dot(u, sc_ref[...]).astype(BF16)
    even = _dot(fe_ref[...].astype(BF16), ab_ref[0])
    odd = _dot(fo_ref[...].astype(BF16), ab_ref[1])
    for rows, z in ((slice(0, half), (even + odd) * scale), (slice(half, s), (even - odd) * scale)):
        for e in range(FOURIER_PACK):
            y = _dot(z[:, e * w:(e + 1) * w].astype(BF16), w_ref[...])
            o_ref[e, rows, :] = y.astype(BF16)


def _fourier(u, cc_bd, sc_bd, f_even, f_odd, w_bd):
    _, b, s, _ = u.shape
    pack = FOURIER_PACK
    scale = 1.0 / math.sqrt(s * FOURIER_GROUP_DIM)
    const2 = lambda bi: (0, 0)
    table = pl.BlockSpec((s // 2, s), const2, pipeline_mode=pl.Buffered(1))
    return pl.pallas_call(
        functools.partial(_fourier_kernel, scale=scale),
        grid=(b // pack,),
        in_specs=[
            pl.BlockSpec((2, pack, s, LANES), lambda bi: (0, bi, 0, 0)),
            pl.BlockSpec((FOURIER_WIDTH, FOURIER_WIDTH), const2),
            pl.BlockSpec((FOURIER_WIDTH, FOURIER_WIDTH), const2),
            table,
            table,
            pl.BlockSpec((FOURIER_WIDTH, FOURIER_WIDTH), const2),
        ],
        out_specs=pl.BlockSpec((pack, s, FOURIER_WIDTH), lambda bi: (bi, 0, 0)),
        out_shape=jax.ShapeDtypeStruct((b, s, FOURIER_WIDTH), BF16),
        scratch_shapes=[pltpu.VMEM((2, s, pack * FOURIER_WIDTH), BF16)],
        compiler_params=_params(1),
        name="fourier",
    )(u, cc_bd, sc_bd, f_even, f_odd, w_bd)


def _split3(g):
    hi = g.astype(BF16)
    r1 = g - hi.astype(F32)
    mid = r1.astype(BF16)
    lo = (r1 - mid.astype(F32)).astype(BF16)
    return hi, mid, lo


def _gla_kernel(qk_ref, v_ref, r_ref, z_ref, w2_ref, b2_ref, g_ref, o_ref,
                la_ref, vt_ref, acc_ref, qt_ref, kt_ref, a_ref, kv_ref, dec_ref, st_ref):
    s = v_ref.shape[1]
    c = GLA_CHUNK
    pair = 2 * c
    n_pairs = s // pair
    n_chunks = s // c
    kc = GLA_K_COLS

    logit = _dot(z_ref[0].astype(BF16), w2_ref[...]) + b2_ref[...]
    la_ref[...] = (jnp.minimum(logit, 0.0) - jnp.log(1.0 + jnp.exp(-jnp.abs(logit)))) * (1.0 / GLA_GATE_TAU)
    vt_ref[...] = v_ref[0].T.astype(BF16)

    row_i = lax.broadcasted_iota(jnp.int32, (pair, pair), 0)
    col_i = lax.broadcasted_iota(jnp.int32, (pair, pair), 1)
    same_chunk = (row_i // c) == (col_i // c)
    cum_mats = ((same_chunk & (col_i <= row_i)).astype(BF16), (same_chunk & (col_i >= row_i)).astype(BF16))
    r_hk = lax.broadcasted_iota(jnp.int32, (GLA_HEADS * c, kc), 0) // c
    l_hk = lax.broadcasted_iota(jnp.int32, (GLA_HEADS * c, kc), 1) // GLA_K_DIM
    blk_k = r_hk == l_hk
    r_hv = lax.broadcasted_iota(jnp.int32, (GLA_HEADS * c, GLA_WIDTH), 0) // c
    l_hv = lax.broadcasted_iota(jnp.int32, (GLA_HEADS * c, GLA_WIDTH), 1) // GLA_V_DIM
    blk_v = r_hv == l_hv
    qi = lax.broadcasted_iota(jnp.int32, (c, GLA_HEADS * c), 0)
    kj = lax.broadcasted_iota(jnp.int32, (c, GLA_HEADS * c), 1) % c
    q_scale = GLA_K_DIM ** -0.5

    causal = (kj <= qi, kj >= qi)

    def pair_rows(p):
        start = p * pair
        return pl.ds(start if isinstance(start, int) else pl.multiple_of(start, pair), pair)

    def decayed_qk(p):
        rows = pair_rows(p)
        q2 = qk_ref[0, rows, 0:kc] * q_scale
        k2 = qk_ref[0, rows, kc:2 * kc]
        for d in range(2):
            hi, mid, lo = _split3(la_ref[rows, d * kc:(d + 1) * kc])
            c3 = _dot(cum_mats[d], jnp.concatenate([hi, mid, lo], axis=1))
            bcum = c3[:, 0:kc] + c3[:, kc:2 * kc] + c3[:, 2 * kc:3 * kc]
            eb = jnp.exp(bcum)
            qt_ref[d, rows, :] = (q2 * eb).astype(BF16)
            kt_ref[d, rows, :] = k2 * jnp.exp(-bcum)
            for ci in range(2):
                edge = ci * c if d == 1 else (ci + 1) * c - 1
                dec_ref[d, 2 * p + ci] = jnp.broadcast_to(eb[edge:edge + 1], (SUBLANES, kc))

    def intra_scores(p):
        for ci in range(2):
            start = p * pair + ci * c
            rows = pl.ds(start if isinstance(start, int) else pl.multiple_of(start, c), c)
            for d in range(2):
                kt_bd = jnp.where(blk_k, jnp.concatenate([kt_ref[d, rows, :]] * GLA_HEADS, axis=0), 0.0)
                a = _dot_nt(qt_ref[d, rows, :], kt_bd.astype(BF16))
                a_ref[2 * p + ci, d * c:(d + 1) * c, :] = jnp.where(causal[d], a, 0.0).astype(BF16)

    def intra_out_and_kv(p):
        rows2 = pair_rows(p)
        v2 = v_ref[0, rows2, :]
        vt2 = vt_ref[:, rows2]
        kt2 = jnp.concatenate([kt_ref[0, rows2, :], kt_ref[1, rows2, :]], axis=1)
        for ci in range(2):
            sl = slice(ci * c, (ci + 1) * c)
            start = p * pair + ci * c
            rows = pl.ds(start if isinstance(start, int) else pl.multiple_of(start, c), c)
            v_bd = jnp.where(blk_v, jnp.concatenate([v2[sl]] * GLA_HEADS, axis=0), 0.0).astype(BF16)
            o_both = _dot(a_ref[2 * p + ci], v_bd)
            acc_ref[rows, :] = o_both[0:c] + o_both[c:2 * c]
            gap = jnp.zeros((c, 2 * kc), F32)
            kt_pair = jnp.concatenate([kt2[sl], gap] if ci == 0 else [gap, kt2[sl]], axis=0).astype(BF16)
            kv = _dot(vt2, kt_pair)
            kv_ref[0, 2 * p + ci] = jnp.where(blk_k, kv[:, 0:kc], 0.0)
            kv_ref[1, 2 * p + ci] = jnp.where(blk_k, kv[:, kc:2 * kc], 0.0)

    decayed_qk(0)
    intra_scores(0)
    decayed_qk(1)

    def intra_step(p, carry):
        intra_out_and_kv(p)
        intra_scores(p + 1)
        decayed_qk(p + 2)
        return carry

    lax.fori_loop(0, n_pairs - 2, intra_step, 0, unroll=2)
    intra_out_and_kv(n_pairs - 2)
    intra_scores(n_pairs - 1)
    intra_out_and_kv(n_pairs - 1)

    st_ref[...] = jnp.zeros_like(st_ref)

    def scan_step(i, carry):
        for d in range(2):
            n = i if d == 0 else n_chunks - 1 - i
            rows = pl.ds(pl.multiple_of(n * c, c), c)
            st = st_ref[d]
            acc_ref[rows, :] += _dot_nt(qt_ref[d, rows, :], st.astype(BF16))
            st = (st + kv_ref[d, n]).reshape(GLA_WIDTH // SUBLANES, SUBLANES, kc) * dec_ref[d, n][None]
            st_ref[d] = st.reshape(GLA_WIDTH, kc)
        return carry

    lax.fori_loop(0, n_chunks, scan_step, 0, unroll=8)

    o = acc_ref[...]
    o2 = o * o
    o2_hi = o2.astype(BF16)
    o2_lo = (o2 - o2_hi.astype(F32)).astype(BF16)
    gi = lax.broadcasted_iota(jnp.int32, (GLA_WIDTH, GLA_WIDTH), 0) // GLA_V_DIM
    gj = lax.broadcasted_iota(jnp.int32, (GLA_WIDTH, GLA_WIDTH), 1) // GLA_V_DIM
    ones_bd = (gi == gj).astype(BF16)
    ms = (_dot(o2_hi, ones_bd) + _dot(o2_lo, ones_bd)) * (1.0 / GLA_V_DIM)
    r = r_ref[0]
    gate = r * (1.0 / (1.0 + jnp.exp(-r)))
    o_ref[0] = (o * lax.rsqrt(ms + LN_EPS) * g_ref[...] * gate).astype(BF16)


def _gla(gqk, gv, gr, gz, w2_bd, b2_cat, g_tiled):
    b, s, _ = gv.shape
    per_b = lambda bi: (bi, 0, 0)
    const = lambda bi: (0, 0)
    return pl.pallas_call(
        _gla_kernel,
        grid=(b,),
        in_specs=[
            pl.BlockSpec((1, s, 2 * GLA_K_COLS), per_b),
            pl.BlockSpec((1, s, GLA_WIDTH), per_b),
            pl.BlockSpec((1, s, GLA_WIDTH), per_b),
            pl.BlockSpec((1, s, 2 * GLA_GATE_RANK), per_b),
            pl.BlockSpec((2 * GLA_GATE_RANK, 2 * GLA_K_COLS), const),
            pl.BlockSpec((1, 2 * GLA_K_COLS), const),
            pl.BlockSpec((1, GLA_WIDTH), const),
        ],
        out_specs=pl.BlockSpec((1, s, GLA_WIDTH), per_b),
        out_shape=jax.ShapeDtypeStruct((b, s, GLA_WIDTH), BF16),
        scratch_shapes=[
            pltpu.VMEM((s, 2 * GLA_K_COLS), F32),
            pltpu.VMEM((GLA_WIDTH, s), BF16),
            pltpu.VMEM((s, GLA_WIDTH), F32),
            pltpu.VMEM((2, s, GLA_K_COLS), BF16),
            pltpu.VMEM((2, s, GLA_K_COLS), F32),
            pltpu.VMEM((s // GLA_CHUNK, 2 * GLA_CHUNK, GLA_WIDTH), BF16),
            pltpu.VMEM((2, s // GLA_CHUNK, GLA_WIDTH, GLA_K_COLS), F32),
            pltpu.VMEM((2, s // GLA_CHUNK, SUBLANES, GLA_K_COLS), F32),
            pltpu.VMEM((2, GLA_WIDTH, GLA_K_COLS), F32),
        ],
        compiler_params=_params(1),
        name="gla",
    )(gqk, gv, gr, gz, w2_bd, b2_cat, g_tiled)


def _layer_norm(y, g, b):
    mu = jnp.mean(y, axis=1, keepdims=True)
    d = y - mu
    var = jnp.mean(d * d, axis=1, keepdims=True)
    return d * lax.rsqrt(var + LN_EPS) * g + b


ROW_SUB = 256


def _outproj_kernel(od_ref, of_ref, og_ref, x_ref, w_ref, g_ref, b_ref, o_ref):
    w = w_ref[...].astype(BF16)
    for r in range(x_ref.shape[0] // ROW_SUB):
        rows = slice(r * ROW_SUB, (r + 1) * ROW_SUB)
        mixed = jnp.concatenate([od_ref[0, hd, rows, :] for hd in range(DIFF_HEADS)]
                                + [of_ref[rows, :], og_ref[rows, :]], axis=1)
        m = _dot(mixed, w)
        o_ref[rows, :] = _layer_norm(DEEPNORM_ALPHA * x_ref[rows, :] + m, g_ref[...], b_ref[...])


def _out_proj(o_diff, o_four, o_gla, x2d, w_all, layer, g, b, tm=2048):
    m = x2d.shape[0]
    seq = o_diff.shape[2]
    n_seq_tiles = seq // tm
    row = lambda i: (i, 0)
    const = lambda i: (0, 0)
    return pl.pallas_call(
        _outproj_kernel,
        grid=(m // tm,),
        in_specs=[
            pl.BlockSpec((1, DIFF_HEADS, tm, LANES), lambda i: (i // n_seq_tiles, 0, i % n_seq_tiles, 0)),
            pl.BlockSpec((tm, FOURIER_WIDTH), row),
            pl.BlockSpec((tm, GLA_WIDTH), row),
            pl.BlockSpec((tm, D_MODEL), row),
            _layer_weight(w_all, layer),
            pl.BlockSpec((1, D_MODEL), const),
            pl.BlockSpec((1, D_MODEL), const),
        ],
        out_specs=pl.BlockSpec((tm, D_MODEL), row),
        out_shape=jax.ShapeDtypeStruct((m, D_MODEL), F32),
        compiler_params=_params(1),
        name="out_proj",
    )(o_diff, o_four, o_gla, x2d, w_all, g, b)


FFN_CHUNK = 256


FFN_ROWS = 512


def _ffn_kernel(x_ref, wg_ref, wu_ref, wd_ref, g_ref, b_ref, o_ref):
    sub_tiles = [slice(r * FFN_ROWS, (r + 1) * FFN_ROWS) for r in range(x_ref.shape[0] // FFN_ROWS)]
    xbs = [x_ref[rows, :].astype(BF16) for rows in sub_tiles]
    for ci in range(FFN_HIDDEN // FFN_CHUNK):
        cols = slice(ci * FFN_CHUNK, (ci + 1) * FFN_CHUNK)
        wg = wg_ref[:, cols].astype(BF16)
        wu = wu_ref[:, cols].astype(BF16)
        wd = wd_ref[cols, :].astype(BF16)
        for rows, xb in zip(sub_tiles, xbs):
            hg = _dot(xb, wg)
            hu = _dot(xb, wu)
            act = (hg * (1.0 / (1.0 + jnp.exp(-hg))) * hu).astype(BF16)
            part = _dot(act, wd)
            if ci == 0:
                o_ref[rows, :] = part
            else:
                o_ref[rows, :] += part
    for rows in sub_tiles:
        o_ref[rows, :] = _layer_norm(DEEPNORM_ALPHA * x_ref[rows, :] + o_ref[rows, :], g_ref[...], b_ref[...])


def _ffn(x2d, wg_all, wu_all, wd_all, layer, g, b, tm=1024):
    m = x2d.shape[0]
    row = lambda i: (i, 0)
    const = lambda i: (0, 0)
    return pl.pallas_call(
        _ffn_kernel,
        grid=(m // tm,),
        in_specs=[
            pl.BlockSpec((tm, D_MODEL), row),
            _layer_weight(wg_all, layer),
            _layer_weight(wu_all, layer),
            _layer_weight(wd_all, layer),
            pl.BlockSpec((1, D_MODEL), const),
            pl.BlockSpec((1, D_MODEL), const),
        ],
        out_specs=pl.BlockSpec((tm, D_MODEL), row),
        out_shape=jax.ShapeDtypeStruct((m, D_MODEL), F32),
        compiler_params=_params(1, FFN_VMEM_LIMIT),
        name="ffn",
    )(x2d, wg_all, wu_all, wd_all, g, b)


def _rope_tables(seq):
    half = DIFF_QK_DIM // 2
    pos = jnp.arange(seq, dtype=F32)
    inv_freq = ROPE_THETA ** (-jnp.arange(0, DIFF_QK_DIM, 2, dtype=F32) / DIFF_QK_DIM)
    ang = pos[:, None] * inv_freq[None, :]
    cos, sin = jnp.cos(ang), jnp.sin(ang)
    reps = LANES // DIFF_QK_DIM
    cos128 = jnp.tile(jnp.concatenate([cos, cos], axis=1), (1, reps))
    sin128 = jnp.tile(jnp.concatenate([-sin, sin], axis=1), (1, reps))
    assert cos128.shape == (seq, LANES) and half * 2 == DIFF_QK_DIM
    return cos128, sin128, cos128.T, sin128.T


def _dft_tables(seq):
    half = seq // 2
    j = np.arange(half, dtype=np.int64)[:, None]
    pos = np.arange(half, dtype=np.int64)[None, :]
    ang_even = 2.0 * np.pi * ((j * pos) % half).astype(np.float64) / half
    ang_odd = 2.0 * np.pi * ((j * (2 * pos + 1)) % seq).astype(np.float64) / seq
    f_even = np.concatenate([np.cos(ang_even), -np.sin(ang_even)], axis=1)
    f_odd = np.concatenate([np.cos(ang_odd), -np.sin(ang_odd)], axis=1)
    c = np.arange(FOURIER_GROUP_DIM, dtype=np.int64)
    ang_c = 2.0 * np.pi * ((c[:, None] * c[None, :]) % FOURIER_GROUP_DIM).astype(np.float64) / FOURIER_GROUP_DIM
    eye = np.eye(FOURIER_GROUPS)
    cc_bd = np.kron(eye, np.cos(ang_c))
    sc_bd = np.kron(eye, np.sin(ang_c))
    return (jnp.asarray(f_even, dtype=F32), jnp.asarray(f_odd, dtype=F32),
            jnp.asarray(cc_bd, dtype=F32).astype(BF16), jnp.asarray(sc_bd, dtype=F32).astype(BF16))


def _block_diag(blocks):
    n = len(blocks)
    rows = []
    for i, blk in enumerate(blocks):
        rows.append(jnp.concatenate(
            [blk if j == i else jnp.zeros((blk.shape[0], blocks[j].shape[1]), blk.dtype) for j in range(n)],
            axis=1))
    return jnp.concatenate(rows, axis=0)


def kernel(x, w_in, diff_lambda, diff_norm_g, fourier_w, gla_gate_w2, gla_gate_b2, gla_norm_g, w_out,
           ln1_g, ln1_b, ffn_w_gate, ffn_w_up, ffn_w_down, ln2_g, ln2_b):
    b, s, d = x.shape
    m = b * s
    rope = _rope_tables(s)
    f_even, f_odd, cc_bd, sc_bd = _dft_tables(s)
    x2d = x.reshape(m, d)
    for l in range(DEPTH):
        lam_init = 0.8 - 0.6 * math.exp(-0.3 * l)
        wqv_t = jnp.concatenate([w_in[l, :, OFF_DQ:OFF_DK], w_in[l, :, OFF_DV:OFF_FU]], axis=1).T.astype(BF16)
        w_rest = jnp.concatenate([w_in[l, :, OFF_DK:OFF_DV], w_in[l, :, OFF_FU:]], axis=1).astype(BF16)
        qt, k, vt, fu, gqk, gv, gr, gz = _in_proj(x2d, wqv_t, w_rest, rope, s)
        o_diff = _diff_attention(qt, k, vt, diff_lambda[l], diff_norm_g[l].reshape(1, -1), lam_init)
        w_four = _block_diag([fourier_w[l, g] for g in range(FOURIER_GROUPS)]).astype(BF16)
        o_four = _fourier(fu.reshape(-1, b, s, LANES), cc_bd, sc_bd, f_even, f_odd, w_four)
        w2_bd = _block_diag([gla_gate_w2[l, 0], gla_gate_w2[l, 1]]).astype(BF16)
        b2_cat = gla_gate_b2[l].reshape(1, -1)
        g_gla = jnp.tile(gla_norm_g[l], GLA_HEADS).reshape(1, -1)
        o_gla = _gla(gqk.reshape(b, s, -1), gv.reshape(b, s, -1), gr.reshape(b, s, -1),
                     gz.reshape(b, s, -1), w2_bd, b2_cat, g_gla)
        x2d = _out_proj(o_diff, o_four.reshape(m, -1), o_gla.reshape(m, -1), x2d,
                        w_out, l, ln1_g[l].reshape(1, -1), ln1_b[l].reshape(1, -1))
        x2d = _ffn(x2d, ffn_w_gate, ffn_w_up, ffn_w_down, l,
                   ln2_g[l].reshape(1, -1), ln2_b[l].reshape(1, -1))
    return x2d.reshape(b, s, d)
```
